```python
import math
import jax, jax.numpy as jnp
from jax import lax
import numpy as np

D_MODEL = 2048
BATCH = 2
SEQ = 4096
DEPTH = 4
DEC_BATCH = 8
DEC_SEQ = 1
PAST_LEN = 16384
PAGE_SIZE = 128

HEAD_DIM = 128
N_EVEN = (DEPTH + 1) // 2
N_ODD = DEPTH // 2
NSA_H = 8
NSA_KVH = 2
NSA_G = NSA_H // NSA_KVH
CMP_BLOCK = 64
TOP_N = 16
WINDOW = 512
NSA_Q_BLOCK = 64
FORCE_SCORE = 1.0e4
DIFF_H = 4
DIFF_VD = 2 * HEAD_DIM
Q_BLOCK = 128
RET_H = 8
RET_DK = D_MODEL // RET_H
RET_DV = 2 * D_MODEL // RET_H
RET_CHUNK = 128
D_FF = ((8 * D_MODEL + 3 * 256 - 1) // (3 * 256)) * 256
NORM_EPS = 1e-6
NEG_INF = -1e30

NSA_Q_W = NSA_H * HEAD_DIM
NSA_KV_W = NSA_KVH * HEAD_DIM
NSA_GATE_W = NSA_H * 3
DIFF_Q_W = DIFF_H * 2 * HEAD_DIM
DIFF_V_W = DIFF_H * DIFF_VD
EVEN_IN_W = NSA_Q_W + 6 * NSA_KV_W + NSA_GATE_W + 2 * DIFF_Q_W + DIFF_V_W
MIX_W = NSA_H * HEAD_DIM + DIFF_H * DIFF_VD
RET_IN_W = 2 * RET_H * RET_DK + 2 * RET_H * RET_DV

kernel_name = "hybrid_nsa_diff_retention_step"


def rmsnorm(x, g):
    xf = x.astype(jnp.float32)
    y = xf * lax.rsqrt(jnp.mean(xf * xf, axis=-1, keepdims=True) + NORM_EPS)
    return (y * g.astype(jnp.float32)).astype(x.dtype)


def alibi_slopes(n):
    return jnp.asarray([2.0 ** (-8.0 * (i + 1) / n) for i in range(n)], dtype=jnp.float32)


def swiglu(x, w13, w2):
    a, b = jnp.split(x @ w13, 2, axis=-1)
    return (jax.nn.silu(a) * b) @ w2


def gather_pages(pool, layer, page_table):
    g = pool[layer, page_table]
    return g.reshape(page_table.shape[0], -1, *pool.shape[3:])


def split_even(proj):
    B, T, _ = proj.shape
    sizes = [NSA_Q_W] + [NSA_KV_W] * 6 + [NSA_GATE_W, DIFF_Q_W, DIFF_Q_W, DIFF_V_W]
    offs = [int(o) for o in np.cumsum(sizes)[:-1]]
    p = jnp.split(proj, offs, axis=-1)
    q_n = p[0].reshape(B, T, NSA_H, HEAD_DIM)
    kv = [t.reshape(B, T, NSA_KVH, HEAD_DIM) for t in p[1:7]]
    gates = jax.nn.sigmoid(p[7].astype(jnp.float32)).reshape(B, T, NSA_H, 3)
    q_d = p[8].reshape(B, T, DIFF_H, 2, HEAD_DIM)
    k_d = p[9].reshape(B, T, DIFF_H, 2 * HEAD_DIM)
    v_d = p[10].reshape(B, T, DIFF_H, DIFF_VD)
    return q_n, kv, gates, q_d, k_d, v_d


def nsa_context(cmp_all, sel_all, cmp_pos):
    B, L = cmp_all.shape[:2]
    ncb = L // CMP_BLOCK
    w = jax.nn.softmax(cmp_pos.astype(jnp.float32), axis=1).astype(cmp_all.dtype)
    blk = cmp_all[:, :ncb * CMP_BLOCK].reshape(B, ncb, CMP_BLOCK, 2, NSA_KVH, HEAD_DIM)
    kc = jnp.einsum('bnjgd,jg->bngd', blk[:, :, :, 0], w[0])
    vc = jnp.einsum('bnjgd,jg->bngd', blk[:, :, :, 1], w[1])
    nsb = -(-L // CMP_BLOCK)
    sp = jnp.pad(sel_all, ((0, 0), (0, nsb * CMP_BLOCK - L), (0, 0), (0, 0), (0, 0)))
    sp = sp.reshape(B, nsb, CMP_BLOCK, 2, NSA_KVH, HEAD_DIM).transpose(3, 0, 4, 1, 2, 5)
    return kc, vc, sp[0], sp[1]


def nsa_attend(q, qpos, gates, kc, vc, ksb, vsb, kw, vw, kwpos, slopes):
    B, Tq = q.shape[:2]
    f32 = jnp.float32
    scale = HEAD_DIM ** -0.5
    qg = q.reshape(B, Tq, NSA_KVH, NSA_G, HEAD_DIM)
    m = slopes.reshape(NSA_KVH, NSA_G)[None, :, :, None, None]
    ncb = kc.shape[1]
    bend = (jnp.arange(ncb) + 1) * CMP_BLOCK - 1
    cdist = qpos[:, None] - bend[None, :]
    cmask = cdist >= 0
    s = jnp.einsum('btghd,bngd->bghtn', qg, kc).astype(f32) * scale - m * cdist.astype(f32)
    s = jnp.where(cmask, s, NEG_INF)
    p_cmp = jax.nn.softmax(s, axis=-1) * cmask
    o_cmp = jnp.einsum('bghtn,bngd->btghd', p_cmp.astype(vc.dtype), vc)
    nsb = ksb.shape[2]
    imp = jnp.where(cmask, p_cmp.sum(axis=2), -1.0)
    imp = jnp.pad(imp, ((0, 0), (0, 0), (0, 0), (0, nsb - ncb)), constant_values=-1.0)
    blk = jnp.arange(nsb)
    forced = (blk[None, :] == (qpos // CMP_BLOCK)[:, None]) | (blk[None, :] == 0)
    imp = jnp.where(forced, FORCE_SCORE, imp)
    _, idx = lax.top_k(imp, min(TOP_N, nsb))
    bi = jnp.arange(B)[:, None, None, None]
    gi = jnp.arange(NSA_KVH)[None, :, None, None]
    ksel = ksb[bi, gi, idx]
    vsel = vsb[bi, gi, idx]
    kpos = idx[..., None] * CMP_BLOCK + jnp.arange(CMP_BLOCK)
    sdist = (qpos[None, None, :, None, None] - kpos)[:, :, None]
    s = jnp.einsum('btghd,bgtnjd->bghtnj', qg, ksel).astype(f32) * scale - m[..., None] * sdist.astype(f32)
    s = jnp.where(sdist >= 0, s, NEG_INF)
    p_sel = jax.nn.softmax(s, axis=(-2, -1))
    o_sel = jnp.einsum('bghtnj,bgtnjd->btghd', p_sel.astype(vsel.dtype), vsel)
    wdist = qpos[:, None] - kwpos[None, :]
    wmask = (wdist >= 0) & (wdist <= WINDOW) & (kwpos[None, :] >= 0)
    s = jnp.einsum('btghd,bsgd->bghts', qg, kw).astype(f32) * scale - m * wdist.astype(f32)
    s = jnp.where(wmask, s, NEG_INF)
    p_win = jax.nn.softmax(s, axis=-1)
    o_win = jnp.einsum('bghts,bsgd->btghd', p_win.astype(vw.dtype), vw)
    g = gates.reshape(B, Tq, NSA_KVH, NSA_G, 3)[..., None]
    o = g[..., 0, :] * o_cmp + g[..., 1, :] * o_sel + g[..., 2, :] * o_win
    return o.astype(q.dtype).reshape(B, Tq, NSA_H * HEAD_DIM)


def diff_lambda_value(lv, lam_init):
    lv = lv.astype(jnp.float32)
    return jnp.exp(jnp.sum(lv[0] * lv[1])) - jnp.exp(jnp.sum(lv[2] * lv[3])) + lam_init


def diff_attend(q, qpos, k, v, kpos, slopes, lam):
    dist = qpos[:, None] - kpos[None, :]
    s = jnp.einsum('bthcd,bshcd->bhcts', q, k).astype(jnp.float32) * (HEAD_DIM ** -0.5)
    s = s - slopes[None, :, None, None, None] * dist.astype(jnp.float32)
    s = jnp.where(dist >= 0, s, NEG_INF)
    p = jax.nn.softmax(s, axis=-1)
    a = p[:, :, 0] - lam * p[:, :, 1]
    return jnp.einsum('bhts,bshe->bthe', a.astype(v.dtype), v)


def even_out(o_nsa, o_diff, dnorm, lam_init, w_out):
    B, T = o_nsa.shape[:2]
    od = rmsnorm(o_diff, dnorm) * (1.0 - lam_init)
    return jnp.concatenate([o_nsa, od.reshape(B, T, DIFF_H * DIFF_VD)], axis=-1) @ w_out


def even_layer_prompt(h, w_in, w_out, cmp_pos, lam_vec, dnorm, lam_init):
    B, S, _ = h.shape
    q_n, kv, gates, q_d, k_d, v_d = split_even(h @ w_in)
    cmp_rows = jnp.stack(kv[0:2], axis=2)
    sel_rows = jnp.stack(kv[2:4], axis=2)
    win_rows = jnp.stack(kv[4:6], axis=2)
    diff_rows = jnp.stack([k_d, v_d], axis=2)
    kc, vc, ksb, vsb = nsa_context(cmp_rows, sel_rows, cmp_pos)
    win_pad = jnp.pad(win_rows, ((0, 0), (WINDOW, 0), (0, 0), (0, 0), (0, 0)))
    nsa_sl = alibi_slopes(NSA_H)

    def nsa_blk(i):
        q0 = i * NSA_Q_BLOCK
        wb = lax.dynamic_slice_in_dim(win_pad, q0, WINDOW + NSA_Q_BLOCK, axis=1)
        return nsa_attend(lax.dynamic_slice_in_dim(q_n, q0, NSA_Q_BLOCK, axis=1), q0 + jnp.arange(NSA_Q_BLOCK),
                          lax.dynamic_slice_in_dim(gates, q0, NSA_Q_BLOCK, axis=1), kc, vc, ksb, vsb,
                          wb[:, :, 0], wb[:, :, 1], q0 - WINDOW + jnp.arange(WINDOW + NSA_Q_BLOCK), nsa_sl)

    o_nsa = lax.map(nsa_blk, jnp.arange(S // NSA_Q_BLOCK))
    o_nsa = jnp.swapaxes(o_nsa, 0, 1).reshape(B, S, NSA_H * HEAD_DIM)
    lam = diff_lambda_value(lam_vec, lam_init)
    k_all = k_d.reshape(B, S, DIFF_H, 2, HEAD_DIM)
    kpos = jnp.arange(S)
    diff_sl = alibi_slopes(DIFF_H)

    def diff_blk(i):
        q0 = i * Q_BLOCK
        return diff_attend(lax.dynamic_slice_in_dim(q_d, q0, Q_BLOCK, axis=1), q0 + jnp.arange(Q_BLOCK),
                           k_all, v_d, kpos, diff_sl, lam)

    o_diff = lax.map(diff_blk, jnp.arange(S // Q_BLOCK))
    o_diff = jnp.swapaxes(o_diff, 0, 1).reshape(B, S, DIFF_H, DIFF_VD)
    y = even_out(o_nsa, o_diff, dnorm, lam_init, w_out)
    return y, cmp_rows, sel_rows, win_rows[:, -min(WINDOW, S):], diff_rows


def even_layer_sample(h, past_cmp, past_sel, win_buf, past_diff, w_in, w_out, cmp_pos, lam_vec, dnorm, lam_init):
    B, T, _ = h.shape
    P = past_cmp.shape[1]
    L = P + T
    q_n, kv, gates, q_d, k_d, v_d = split_even(h @ w_in)
    cmp_rows = jnp.stack(kv[0:2], axis=2)
    sel_rows = jnp.stack(kv[2:4], axis=2)
    win_rows = jnp.stack(kv[4:6], axis=2)
    diff_rows = jnp.stack([k_d, v_d], axis=2)
    kc, vc, ksb, vsb = nsa_context(jnp.concatenate([past_cmp, cmp_rows], axis=1),
                                   jnp.concatenate([past_sel, sel_rows], axis=1), cmp_pos)
    w_buf = win_buf.shape[1]
    win_all = jnp.concatenate([win_buf, win_rows], axis=1)
    band = jnp.pad(win_all, ((0, 0), (WINDOW - w_buf, 0), (0, 0), (0, 0), (0, 0)))
    qpos = P + jnp.arange(T)
    o_nsa = nsa_attend(q_n, qpos, gates, kc, vc, ksb, vsb, band[:, :, 0], band[:, :, 1],
                       P - WINDOW + jnp.arange(WINDOW + T), alibi_slopes(NSA_H))
    lam = diff_lambda_value(lam_vec, lam_init)
    diff_all = jnp.concatenate([past_diff, diff_rows], axis=1)
    o_diff = diff_attend(q_d, qpos, diff_all[:, :, 0].reshape(B, L, DIFF_H, 2, HEAD_DIM), diff_all[:, :, 1],
                         jnp.arange(L), alibi_slopes(DIFF_H), lam)
    y = even_out(o_nsa, o_diff, dnorm, lam_init, w_out)
    return y, cmp_rows, sel_rows, win_all[:, -min(WINDOW, w_buf + T):], diff_rows


def ret_log_decay():
    return jnp.log1p(-jnp.exp2(-5.0 - jnp.arange(RET_H, dtype=jnp.float32)))


def ret_project(h, w_in):
    B, T, _ = h.shape
    hk = RET_H * RET_DK
    q, k, v, g = jnp.split(h @ w_in, [hk, 2 * hk, 2 * hk + RET_H * RET_DV], axis=-1)
    q = q.reshape(B, T, RET_H, RET_DK).astype(jnp.float32)
    k = k.reshape(B, T, RET_H, RET_DK).astype(jnp.float32) * (RET_DK ** -0.5)
    v = v.reshape(B, T, RET_H, RET_DV).astype(jnp.float32)
    return q, k, v, g


def retention_chunk(state, q, k, v, log_g):
    C = q.shape[1]
    i = jnp.arange(C, dtype=jnp.float32)
    d = i[:, None] - i[None, :]
    decay = jnp.where(d >= 0, jnp.exp(log_g[:, None, None] * jnp.maximum(d, 0.0)), 0.0)
    s = jnp.einsum('bihd,bjhd->bhij', q, k) * decay[None]
    o = jnp.einsum('bhij,bjhe->bihe', s, v)
    q_dec = q * jnp.exp(log_g[None, None, :, None] * (i + 1.0)[None, :, None, None])
    o = o + jnp.einsum('bihd,bhde->bihe', q_dec, state)
    k_dec = k * jnp.exp(log_g[None, None, :, None] * (C - 1.0 - i)[None, :, None, None])
    new_state = state * jnp.exp(log_g * C)[None, :, None, None] + jnp.einsum('bjhd,bjhe->bhde', k_dec, v)
    return new_state, o


def ret_output(o, g, rnorm, w_out, dtype):
    B, T = o.shape[:2]
    o = rmsnorm(o, rnorm).reshape(B, T, RET_H * RET_DV).astype(dtype)
    return (jax.nn.silu(g) * o) @ w_out


def ret_layer_prompt(h, w_in, rnorm, w_out):
    B, S, _ = h.shape
    q, k, v, g = ret_project(h, w_in)
    nc = S // RET_CHUNK
    log_g = ret_log_decay()

    def chunks(t):
        return jnp.swapaxes(t.reshape(B, nc, RET_CHUNK, *t.shape[2:]), 0, 1)

    def step(st, xs):
        return retention_chunk(st, xs[0], xs[1], xs[2], log_g)

    st0 = jnp.zeros((B, RET_H, RET_DK, RET_DV), jnp.float32)
    st, o = lax.scan(step, st0, (chunks(q), chunks(k), chunks(v)))
    o = jnp.swapaxes(o, 0, 1).reshape(B, S, RET_H, RET_DV)
    return ret_output(o, g, rnorm, w_out, h.dtype), st


def ret_layer_sample(h, state, w_in, rnorm, w_out):
    q, k, v, g = ret_project(h, w_in)
    st, o = retention_chunk(state.astype(jnp.float32), q, k, v, ret_log_decay())
    return ret_output(o, g, rnorm, w_out, h.dtype), st


def setup_inputs(seed: int = 0) -> dict:
    key = jax.random.key(seed)
    ks = jax.random.split(key, 24)
    f32 = jnp.float32
    n_pages = PAST_LEN // PAGE_SIZE
    n_pool = (5 * DEC_BATCH * n_pages + 3) // 4
    w_buf = min(WINDOW, PAST_LEN)

    def nrm(k, shape, scale):
        return jax.random.normal(k, shape, f32) * scale

    return {
        "x_prompt": nrm(ks[0], (BATCH, SEQ, D_MODEL), 1.0),
        "x_sample": nrm(ks[1], (DEC_BATCH, DEC_SEQ, D_MODEL), 1.0),
        "cache_nsa_cmp": nrm(ks[2], (N_EVEN, n_pool, PAGE_SIZE, 2, NSA_KVH, HEAD_DIM), 1.0),
        "cache_nsa_sel": nrm(ks[3], (N_EVEN, n_pool, PAGE_SIZE, 2, NSA_KVH, HEAD_DIM), 1.0),
        "cache_nsa_win": nrm(ks[4], (N_EVEN, DEC_BATCH, w_buf, 2, NSA_KVH, HEAD_DIM), 1.0),
        "cache_diff": nrm(ks[5], (N_EVEN, n_pool, PAGE_SIZE, 2, DIFF_H, 2 * HEAD_DIM), 1.0),
        "state_ret": nrm(ks[6], (N_ODD, DEC_BATCH, RET_H, RET_DK, RET_DV), 0.5),
        "page_table": jax.random.permutation(ks[7], n_pool)[:DEC_BATCH * n_pages].reshape(DEC_BATCH, n_pages).astype(jnp.int32),
        "norm_mix": 1.0 + nrm(ks[8], (DEPTH, D_MODEL), 0.05),
        "norm_ffn": 1.0 + nrm(ks[9], (DEPTH, D_MODEL), 0.05),
        "norm_final": 1.0 + nrm(ks[10], (D_MODEL,), 0.05),
        "even_w_in": nrm(ks[11], (N_EVEN, D_MODEL, EVEN_IN_W), D_MODEL ** -0.5),
        "even_w_out": nrm(ks[12], (N_EVEN, MIX_W, D_MODEL), MIX_W ** -0.5),
        "nsa_cmp_pos": nrm(ks[13], (N_EVEN, 2, CMP_BLOCK, NSA_KVH), 0.1),
        "diff_lambda": nrm(ks[14], (N_EVEN, 4, HEAD_DIM), 0.1),
        "diff_norm": 1.0 + nrm(ks[15], (N_EVEN, DIFF_VD), 0.05),
        "ret_w_in": nrm(ks[16], (N_ODD, D_MODEL, RET_IN_W), D_MODEL ** -0.5),
        "ret_norm": 1.0 + nrm(ks[17], (N_ODD, RET_H, RET_DV), 0.05),
        "ret_w_out": nrm(ks[18], (N_ODD, RET_H * RET_DV, D_MODEL), (RET_H * RET_DV) ** -0.5),
        "ffn_w13": nrm(ks[19], (DEPTH, D_MODEL, 2 * D_FF), D_MODEL ** -0.5),
        "ffn_w2": nrm(ks[20], (DEPTH, D_FF, D_MODEL), D_FF ** -0.5),
    }


def reference(x_prompt, x_sample, cache_nsa_cmp, cache_nsa_sel, cache_nsa_win, cache_diff, state_ret, page_table,
              norm_mix, norm_ffn, norm_final, even_w_in, even_w_out, nsa_cmp_pos, diff_lambda, diff_norm,
              ret_w_in, ret_norm, ret_w_out, ffn_w13, ffn_w2):
    xp, xs = x_prompt, x_sample
    pc, psl, pw, pd, pr = [], [], [], [], []
    sc, ssl, sw, sd, sr = [], [], [], [], []
    for layer in range(DEPTH):
        hp = rmsnorm(xp, norm_mix[layer])
        hs = rmsnorm(xs, norm_mix[layer])
        if layer % 2 == 0:
            e = layer // 2
            lam_init = 0.8 - 0.6 * math.exp(-0.3 * layer)
            mp, c_p, s_p, w_p, d_p = even_layer_prompt(hp, even_w_in[e], even_w_out[e], nsa_cmp_pos[e],
                                                       diff_lambda[e], diff_norm[e], lam_init)
            ms, c_s, s_s, w_s, d_s = even_layer_sample(
                hs, gather_pages(cache_nsa_cmp, e, page_table), gather_pages(cache_nsa_sel, e, page_table),
                cache_nsa_win[e], gather_pages(cache_diff, e, page_table), even_w_in[e], even_w_out[e],
                nsa_cmp_pos[e], diff_lambda[e], diff_norm[e], lam_init)
            pc.append(c_p); psl.append(s_p); pw.append(w_p); pd.append(d_p)
            sc.append(c_s); ssl.append(s_s); sw.append(w_s); sd.append(d_s)
        else:
            o = layer // 2
            mp, st_p = ret_layer_prompt(hp, ret_w_in[o], ret_norm[o], ret_w_out[o])
            ms, st_s = ret_layer_sample(hs, state_ret[o], ret_w_in[o], ret_norm[o], ret_w_out[o])
            pr.append(st_p); sr.append(st_s)
        xp = xp + mp
        xs = xs + ms
        xp = xp + swiglu(rmsnorm(xp, norm_ffn[layer]), ffn_w13[layer], ffn_w2[layer])
        xs = xs + swiglu(rmsnorm(xs, norm_ffn[layer]), ffn_w13[layer], ffn_w2[layer])
    y_prompt = rmsnorm(xp, norm_final)
    y_sample = rmsnorm(xs, norm_final)
    return (y_prompt, y_sample, jnp.stack(pc), jnp.stack(psl), jnp.stack(pw), jnp.stack(pd), jnp.stack(pr),
            jnp.stack(sc), jnp.stack(ssl), jnp.stack(sw), jnp.stack(sd), jnp.stack(sr))
```

```python
import functools
import math

import jax
import jax.numpy as jnp
import numpy as np
from jax import lax
from jax.experimental import pallas as pl
from jax.experimental.pallas import tpu as pltpu

F32 = jnp.float32
BF16 = jnp.bfloat16

D_MODEL = 2048
DEPTH = 4
PAGE_SIZE = 128
HEAD_DIM = 128
NSA_H = 8
NSA_KVH = 2
NSA_G = NSA_H // NSA_KVH
CMP_BLOCK = 64
CMP_SHIFT = CMP_BLOCK.bit_length() - 1
TOP_N = 16
WINDOW = 512
FORCE_SCORE = 1.0e4
DIFF_H = 4
DIFF_VD = 2 * HEAD_DIM
RET_H = 8
RET_DK = D_MODEL // RET_H
RET_DV = 2 * D_MODEL // RET_H
RET_CHUNK = 128
D_FF = ((8 * D_MODEL + 3 * 256 - 1) // (3 * 256)) * 256
NORM_EPS = 1e-6
NEG_INF = -1e30
ATT_SCALE = HEAD_DIM ** -0.5

NSA_Q_W = NSA_H * HEAD_DIM
NSA_KV_W = NSA_KVH * HEAD_DIM
NSA_GATE_W = NSA_H * 3
DIFF_Q_W = DIFF_H * 2 * HEAD_DIM
DIFF_V_W = DIFF_H * DIFF_VD
MAIN_W = NSA_Q_W + 6 * NSA_KV_W + 2 * DIFF_Q_W + DIFF_V_W
GATE_OFF = NSA_Q_W + 6 * NSA_KV_W

LANES = 128
SUBLANES = 8
VMEM_LIMIT = 48 * 1024 * 1024


def _cparams(sem):
    return pltpu.CompilerParams(dimension_semantics=sem, vmem_limit_bytes=VMEM_LIMIT)


def _rmsnorm_kernel(x_ref, g_ref, o_ref):
    x = x_ref[...]
    y = x * lax.rsqrt(jnp.mean(x * x, axis=-1, keepdims=True) + NORM_EPS)
    o_ref[...] = (y * g_ref[...]).astype(o_ref.dtype)


def rmsnorm(x, g, out_dtype):
    m, d = x.shape
    tm = min(512, m)
    return pl.pallas_call(
        _rmsnorm_kernel,
        grid=(m // tm,),
        in_specs=[pl.BlockSpec((tm, d), lambda i: (i, 0)), pl.BlockSpec((1, d), lambda i: (0, 0))],
        out_specs=pl.BlockSpec((tm, d), lambda i: (i, 0)),
        out_shape=jax.ShapeDtypeStruct((m, d), out_dtype),
        compiler_params=_cparams(("parallel",)),
        name="rmsnorm",
    )(x, g.reshape(1, d))


def _mm_kernel(x_ref, w_ref, *refs, has_res, n_out):
    acc = jnp.dot(x_ref[...], w_ref[...], preferred_element_type=F32)
    if has_res:
        acc = refs[0][...] + acc
        refs = refs[1:]
    for o_ref in refs[:n_out]:
        o_ref[...] = acc.astype(o_ref.dtype)


def matmul(x, w, res=None, out_dtypes=(F32,), tm=1024, tn=512):
    m, k = x.shape
    n = w.shape[1]
    tm = min(tm, m)
    tn = min(tn, n)
    if k > 4096:
        tm = min(tm, 512)
    in_specs = [pl.BlockSpec((tm, k), lambda i, j: (i, 0)), pl.BlockSpec((k, tn), lambda i, j: (0, j))]
    args = [x, w]
    if res is not None:
        in_specs.append(pl.BlockSpec((tm, tn), lambda i, j: (i, j)))
        args.append(res)
    outs = pl.pallas_call(
        functools.partial(_mm_kernel, has_res=res is not None, n_out=len(out_dtypes)),
        grid=(m // tm, n // tn),
        in_specs=in_specs,
        out_specs=[pl.BlockSpec((tm, tn), lambda i, j: (i, j)) for _ in out_dtypes],
        out_shape=[jax.ShapeDtypeStruct((m, n), dt) for dt in out_dtypes],
        compiler_params=_cparams(("parallel", "parallel")),
        name="matmul",
    )(*args)
    return outs[0] if len(out_dtypes) == 1 else outs


def _swiglu_up_kernel(x_ref, w1_ref, w3_ref, o_ref):
    x = x_ref[...]
    a = jnp.dot(x, w1_ref[...], preferred_element_type=F32)
    b = jnp.dot(x, w3_ref[...], preferred_element_type=F32)
    o_ref[...] = (a * jax.nn.sigmoid(a) * b).astype(o_ref.dtype)


def swiglu_up(x, w13, tm=1024, tn=512):
    m, k = x.shape
    half = w13.shape[1] // 2
    tm = min(tm, m)
    nb = half // tn
    return pl.pallas_call(
        _swiglu_up_kernel,
        grid=(m // tm, nb),
        in_specs=[pl.BlockSpec((tm, k), lambda i, j: (i, 0)),
                  pl.BlockSpec((k, tn), lambda i, j: (0, j)),
                  pl.BlockSpec((k, tn), lambda i, j: (0, j + nb))],
        out_specs=pl.BlockSpec((tm, tn), lambda i, j: (i, j)),
        out_shape=jax.ShapeDtypeStruct((m, half), BF16),
        compiler_params=_cparams(("parallel", "parallel")),
        name="swiglu_up",
    )(x, w13, w13)


def _softmax_rows64(logits):
    e = jnp.exp(logits - jnp.max(logits, axis=-1, keepdims=True))
    reps = logits.shape[-1] // CMP_BLOCK
    return e * (reps / jnp.sum(e, axis=-1, keepdims=True))


def _compress_prompt_kernel(x_ref, pos_ref, o_ref, *, seq):
    w = _softmax_rows64(pos_ref[...])
    blk = lax.broadcasted_iota(jnp.int32, (LANES, seq), 0)
    key_blk = lax.broadcasted_iota(jnp.int32, (LANES, seq), 1) >> CMP_SHIFT
    onblk = blk == key_blk
    for c in range(4):
        wb = jnp.where(onblk, w[c:c + 1, :], 0.0).astype(BF16)
        o_ref[0, :, c * LANES:(c + 1) * LANES] = jnp.dot(
            wb, x_ref[:, c * LANES:(c + 1) * LANES], preferred_element_type=F32).astype(BF16)


def compress_prompt(proj_bf, pos_tiled, batch, seq):
    return pl.pallas_call(
        functools.partial(_compress_prompt_kernel, seq=seq),
        grid=(batch,),
        in_specs=[pl.BlockSpec((seq, 4 * LANES), lambda b: (b, NSA_Q_W // (4 * LANES))),
                  pl.BlockSpec((4, seq), lambda b: (0, 0))],
        out_specs=pl.BlockSpec((1, LANES, 4 * LANES), lambda b: (b, 0, 0)),
        out_shape=jax.ShapeDtypeStruct((batch, LANES, 4 * LANES), BF16),
        compiler_params=_cparams(("parallel",)),
        name="nsa_compress_prompt",
    )(proj_bf, pos_tiled)


def _flash_step(s, valid, v, m_ref, l_ref, acc_ref, idx):
    s = jnp.where(valid, s, NEG_INF)
    m_prev = m_ref[idx]
    m_new = jnp.maximum(m_prev, jnp.max(s, axis=-1, keepdims=True))
    alpha = jnp.exp(m_prev - m_new)
    p = jnp.exp(s - m_new)
    l_ref[idx] = alpha * l_ref[idx] + jnp.sum(p, axis=-1, keepdims=True)
    m_ref[idx] = m_new
    dv = v.shape[-1]
    a = alpha if dv == LANES else jnp.concatenate([alpha] * (dv // LANES), axis=-1)
    acc_ref[idx] = a * acc_ref[idx] + jnp.dot(p.astype(BF16), v, preferred_element_type=F32)


def _flash_result(l_ref, acc_ref, idx):
    l = l_ref[idx]
    dv = acc_ref.shape[-1]
    l = l if dv == LANES else jnp.concatenate([l] * (dv // LANES), axis=-1)
    return acc_ref[idx] / l


def _nsa_prompt_kernel(slopes_ref, q_ref, gate_ref, kc_ref, vc_ref, ks_ref, vs_ref, kw_ref, vw_ref,
                       o_ref, m_ref, l_ref, acc_ref, *, tq, n_blocks):
    g = pl.program_id(1)
    qi = pl.program_id(2)
    q0 = qi * tq
    row = lax.broadcasted_iota(jnp.int32, (tq, LANES), 0)
    col = lax.broadcasted_iota(jnp.int32, (tq, LANES), 1)
    qpos = q0 + row
    nt = (((1,), (1,)), ((), ()))

    kc = kc_ref[0]
    vc = vc_ref[0]
    cdist = qpos - (col * CMP_BLOCK + (CMP_BLOCK - 1))
    cmask = cdist >= 0
    cdist_f = cdist.astype(F32)
    imp = jnp.zeros((tq, LANES), F32)
    o_cmp = []
    for h in range(NSA_G):
        slope = slopes_ref[g * NSA_G + h]
        qh = q_ref[:, h * HEAD_DIM:(h + 1) * HEAD_DIM]
        s = lax.dot_general(qh, kc, nt, preferred_element_type=F32) * ATT_SCALE - slope * cdist_f
        s = jnp.where(cmask, s, NEG_INF)
        p = jnp.exp(s - jnp.max(s, axis=-1, keepdims=True))
        p = jnp.where(cmask, p / jnp.sum(p, axis=-1, keepdims=True), 0.0)
        imp = imp + p
        o_cmp.append(jnp.dot(p.astype(BF16), vc, preferred_element_type=F32))

    imp = jnp.where(cmask, imp, -1.0)
    forced = (col == (qpos >> CMP_SHIFT)) | (col == 0)
    imp = jnp.where(forced, FORCE_SCORE, imp)
    nblk = -(-n_blocks // SUBLANES) * SUBLANES
    imp_t = imp.T[:nblk]
    bi = lax.broadcasted_iota(jnp.int32, (nblk, tq), 0)
    rank = jnp.zeros((nblk, tq), jnp.int32)
    for j in range(nblk):
        r = imp_t[j:j + 1, :]
        tie = jnp.where(bi > j, 1, 0)
        rank = rank + jnp.where(r > imp_t, 1, jnp.where(r == imp_t, tie, 0))
    sel_t = jnp.where(rank < TOP_N, 1.0, 0.0)
    sel = jnp.concatenate([sel_t, jnp.zeros((LANES - nblk, tq), F32)], axis=0).T.astype(BF16)

    def init():
        m_ref[...] = jnp.full(m_ref.shape, NEG_INF, F32)
        l_ref[...] = jnp.zeros(l_ref.shape, F32)
        acc_ref[...] = jnp.zeros(acc_ref.shape, F32)

    e_row = lax.broadcasted_iota(jnp.int32, (LANES, LANES), 0)
    e_col = lax.broadcasted_iota(jnp.int32, (LANES, LANES), 1)
    rel = lax.broadcasted_iota(jnp.int32, (1, LANES), 1)

    def branch(k_ref, v_ref, n_tiles, selected):
        init()

        def body(i, carry):
            kt = qi - i
            k0 = pl.multiple_of(kt * LANES, LANES)
            k = k_ref[pl.ds(k0, LANES), :]
            v = v_ref[pl.ds(k0, LANES), :]
            dist = qpos - (k0 + col)
            if selected:
                expand = jnp.where(((k0 + e_col) >> CMP_SHIFT) == e_row, 1.0, 0.0).astype(BF16)
                picked = jnp.dot(sel, expand, preferred_element_type=F32) > 0.5
                valid = picked & (dist >= 0)
            else:
                valid = (dist >= 0) & (dist <= WINDOW)
            bias = (k0 - q0 + rel).astype(F32)
            for h in range(NSA_G):
                slope = slopes_ref[g * NSA_G + h]
                qh = q_ref[:, h * HEAD_DIM:(h + 1) * HEAD_DIM]
                s = lax.dot_general(qh, k, nt, preferred_element_type=F32) * ATT_SCALE + slope * bias
                _flash_step(s, valid, v, m_ref, l_ref, acc_ref, h)
            return carry

        lax.fori_loop(0, n_tiles, body, 0)
        return [_flash_result(l_ref, acc_ref, h) for h in range(NSA_G)]

    o_sel = branch(ks_ref, vs_ref, qi + 1, True)
    o_win = branch(kw_ref, vw_ref, jnp.minimum(qi, WINDOW // LANES) + 1, False)

    gates = jax.nn.sigmoid(gate_ref[...])
    for h in range(NSA_G):
        o = (gates[:, 3 * h:3 * h + 1] * o_cmp[h] + gates[:, 3 * h + 1:3 * h + 2] * o_sel[h]
             + gates[:, 3 * h + 2:3 * h + 3] * o_win[h])
        o_ref[:, h * HEAD_DIM:(h + 1) * HEAD_DIM] = o.astype(o_ref.dtype)


def nsa_prompt(proj_bf, gate_pre, kcvc, slopes, batch, seq, tq=128):
    nq = seq // tq
    gw = NSA_G * HEAD_DIM
    kv0 = NSA_Q_W // HEAD_DIM

    def kv_spec(slot):
        return pl.BlockSpec((seq, HEAD_DIM), lambda b, g, i: (b, kv0 + 2 * slot + g))

    return pl.pallas_call(
        functools.partial(_nsa_prompt_kernel, tq=tq, n_blocks=seq // CMP_BLOCK),
        grid=(batch, NSA_KVH, nq),
        in_specs=[pl.BlockSpec(memory_space=pltpu.SMEM),
                  pl.BlockSpec((tq, gw), lambda b, g, i: (b * nq + i, g)),
                  pl.BlockSpec((tq, LANES), lambda b, g, i: (b * nq + i, g)),
                  pl.BlockSpec((1, LANES, HEAD_DIM), lambda b, g, i: (b, 0, g)),
                  pl.BlockSpec((1, LANES, HEAD_DIM), lambda b, g, i: (b, 0, 2 + g)),
                  kv_spec(2), kv_spec(3), kv_spec(4), kv_spec(5)],
        out_specs=pl.BlockSpec((tq, gw), lambda b, g, i: (b * nq + i, g)),
        out_shape=jax.ShapeDtypeStruct((batch * seq, NSA_Q_W), BF16),
        scratch_shapes=[pltpu.VMEM((NSA_G, tq, LANES), F32), pltpu.VMEM((NSA_G, tq, LANES), F32),
                        pltpu.VMEM((NSA_G, tq, HEAD_DIM), F32)],
        compiler_params=_cparams(("parallel", "parallel", "arbitrary")),
        name="nsa_prompt",
    )(slopes, proj_bf, gate_pre, kcvc, kcvc, proj_bf, proj_bf, proj_bf, proj_bf)


def _diff_lambda(lv, lam_init):
    a = jnp.sum(lv[0:1] * lv[1:2], axis=-1, keepdims=True)
    b = jnp.sum(lv[2:3] * lv[3:4], axis=-1, keepdims=True)
    return jnp.exp(a) - jnp.exp(b) + lam_init


def _diff_prompt_kernel(slopes_ref, q_ref, k_ref, v_ref, lam_ref, dn_ref, o_ref, m_ref, l_ref, acc_ref,
                        *, tq, lam_init):
    h = pl.program_id(1)
    qi = pl.program_id(2)
    q0 = qi * tq
    row = lax.broadcasted_iota(jnp.int32, (tq, LANES), 0)
    col = lax.broadcasted_iota(jnp.int32, (tq, LANES), 1)
    rel = lax.broadcasted_iota(jnp.int32, (1, LANES), 1)
    qpos = q0 + row
    slope = slopes_ref[h]
    nt = (((1,), (1,)), ((), ()))
    m_ref[...] = jnp.full(m_ref.shape, NEG_INF, F32)
    l_ref[...] = jnp.zeros(l_ref.shape, F32)
    acc_ref[...] = jnp.zeros(acc_ref.shape, F32)

    def body(i, carry):
        kt = qi - i
        k0 = pl.multiple_of(kt * LANES, LANES)
        k = k_ref[pl.ds(k0, LANES), :]
        v = v_ref[pl.ds(k0, LANES), :]
        valid = qpos - (k0 + col) >= 0
        bias = slope * (k0 - q0 + rel).astype(F32)
        for c in range(2):
            s = lax.dot_general(q_ref[:, c * HEAD_DIM:(c + 1) * HEAD_DIM], k[:, c * HEAD_DIM:(c + 1) * HEAD_DIM],
                                nt, preferred_element_type=F32) * ATT_SCALE + bias
            _flash_step(s, valid, v, m_ref, l_ref, acc_ref, c)
        return carry

    lax.fori_loop(0, qi + 1, body, 0)
    lam = _diff_lambda(lam_ref[...], lam_init)
    o = _flash_result(l_ref, acc_ref, 0) - lam * _flash_result(l_ref, acc_ref, 1)
    y = o * lax.rsqrt(jnp.mean(o * o, axis=-1, keepdims=True) + NORM_EPS)
    o_ref[...] = (y * dn_ref[...] * (1.0 - lam_init)).astype(o_ref.dtype)


def diff_prompt(proj_bf, lam_vec, dnorm, slopes, lam_init, batch, seq, tq=128):
    nq = seq // tq
    w = 2 * HEAD_DIM
    qb = (NSA_Q_W + 6 * NSA_KV_W) // w
    kb = qb + DIFF_H
    vb = kb + DIFF_H
    return pl.pallas_call(
        functools.partial(_diff_prompt_kernel, tq=tq, lam_init=lam_init),
        grid=(batch, DIFF_H, nq),
        in_specs=[pl.BlockSpec(memory_space=pltpu.SMEM),
                  pl.BlockSpec((tq, w), lambda b, h, i: (b * nq + i, qb + h)),
                  pl.BlockSpec((seq, w), lambda b, h, i: (b, kb + h)),
                  pl.BlockSpec((seq, w), lambda b, h, i: (b, vb + h)),
                  pl.BlockSpec((4, HEAD_DIM), lambda b, h, i: (0, 0)),
                  pl.BlockSpec((1, DIFF_VD), lambda b, h, i: (0, 0))],
        out_specs=pl.BlockSpec((tq, w), lambda b, h, i: (b * nq + i, h)),
        out_shape=jax.ShapeDtypeStruct((batch * seq, DIFF_V_W), BF16),
        scratch_shapes=[pltpu.VMEM((2, tq, LANES), F32), pltpu.VMEM((2, tq, LANES), F32),
                        pltpu.VMEM((2, tq, DIFF_VD), F32)],
        compiler_params=_cparams(("parallel", "parallel", "arbitrary")),
        name="diff_prompt",
    )(slopes, proj_bf, proj_bf, proj_bf, lam_vec, dnorm.reshape(1, DIFF_VD))


def _ret_prompt_kernel(lg_ref, q_ref, k_ref, v_ref, g_ref, rn_ref, o_ref, st_ref, *, chunk):
    h = pl.program_id(1)
    c = pl.program_id(2)
    lg = lg_ref[h]

    @pl.when(c == 0)
    def _():
        st_ref[...] = jnp.zeros(st_ref.shape, F32)

    ii = lax.broadcasted_iota(jnp.int32, (chunk, chunk), 0)
    jj = lax.broadcasted_iota(jnp.int32, (chunk, chunk), 1)
    d = (ii - jj).astype(F32)
    decay = jnp.where(d >= 0, jnp.exp(lg * jnp.maximum(d, 0.0)), 0.0)
    ik = lax.broadcasted_iota(jnp.int32, (chunk, RET_DK), 0).astype(F32)
    q_dec = jnp.exp(lg * (ik + 1.0))
    k_dec = jnp.exp(lg * (chunk - 1.0 - ik))
    ones = jnp.ones((1, 1), F32)
    g_chunk = jnp.exp(ones * (lg * chunk))

    q = q_ref[...]
    k = k_ref[...] * (RET_DK ** -0.5)
    v = v_ref[...].astype(BF16)
    state = st_ref[0, 0]
    s = lax.dot_general(q.astype(BF16), k.astype(BF16), (((1,), (1,)), ((), ())),
                        preferred_element_type=F32) * decay
    o = jnp.dot(s.astype(BF16), v, preferred_element_type=F32)
    o = o + jnp.dot((q * q_dec).astype(BF16), state.astype(BF16), preferred_element_type=F32)
    kv = lax.dot_general((k * k_dec).astype(BF16), v, (((0,), (0,)), ((), ())), preferred_element_type=F32)
    st_ref[0, 0] = state * g_chunk + kv

    y = o * lax.rsqrt(jnp.mean(o * o, axis=-1, keepdims=True) + NORM_EPS) * rn_ref[0]
    gate = g_ref[...]
    o_ref[...] = (gate * jax.nn.sigmoid(gate) * y).astype(o_ref.dtype)


def ret_prompt(proj, rnorm, log_g, batch, seq, chunk=RET_CHUNK):
    nc = seq // chunk
    kb = RET_H
    vb = 2 * RET_H * RET_DK // RET_DV
    gb = vb + RET_H
    return pl.pallas_call(
        functools.partial(_ret_prompt_kernel, chunk=chunk),
        grid=(batch, RET_H, nc),
        in_specs=[pl.BlockSpec(memory_space=pltpu.SMEM),
                  pl.BlockSpec((chunk, RET_DK), lambda b, h, c: (b * nc + c, h)),
                  pl.BlockSpec((chunk, RET_DK), lambda b, h, c: (b * nc + c, kb + h)),
                  pl.BlockSpec((chunk, RET_DV), lambda b, h, c: (b * nc + c, vb + h)),
                  pl.BlockSpec((chunk, RET_DV), lambda b, h, c: (b * nc + c, gb + h)),
                  pl.BlockSpec((1, 1, RET_DV), lambda b, h, c: (h, 0, 0))],
        out_specs=[pl.BlockSpec((chunk, RET_DV), lambda b, h, c: (b * nc + c, h)),
                   pl.BlockSpec((1, 1, RET_DK, RET_DV), lambda b, h, c: (b, h, 0, 0))],
        out_shape=[jax.ShapeDtypeStruct((batch * seq, RET_H * RET_DV), BF16),
                   jax.ShapeDtypeStruct((batch, RET_H, RET_DK, RET_DV), F32)],
        compiler_params=_cparams(("parallel", "parallel", "arbitrary")),
        name="ret_prompt",
    )(log_g, proj, proj, proj, proj, rnorm.reshape(RET_H, 1, RET_DV))


NT_DIMS = (((1,), (1,)), ((), ()))


def _compress_sample_kernel(pt_ref, pos_ref, *refs, pp):
    del pt_ref
    o_ref = refs[pp]
    logits = pos_ref[...]
    e = jnp.exp(logits - jnp.max(logits, axis=0, keepdims=True))
    w = e * ((PAGE_SIZE // CMP_BLOCK) / jnp.sum(e, axis=0, keepdims=True))
    for i in range(pp):
        y = refs[i][0, 0] * w
        for half in range(PAGE_SIZE // CMP_BLOCK):
            r = i * (PAGE_SIZE // CMP_BLOCK) + half
            o_ref[0, r:r + 1, :] = jnp.sum(y[half * CMP_BLOCK:(half + 1) * CMP_BLOCK], axis=0, keepdims=True)


def compress_sample(cache, layer, page_table, pos_rows, pp=8):
    db, n_pages = page_table.shape
    per_page = PAGE_SIZE // CMP_BLOCK
    width = cache.shape[-1]

    def page_spec(i):
        return pl.BlockSpec((1, 1, PAGE_SIZE, width), lambda b, j, pt: (layer, pt[b, j * pp + i], 0, 0))

    return pl.pallas_call(
        functools.partial(_compress_sample_kernel, pp=pp),
        grid_spec=pltpu.PrefetchScalarGridSpec(
            num_scalar_prefetch=1,
            grid=(db, n_pages // pp),
            in_specs=[pl.BlockSpec((PAGE_SIZE, width), lambda b, j, pt: (0, 0))] + [page_spec(i) for i in range(pp)],
            out_specs=pl.BlockSpec((1, pp * per_page, width), lambda b, j, pt: (b, j, 0))),
        out_shape=jax.ShapeDtypeStruct((db, n_pages * per_page, width), F32),
        compiler_params=_cparams(("parallel", "arbitrary")),
        name="nsa_compress_sample",
    )(page_table, pos_rows, *([cache] * pp))


def _pick_group(x):
    row = lax.broadcasted_iota(jnp.int32, (NSA_H, HEAD_DIM), 0)
    return jnp.where(row < NSA_G, x[:, :HEAD_DIM], x[:, HEAD_DIM:])


def _nsa_decode_cmp_kernel(q_ref, slope_ref, kcvc_ref, win_ref, ocmp_ref, owin_ref, idx_ref, *, past, n_win):
    q8 = q_ref[0]
    kcvc = kcvc_ref[0]
    nb = kcvc.shape[0]
    kvw = NSA_KVH * HEAD_DIM
    lane = lax.broadcasted_iota(jnp.int32, (NSA_H, nb), 1)
    row = lax.broadcasted_iota(jnp.int32, (NSA_H, nb), 0)
    slope = slope_ref[...][:, :1]
    cdist = (past - (lane * CMP_BLOCK + CMP_BLOCK - 1)).astype(F32)
    s = lax.dot_general(q8, kcvc[:, :kvw].astype(BF16), NT_DIMS, preferred_element_type=F32) * ATT_SCALE
    s = s - slope * cdist
    p = jnp.exp(s - jnp.max(s, axis=-1, keepdims=True))
    p = p / jnp.sum(p, axis=-1, keepdims=True)
    ocmp_ref[0] = _pick_group(jnp.dot(p.astype(BF16), kcvc[:, kvw:].astype(BF16), preferred_element_type=F32))

    g0 = p[0:1] + p[1:2] + p[2:3] + p[3:4]
    g1 = p[4:5] + p[5:6] + p[6:7] + p[7:8]
    x = jnp.where(row < NSA_G, g0, g1)
    x = jnp.where(lane == 0, FORCE_SCORE, x)
    out_lane = lax.broadcasted_iota(jnp.int32, (NSA_H, LANES), 1)
    lane_f = lane.astype(F32)
    picked = jnp.zeros((NSA_H, LANES), F32)
    for t in range(TOP_N - 1):
        mx = jnp.max(x, axis=-1, keepdims=True)
        idx = jnp.min(jnp.where(x == mx, lane_f, float(nb)), axis=-1, keepdims=True)
        picked = jnp.where(out_lane == t, idx, picked)
        x = jnp.where(lane_f == idx, -2.0, x)
    idx_ref[0] = picked.astype(jnp.int32)

    win = win_ref[0]
    nw = win.shape[0]
    wl = lax.broadcasted_iota(jnp.int32, (NSA_H, nw), 1)
    wdist = (n_win - 1 - wl)
    valid = wdist >= 0
    s = lax.dot_general(q8, win[:, :kvw].astype(BF16), NT_DIMS, preferred_element_type=F32) * ATT_SCALE
    s = jnp.where(valid, s - slope * wdist.astype(F32), NEG_INF)
    p = jnp.exp(s - jnp.max(s, axis=-1, keepdims=True))
    p = p / jnp.sum(p, axis=-1, keepdims=True)
    owin_ref[0] = _pick_group(jnp.dot(p.astype(BF16), win[:, kvw:].astype(BF16), preferred_element_type=F32))


def nsa_decode_cmp(q8, slope8, kcvc, win_all, past, n_win):
    db = q8.shape[0]
    nb = kcvc.shape[1]
    nw = win_all.shape[1]
    w = kcvc.shape[2]
    head_out = jax.ShapeDtypeStruct((db, NSA_H, HEAD_DIM), F32)
    head_spec = pl.BlockSpec((1, NSA_H, HEAD_DIM), lambda b: (b, 0, 0))
    return pl.pallas_call(
        functools.partial(_nsa_decode_cmp_kernel, past=past, n_win=n_win),
        grid=(db,),
        in_specs=[pl.BlockSpec((1, NSA_H, NSA_KVH * HEAD_DIM), lambda b: (b, 0, 0)),
                  pl.BlockSpec((NSA_H, LANES), lambda b: (0, 0)),
                  pl.BlockSpec((1, nb, w), lambda b: (b, 0, 0)),
                  pl.BlockSpec((1, nw, w), lambda b: (b, 0, 0))],
        out_specs=[head_spec, head_spec, pl.BlockSpec((1, NSA_H, LANES), lambda b: (b, 0, 0))],
        out_shape=[head_out, head_out, jax.ShapeDtypeStruct((db, NSA_H, LANES), jnp.int32)],
        compiler_params=_cparams(("parallel",)),
        name="nsa_decode_cmp",
    )(q8, slope8, kcvc, win_all)


def _nsa_decode_sel_kernel(pt_ref, ids_ref, q_ref, slope_ref, new_ref, pa_ref, pb_ref, ocmp_ref, owin_ref,
                           gate_ref, o_ref, m_ref, l_ref, acc_ref, *, past, n_sel):
    del pt_ref
    b = pl.program_id(0)
    t = pl.program_id(1)
    q8 = q_ref[0]
    row = lax.broadcasted_iota(jnp.int32, (NSA_H, LANES), 0)
    lane = lax.broadcasted_iota(jnp.int32, (NSA_H, LANES), 1)
    slope = slope_ref[...][:, :1]
    kvw = NSA_KVH * HEAD_DIM

    @pl.when(t == 0)
    def _():
        new = new_ref[0].astype(BF16).astype(F32)
        m_ref[...] = jnp.sum(q8.astype(F32) * new[:, :kvw], axis=-1, keepdims=True) * ATT_SCALE + jnp.zeros(
            (NSA_H, LANES), F32)
        l_ref[...] = jnp.ones((NSA_H, LANES), F32)
        acc_ref[...] = jnp.where(row < NSA_G, new[:, kvw:kvw + HEAD_DIM], new[:, kvw + HEAD_DIM:])

    na = ids_ref[b, t]
    nbk = ids_ref[b, n_sel + t]
    blk = jnp.where(row < NSA_G, na, nbk)
    per_page = PAGE_SIZE // CMP_BLOCK
    valid = (lane >> CMP_SHIFT) == (blk & (per_page - 1))
    dist = past - ((blk >> (per_page.bit_length() - 1)) * PAGE_SIZE + lane)
    pa = pa_ref[0, 0]
    pb = pb_ref[0, 0]
    sa = lax.dot_general(q8[:, :HEAD_DIM], pa[:, :HEAD_DIM].astype(BF16), NT_DIMS, preferred_element_type=F32)
    sb = lax.dot_general(q8[:, HEAD_DIM:], pb[:, HEAD_DIM:kvw].astype(BF16), NT_DIMS, preferred_element_type=F32)
    s = jnp.where(row < NSA_G, sa, sb) * ATT_SCALE - slope * dist.astype(F32)
    s = jnp.where(valid, s, NEG_INF)
    m_prev = m_ref[...]
    m_new = jnp.maximum(m_prev, jnp.max(s, axis=-1, keepdims=True))
    alpha = jnp.exp(m_prev - m_new)
    p = jnp.exp(s - m_new)
    l_ref[...] = alpha * l_ref[...] + jnp.sum(p, axis=-1, keepdims=True)
    m_ref[...] = m_new
    pb16 = p.astype(BF16)
    oa = jnp.dot(pb16, pa[:, kvw:kvw + HEAD_DIM].astype(BF16), preferred_element_type=F32)
    ob = jnp.dot(pb16, pb[:, kvw + HEAD_DIM:].astype(BF16), preferred_element_type=F32)
    acc_ref[...] = alpha * acc_ref[...] + jnp.where(row < NSA_G, oa, ob)

    @pl.when(t == n_sel - 1)
    def _():
        gates = jax.nn.sigmoid(gate_ref[0])
        o = (gates[:, 0:1] * ocmp_ref[0] + gates[:, 1:2] * (acc_ref[...] / l_ref[...])
             + gates[:, 2:3] * owin_ref[0])
        o_ref[0] = o.astype(o_ref.dtype)


def nsa_decode_sel(cache, layer, page_table, ids, q8, slope8, new_row, o_cmp, o_win, gates, past):
    db = q8.shape[0]
    n_sel = ids.shape[1] // NSA_KVH
    width = cache.shape[-1]
    per_page = PAGE_SIZE // CMP_BLOCK
    head_spec = pl.BlockSpec((1, NSA_H, HEAD_DIM), lambda b, t, pt, ids: (b, 0, 0))

    def page_spec(g):
        return pl.BlockSpec((1, 1, PAGE_SIZE, width),
                            lambda b, t, pt, ids: (layer, pt[b, ids[b, g * n_sel + t] // per_page], 0, 0))

    return pl.pallas_call(
        functools.partial(_nsa_decode_sel_kernel, past=past, n_sel=n_sel),
        grid_spec=pltpu.PrefetchScalarGridSpec(
            num_scalar_prefetch=2,
            grid=(db, n_sel),
            in_specs=[pl.BlockSpec((1, NSA_H, NSA_KVH * HEAD_DIM), lambda b, t, pt, ids: (b, 0, 0)),
                      pl.BlockSpec((NSA_H, LANES), lambda b, t, pt, ids: (0, 0)),
                      pl.BlockSpec((1, 1, width), lambda b, t, pt, ids: (b, 0, 0)),
                      page_spec(0), page_spec(1), head_spec, head_spec, head_spec],
            out_specs=head_spec,
            scratch_shapes=[pltpu.VMEM((NSA_H, LANES), F32), pltpu.VMEM((NSA_H, LANES), F32),
                            pltpu.VMEM((NSA_H, HEAD_DIM), F32)]),
        out_shape=jax.ShapeDtypeStruct((db, NSA_H, HEAD_DIM), BF16),
        compiler_params=_cparams(("parallel", "arbitrary")),
        name="nsa_decode_sel",
    )(page_table, ids, q8, slope8, new_row, cache, cache, o_cmp, o_win, gates)


def _diff_decode_kernel(pt_ref, q_ref, slope_ref, new_ref, lam_ref, dn_ref, *refs, pp, past, lam_init):
    del pt_ref
    page_refs = refs[:pp]
    o_ref, m_ref, l_ref, acc_ref = refs[pp:]
    j = pl.program_id(1)
    q8 = q_ref[0]
    kw = DIFF_Q_W
    slope = slope_ref[...][:, :1]
    lane = lax.broadcasted_iota(jnp.int32, (2 * DIFF_H, LANES), 1)

    @pl.when(j == 0)
    def _():
        new = new_ref[0].astype(BF16).astype(F32)
        m_ref[...] = jnp.sum(q8.astype(F32) * new[:, :kw], axis=-1, keepdims=True) * ATT_SCALE + jnp.zeros(
            (2 * DIFF_H, LANES), F32)
        l_ref[...] = jnp.ones((2 * DIFF_H, LANES), F32)
        acc_ref[...] = new[:, kw:] + jnp.zeros((2 * DIFF_H, DIFF_V_W), F32)

    for i in range(pp):
        page = page_refs[i][0, 0]
        k0 = (j * pp + i) * PAGE_SIZE
        dist = (past - (k0 + lane)).astype(F32)
        s = lax.dot_general(q8, page[:, :kw].astype(BF16), NT_DIMS, preferred_element_type=F32) * ATT_SCALE
        s = s - slope * dist
        m_prev = m_ref[...]
        m_new = jnp.maximum(m_prev, jnp.max(s, axis=-1, keepdims=True))
        alpha = jnp.exp(m_prev - m_new)
        p = jnp.exp(s - m_new)
        l_ref[...] = alpha * l_ref[...] + jnp.sum(p, axis=-1, keepdims=True)
        m_ref[...] = m_new
        acc_ref[...] = alpha[:, :1] * acc_ref[...] + jnp.dot(p.astype(BF16), page[:, kw:].astype(BF16),
                                                             preferred_element_type=F32)

    @pl.when(j == pl.num_programs(1) - 1)
    def _():
        lam = _diff_lambda(lam_ref[...], lam_init)
        o = acc_ref[...] / l_ref[...][:, :1]
        for h in range(DIFF_H):
            cols = slice(h * DIFF_VD, (h + 1) * DIFF_VD)
            oh = o[h:h + 1, cols] - lam * o[DIFF_H + h:DIFF_H + h + 1, cols]
            y = oh * lax.rsqrt(jnp.mean(oh * oh, axis=-1, keepdims=True) + NORM_EPS)
            o_ref[0, h:h + 1, :] = (y * dn_ref[...] * (1.0 - lam_init)).astype(o_ref.dtype)


def diff_decode(cache, layer, page_table, q8, slope8, new_row, lam_vec, dnorm, lam_init, past, pp=4):
    db, n_pages = page_table.shape
    width = cache.shape[-1]

    def page_spec(i):
        return pl.BlockSpec((1, 1, PAGE_SIZE, width), lambda b, j, pt: (layer, pt[b, j * pp + i], 0, 0))

    return pl.pallas_call(
        functools.partial(_diff_decode_kernel, pp=pp, past=past, lam_init=lam_init),
        grid_spec=pltpu.PrefetchScalarGridSpec(
            num_scalar_prefetch=1,
            grid=(db, n_pages // pp),
            in_specs=[pl.BlockSpec((1, 2 * DIFF_H, DIFF_Q_W), lambda b, j, pt: (b, 0, 0)),
                      pl.BlockSpec((2 * DIFF_H, LANES), lambda b, j, pt: (0, 0)),
                      pl.BlockSpec((1, 1, width), lambda b, j, pt: (b, 0, 0)),
                      pl.BlockSpec((4, HEAD_DIM), lambda b, j, pt: (0, 0)),
                      pl.BlockSpec((1, DIFF_VD), lambda b, j, pt: (0, 0))] + [page_spec(i) for i in range(pp)],
            out_specs=pl.BlockSpec((1, DIFF_H, DIFF_VD), lambda b, j, pt: (b, 0, 0)),
            scratch_shapes=[pltpu.VMEM((2 * DIFF_H, LANES), F32), pltpu.VMEM((2 * DIFF_H, LANES), F32),
                            pltpu.VMEM((2 * DIFF_H, DIFF_V_W), F32)]),
        out_shape=jax.ShapeDtypeStruct((db, DIFF_H, DIFF_VD), BF16),
        compiler_params=_cparams(("parallel", "arbitrary")),
        name="diff_decode",
    )(page_table, q8, slope8, new_row, lam_vec, dnorm.reshape(1, DIFF_VD), *([cache] * pp))


def _ret_decode_kernel(lg_ref, q_ref, k_ref, v_ref, g_ref, rn_ref, st_ref, o_ref, nst_ref):
    ii = lax.broadcasted_iota(jnp.int32, (RET_DK, RET_DK), 0)
    jj = lax.broadcasted_iota(jnp.int32, (RET_DK, RET_DK), 1)
    ones = jnp.ones((1, 1), F32)
    for h in range(RET_H):
        gamma = jnp.exp(ones * lg_ref[h])
        q = q_ref[0, h:h + 1, :]
        k = k_ref[0, h:h + 1, :] * (RET_DK ** -0.5)
        v = v_ref[0, h:h + 1, :]
        state = st_ref[0, h]
        qb = q.astype(BF16).astype(F32)
        kb = k.astype(BF16).astype(F32)
        s = jnp.sum(qb * kb, axis=-1, keepdims=True)
        q_dec = jnp.broadcast_to(q * gamma, (SUBLANES, RET_DK)).astype(BF16)
        o = s * v + jnp.dot(q_dec, state.astype(BF16), preferred_element_type=F32)[0:1]
        k_col = jnp.sum(jnp.where(ii == jj, k, 0.0), axis=-1, keepdims=True)
        nst_ref[0, h] = state * gamma + k_col * v
        y = o * lax.rsqrt(jnp.mean(o * o, axis=-1, keepdims=True) + NORM_EPS) * rn_ref[h]
        gate = g_ref[0, h:h + 1, :]
        o_ref[0, h:h + 1, :] = (gate * jax.nn.sigmoid(gate) * y).astype(o_ref.dtype)


def ret_decode(q, k, v, g, rnorm, state, log_g):
    db = q.shape[0]
    return pl.pallas_call(
        _ret_decode_kernel,
        grid=(db,),
        in_specs=[pl.BlockSpec(memory_space=pltpu.SMEM),
                  pl.BlockSpec((1, RET_H, RET_DK), lambda b: (b, 0, 0)),
                  pl.BlockSpec((1, RET_H, RET_DK), lambda b: (b, 0, 0)),
                  pl.BlockSpec((1, RET_H, RET_DV), lambda b: (b, 0, 0)),
                  pl.BlockSpec((1, RET_H, RET_DV), lambda b: (b, 0, 0)),
                  pl.BlockSpec((RET_H, 1, RET_DV), lambda b: (0, 0, 0)),
                  pl.BlockSpec((1, RET_H, RET_DK, RET_DV), lambda b: (b, 0, 0, 0))],
        out_specs=[pl.BlockSpec((1, RET_H, RET_DV), lambda b: (b, 0, 0)),
                   pl.BlockSpec((1, RET_H, RET_DK, RET_DV), lambda b: (b, 0, 0, 0))],
        out_shape=[jax.ShapeDtypeStruct((db, RET_H, RET_DV), BF16),
                   jax.ShapeDtypeStruct(state.shape, F32)],
        compiler_params=_cparams(("parallel",)),
        name="ret_decode",
    )(log_g, q, k, v, g, rnorm.reshape(RET_H, 1, RET_DV), state)


def _alibi_slopes(n):
    return jnp.asarray([2.0 ** (-8.0 * (i + 1) / n) for i in range(n)], dtype=F32)


def _even_weights(w_in):
    d = w_in.shape[0]
    main = jnp.concatenate([w_in[:, :GATE_OFF], w_in[:, GATE_OFF + NSA_GATE_W:]], axis=1).astype(BF16)
    per_group = NSA_G * 3
    gate = w_in[:, GATE_OFF:GATE_OFF + NSA_GATE_W].reshape(d, NSA_KVH, per_group)
    gate = jnp.pad(gate, ((0, 0), (0, 0), (0, LANES - per_group))).reshape(d, NSA_KVH * LANES).astype(BF16)
    return main, gate


def _even_layer(e, layer, hp, hs, batch, seq, caches, page_table, w_in, w_out, cmp_pos, lam_vec, dnorm):
    cache_cmp, cache_sel, cache_win, cache_diff = caches
    lam_init = 0.8 - 0.6 * math.exp(-0.3 * layer)
    db = hs.shape[0]
    past = page_table.shape[1] * PAGE_SIZE
    w_main, w_gate = _even_weights(w_in)
    nsa_slopes = _alibi_slopes(NSA_H)
    diff_slopes = _alibi_slopes(DIFF_H)
    kv0 = NSA_Q_W
    d0 = NSA_Q_W + 6 * NSA_KV_W

    proj, proj_bf = matmul(hp, w_main, out_dtypes=(F32, BF16))
    gate_pre = matmul(hp, w_gate, tn=NSA_KVH * LANES)
    pos_cg = jnp.transpose(cmp_pos, (0, 2, 1)).reshape(2 * NSA_KVH, CMP_BLOCK)
    kcvc = compress_prompt(proj_bf, jnp.tile(pos_cg, (1, seq // CMP_BLOCK)), batch, seq)
    o_nsa = nsa_prompt(proj_bf, gate_pre, kcvc, nsa_slopes, batch, seq)
    o_diff = diff_prompt(proj_bf, lam_vec, dnorm, diff_slopes, lam_init, batch, seq)
    mix_p = jnp.concatenate([o_nsa, o_diff], axis=1)
    p4 = proj.reshape(batch, seq, MAIN_W)
    kv_shape = (batch, seq, 2, NSA_KVH, HEAD_DIM)
    new_p = (p4[:, :, kv0:kv0 + 2 * NSA_KV_W].reshape(kv_shape),
             p4[:, :, kv0 + 2 * NSA_KV_W:kv0 + 4 * NSA_KV_W].reshape(kv_shape),
             p4[:, seq - min(WINDOW, seq):, kv0 + 4 * NSA_KV_W:kv0 + 6 * NSA_KV_W].reshape(
                 batch, min(WINDOW, seq), 2, NSA_KVH, HEAD_DIM),
             p4[:, :, d0 + DIFF_Q_W:].reshape(batch, seq, 2, DIFF_H, DIFF_VD))

    ps, ps_bf = matmul(hs, w_main, out_dtypes=(F32, BF16))
    gate_s = matmul(hs, w_gate, tn=NSA_KVH * LANES)
    gates8 = gate_s.reshape(db, NSA_KVH, LANES)[:, :, :NSA_G * 3].reshape(db, NSA_H, 3)
    gates8 = jnp.pad(gates8, ((0, 0), (0, 0), (0, LANES - 3)))
    group_of_head = jnp.asarray(np.arange(NSA_H) // NSA_G)
    onehot_g = jax.nn.one_hot(group_of_head, NSA_KVH, dtype=BF16)
    qn = ps_bf[:, :NSA_Q_W].reshape(db, NSA_H, HEAD_DIM)
    q8 = (qn[:, :, None, :] * onehot_g[None, :, :, None]).reshape(db, NSA_H, NSA_KVH * HEAD_DIM)
    slope8 = jnp.broadcast_to(nsa_slopes[:, None], (NSA_H, LANES))
    cmp4 = cache_cmp.reshape(*cache_cmp.shape[:3], -1)
    sel4 = cache_sel.reshape(*cache_sel.shape[:3], -1)
    pos_rows = jnp.tile(jnp.repeat(jnp.transpose(cmp_pos, (1, 0, 2)).reshape(CMP_BLOCK, 2 * NSA_KVH),
                                   HEAD_DIM, axis=1), (PAGE_SIZE // CMP_BLOCK, 1))
    kcvc_s = compress_sample(cmp4, e, page_table, pos_rows)
    w_buf = cache_win.shape[2]
    new_cmp = ps[:, kv0:kv0 + 2 * NSA_KV_W]
    new_sel = ps[:, kv0 + 2 * NSA_KV_W:kv0 + 4 * NSA_KV_W]
    new_win = ps[:, kv0 + 4 * NSA_KV_W:kv0 + 6 * NSA_KV_W]
    win_all = jnp.concatenate([cache_win[e].reshape(db, w_buf, -1), new_win[:, None, :]], axis=1)
    n_win = w_buf + 1
    win_pad = jnp.pad(win_all, ((0, 0), (0, -n_win % LANES), (0, 0)))
    o_cmp, o_win, idx = nsa_decode_cmp(q8, slope8, kcvc_s, win_pad, past, n_win)
    ids = jnp.concatenate([idx[:, 0, :TOP_N - 1], idx[:, NSA_G, :TOP_N - 1]], axis=1)
    o_nsa_s = nsa_decode_sel(sel4, e, page_table, ids, q8, slope8, new_sel[:, None, :], o_cmp, o_win, gates8, past)

    qd = ps_bf[:, d0:d0 + DIFF_Q_W].reshape(db, DIFF_H, 2, HEAD_DIM)
    eye_h = jnp.eye(DIFF_H, dtype=BF16)
    eye_c = jnp.eye(2, dtype=BF16)
    q8d = jnp.einsum('bhcd,hi,cj->bchijd', qd, eye_h, eye_c).reshape(db, 2 * DIFF_H, DIFF_Q_W)
    slope8d = jnp.broadcast_to(jnp.tile(diff_slopes, 2)[:, None], (2 * DIFF_H, LANES))
    diff4 = cache_diff.reshape(*cache_diff.shape[:3], -1)
    new_diff = ps[:, d0 + DIFF_Q_W:]
    o_diff_s = diff_decode(diff4, e, page_table, q8d, slope8d, new_diff[:, None, :], lam_vec, dnorm, lam_init, past)
    mix_s = jnp.concatenate([o_nsa_s.reshape(db, NSA_Q_W), o_diff_s.reshape(db, DIFF_V_W)], axis=1)
    kv_s = (db, 1, 2, NSA_KVH, HEAD_DIM)
    keep = min(WINDOW, n_win)
    new_s = (new_cmp.reshape(kv_s), new_sel.reshape(kv_s),
             win_all[:, n_win - keep:].reshape(db, keep, 2, NSA_KVH, HEAD_DIM),
             new_diff.reshape(db, 1, 2, DIFF_H, DIFF_VD))
    w_out_bf = w_out.astype(BF16)
    return mix_p, mix_s, w_out_bf, new_p, new_s


def _ret_layer(hp, hs, batch, seq, state, w_in, rnorm, w_out):
    db = hs.shape[0]
    log_g = jnp.log1p(-jnp.exp2(-5.0 - jnp.arange(RET_H, dtype=F32)))
    w_in_bf = w_in.astype(BF16)
    proj = matmul(hp, w_in_bf)
    gated_p, st_p = ret_prompt(proj, rnorm, log_g, batch, seq)
    ps = matmul(hs, w_in_bf)
    hk = RET_H * RET_DK
    hv = RET_H * RET_DV
    q = ps[:, :hk].reshape(db, RET_H, RET_DK)
    k = ps[:, hk:2 * hk].reshape(db, RET_H, RET_DK)
    v = ps[:, 2 * hk:2 * hk + hv].reshape(db, RET_H, RET_DV)
    g = ps[:, 2 * hk + hv:].reshape(db, RET_H, RET_DV)
    gated_s, st_s = ret_decode(q, k, v, g, rnorm, state, log_g)
    return gated_p, gated_s.reshape(db, hv), w_out.astype(BF16), st_p, st_s


def kernel(x_prompt, x_sample, cache_nsa_cmp, cache_nsa_sel, cache_nsa_win, cache_diff, state_ret, page_table,
           norm_mix, norm_ffn, norm_final, even_w_in, even_w_out, nsa_cmp_pos, diff_lambda, diff_norm,
           ret_w_in, ret_norm, ret_w_out, ffn_w13, ffn_w2):
    batch, seq, d = x_prompt.shape
    db = x_sample.shape[0]
    xp = x_prompt.reshape(batch * seq, d)
    xs = x_sample.reshape(db, d)
    new_p = [[] for _ in range(5)]
    new_s = [[] for _ in range(5)]
    caches = (cache_nsa_cmp, cache_nsa_sel, cache_nsa_win, cache_diff)
    for layer in range(DEPTH):
        hp = rmsnorm(xp, norm_mix[layer], BF16)
        hs = rmsnorm(xs, norm_mix[layer], BF16)
        if layer % 2 == 0:
            e = layer // 2
            mix_p, mix_s, w_out, np_e, ns_e = _even_layer(
                e, layer, hp, hs, batch, seq, caches, page_table, even_w_in[e], even_w_out[e],
                nsa_cmp_pos[e], diff_lambda[e], diff_norm[e])
            for i in range(4):
                new_p[i].append(np_e[i])
                new_s[i].append(ns_e[i])
        else:
            o = layer // 2
            mix_p, mix_s, w_out, st_p, st_s = _ret_layer(hp, hs, batch, seq, state_ret[o], ret_w_in[o],
                                                         ret_norm[o], ret_w_out[o])
            new_p[4].append(st_p)
            new_s[4].append(st_s)
        xp = matmul(mix_p, w_out, res=xp)
        xs = matmul(mix_s, w_out, res=xs)
        w13 = ffn_w13[layer].astype(BF16)
        w2 = ffn_w2[layer].astype(BF16)
        xp = matmul(swiglu_up(rmsnorm(xp, norm_ffn[layer], BF16), w13), w2, res=xp)
        xs = matmul(swiglu_up(rmsnorm(xs, norm_ffn[layer], BF16), w13), w2, res=xs)
    y_prompt = rmsnorm(xp, norm_final, F32).reshape(batch, seq, d)
    y_sample = rmsnorm(xs, norm_final, F32).reshape(db, 1, d)
    return (y_prompt, y_sample, *[jnp.stack(t) for t in new_p], *[jnp.stack(t) for t in new_s])
```

```python
import functools
import math

import jax
import jax.numpy as jnp
import numpy as np
from jax import lax
from jax.experimental import pallas as pl
from jax.experimental.pallas import tpu as pltpu

F32 = jnp.float32
BF16 = jnp.bfloat16

D_MODEL = 2048
DEPTH = 4
PAGE_SIZE = 128
HEAD_DIM = 128
NSA_H = 8
NSA_KVH = 2
NSA_G = NSA_H // NSA_KVH
CMP_BLOCK = 64
CMP_SHIFT = CMP_BLOCK.bit_length() - 1
TOP_N = 16
WINDOW = 512
FORCE_SCORE = 1.0e4
DIFF_H = 4
DIFF_VD = 2 * HEAD_DIM
RET_H = 8
RET_DK = D_MODEL // RET_H
RET_DV = 2 * D_MODEL // RET_H
RET_CHUNK = 128
D_FF = ((8 * D_MODEL + 3 * 256 - 1) // (3 * 256)) * 256
NORM_EPS = 1e-6
NEG_INF = -1e30
ATT_SCALE = HEAD_DIM ** -0.5
LOG2E = math.log2(math.e)

NSA_Q_W = NSA_H * HEAD_DIM
NSA_KV_W = NSA_KVH * HEAD_DIM
NSA_GATE_W = NSA_H * 3
DIFF_Q_W = DIFF_H * 2 * HEAD_DIM
DIFF_V_W = DIFF_H * DIFF_VD
MAIN_W = NSA_Q_W + 6 * NSA_KV_W + 2 * DIFF_Q_W + DIFF_V_W
GATE_OFF = NSA_Q_W + 6 * NSA_KV_W

NSA_TQ, NSA_TK = 256, 256
DIFF_TQ, DIFF_TK = 512, 512
RET_BLOCK = 256

LANES = 128
SUBLANES = 8
VMEM_LIMIT = 48 * 1024 * 1024


def _cparams(sem):
    return pltpu.CompilerParams(dimension_semantics=sem, vmem_limit_bytes=VMEM_LIMIT)


def _rmsnorm_kernel(x_ref, g_ref, o_ref):
    x = x_ref[...]
    y = x * lax.rsqrt(jnp.mean(x * x, axis=-1, keepdims=True) + NORM_EPS)
    o_ref[...] = (y * g_ref[...]).astype(o_ref.dtype)


def rmsnorm(x, g, out_dtype):
    m, d = x.shape
    tm = min(512, m)
    return pl.pallas_call(
        _rmsnorm_kernel,
        grid=(m // tm,),
        in_specs=[pl.BlockSpec((tm, d), lambda i: (i, 0)), pl.BlockSpec((1, d), lambda i: (0, 0))],
        out_specs=pl.BlockSpec((tm, d), lambda i: (i, 0)),
        out_shape=jax.ShapeDtypeStruct((m, d), out_dtype),
        compiler_params=_cparams(("parallel",)),
        name="rmsnorm",
    )(x, g.reshape(1, d))


def _mm_kernel(x_ref, w_ref, *refs, has_res, n_out):
    acc = jnp.dot(x_ref[...], w_ref[...], preferred_element_type=F32)
    if has_res:
        acc = refs[0][...] + acc
        refs = refs[1:]
    for o_ref in refs[:n_out]:
        o_ref[...] = acc.astype(o_ref.dtype)


def matmul(x, w, res=None, out_dtypes=(F32,), tm=1024, tn=512):
    m, k = x.shape
    n = w.shape[1]
    tm = min(tm, m)
    tn = min(tn, n)
    if k > 4096:
        tm = min(tm, 512)
    in_specs = [pl.BlockSpec((tm, k), lambda i, j: (i, 0)), pl.BlockSpec((k, tn), lambda i, j: (0, j))]
    args = [x, w]
    if res is not None:
        in_specs.append(pl.BlockSpec((tm, tn), lambda i, j: (i, j)))
        args.append(res)
    outs = pl.pallas_call(
        functools.partial(_mm_kernel, has_res=res is not None, n_out=len(out_dtypes)),
        grid=(m // tm, n // tn),
        in_specs=in_specs,
        out_specs=[pl.BlockSpec((tm, tn), lambda i, j: (i, j)) for _ in out_dtypes],
        out_shape=[jax.ShapeDtypeStruct((m, n), dt) for dt in out_dtypes],
        compiler_params=_cparams(("parallel", "parallel")),
        name="matmul",
    )(*args)
    return outs[0] if len(out_dtypes) == 1 else outs


def _swiglu_up_kernel(x_ref, w1_ref, w3_ref, o_ref):
    x = x_ref[...]
    a = jnp.dot(x, w1_ref[...], preferred_element_type=F32)
    b = jnp.dot(x, w3_ref[...], preferred_element_type=F32)
    o_ref[...] = (a * jax.nn.sigmoid(a) * b).astype(o_ref.dtype)


def swiglu_up(x, w13, tm=1024, tn=512):
    m, k = x.shape
    half = w13.shape[1] // 2
    tm = min(tm, m)
    nb = half // tn
    return pl.pallas_call(
        _swiglu_up_kernel,
        grid=(m // tm, nb),
        in_specs=[pl.BlockSpec((tm, k), lambda i, j: (i, 0)),
                  pl.BlockSpec((k, tn), lambda i, j: (0, j)),
                  pl.BlockSpec((k, tn), lambda i, j: (0, j + nb))],
        out_specs=pl.BlockSpec((tm, tn), lambda i, j: (i, j)),
        out_shape=jax.ShapeDtypeStruct((m, half), BF16),
        compiler_params=_cparams(("parallel", "parallel")),
        name="swiglu_up",
    )(x, w13, w13)


def _softmax_rows64(logits):
    e = jnp.exp(logits - jnp.max(logits, axis=-1, keepdims=True))
    reps = logits.shape[-1] // CMP_BLOCK
    return e * (reps / jnp.sum(e, axis=-1, keepdims=True))


def _compress_prompt_kernel(x_ref, pos_ref, o_ref, *, seq):
    w = _softmax_rows64(pos_ref[...])
    blk = lax.broadcasted_iota(jnp.int32, (LANES, seq), 0)
    key_blk = lax.broadcasted_iota(jnp.int32, (LANES, seq), 1) >> CMP_SHIFT
    onblk = blk == key_blk
    for c in range(4):
        wb = jnp.where(onblk, w[c:c + 1, :], 0.0).astype(BF16)
        o_ref[0, :, c * LANES:(c + 1) * LANES] = jnp.dot(
            wb, x_ref[:, c * LANES:(c + 1) * LANES], preferred_element_type=F32).astype(BF16)


def compress_prompt(proj_bf, pos_tiled, batch, seq):
    return pl.pallas_call(
        functools.partial(_compress_prompt_kernel, seq=seq),
        grid=(batch,),
        in_specs=[pl.BlockSpec((seq, 4 * LANES), lambda b: (b, NSA_Q_W // (4 * LANES))),
                  pl.BlockSpec((4, seq), lambda b: (0, 0))],
        out_specs=pl.BlockSpec((1, LANES, 4 * LANES), lambda b: (b, 0, 0)),
        out_shape=jax.ShapeDtypeStruct((batch, LANES, 4 * LANES), BF16),
        compiler_params=_cparams(("parallel",)),
        name="nsa_compress_prompt",
    )(proj_bf, pos_tiled)


def _lane_tile(x, width):
    return x if width == LANES else jnp.concatenate([x] * (width // LANES), axis=-1)


def _flash_step(s, v, m_ref, l_ref, acc_ref, idx):
    ss, vs = (s, v) if isinstance(s, (list, tuple)) else ([s], [v])
    tiles = [[t[:, j * LANES:(j + 1) * LANES] for j in range(t.shape[1] // LANES)] for t in ss]
    blocks = [blk for tile in tiles for blk in tile]
    mx = blocks[0]
    for blk in blocks[1:]:
        mx = jnp.maximum(mx, blk)
    m_prev = m_ref[idx]
    m_new = jnp.maximum(m_prev, jnp.max(mx, axis=-1, keepdims=True))
    alpha = jnp.exp2(m_prev - m_new)
    ps = [[jnp.exp2(blk - m_new) for blk in tile] for tile in tiles]
    row_sum = None
    for tile in ps:
        for p in tile:
            row_sum = p if row_sum is None else row_sum + p
    l_ref[idx] = alpha * l_ref[idx] + jnp.sum(row_sum, axis=-1, keepdims=True)
    m_ref[idx] = m_new
    acc = _lane_tile(alpha, vs[0].shape[-1]) * acc_ref[idx]
    for tile, vt in zip(ps, vs):
        acc = acc + jnp.dot(jnp.concatenate([p.astype(BF16) for p in tile], axis=-1), vt,
                            preferred_element_type=F32)
    acc_ref[idx] = acc


def _flash_init(m_ref, l_ref, acc_ref):
    m_ref[...] = jnp.full(m_ref.shape, NEG_INF, F32)
    l_ref[...] = jnp.zeros(l_ref.shape, F32)
    acc_ref[...] = jnp.zeros(acc_ref.shape, F32)


def _flash_result(l_ref, acc_ref, idx):
    return acc_ref[idx] / _lane_tile(l_ref[idx], acc_ref.shape[-1])


def _softmax_pv(s, v):
    p = jnp.exp2(s - jnp.max(s, axis=-1, keepdims=True))
    l = jnp.sum(p, axis=-1, keepdims=True)
    return jnp.dot(p.astype(BF16), v, preferred_element_type=F32) / l


def _nsa_prompt_kernel(slopes_ref, q_ref, gate_ref, kc_ref, vc_ref, ks_ref, vs_ref, kw_ref, vw_ref,
                       o_ref, m_ref, l_ref, acc_ref, *, tq, tk, n_blocks):
    g = pl.program_id(1)
    qi = pl.program_id(2)
    q0 = qi * tq
    row = lax.broadcasted_iota(jnp.int32, (tq, LANES), 0)
    col = lax.broadcasted_iota(jnp.int32, (tq, LANES), 1)
    qpos = q0 + row
    nt = (((1,), (1,)), ((), ()))

    kc = kc_ref[0]
    vc = vc_ref[0]
    cdist = qpos - (col * CMP_BLOCK + (CMP_BLOCK - 1))
    cmask = cdist >= 0
    cdist_f = cdist.astype(F32)
    imp = jnp.zeros((tq, LANES), F32)
    o_cmp = []
    for h in range(NSA_G):
        slope = slopes_ref[g * NSA_G + h]
        qh = q_ref[:, h * HEAD_DIM:(h + 1) * HEAD_DIM]
        s = lax.dot_general(qh, kc, nt, preferred_element_type=F32) * ATT_SCALE - slope * cdist_f
        s = jnp.where(cmask, s, NEG_INF)
        p = jnp.exp(s - jnp.max(s, axis=-1, keepdims=True))
        p = jnp.where(cmask, p / jnp.sum(p, axis=-1, keepdims=True), 0.0)
        imp = imp + p
        o_cmp.append(jnp.dot(p.astype(BF16), vc, preferred_element_type=F32))

    imp = jnp.where(cmask, imp, -1.0)
    forced = (col == (qpos >> CMP_SHIFT)) | (col == 0)
    imp = jnp.where(forced, FORCE_SCORE, imp)
    nblk = -(-n_blocks // SUBLANES) * SUBLANES
    imp_t = imp.T[:nblk]
    sub = lax.broadcasted_iota(jnp.int32, (SUBLANES, tq), 0)
    sel_rows = []
    for vi in range(nblk // SUBLANES):
        x = imp_t[vi * SUBLANES:(vi + 1) * SUBLANES]
        rank = jnp.zeros((SUBLANES, tq), F32)
        for j in range(nblk):
            r = imp_t[j:j + 1, :]
            jv, jr = divmod(j, SUBLANES)
            ge = jnp.where(r >= x, 1.0, 0.0)
            gt = jnp.where(r > x, 1.0, 0.0)
            if jv < vi:
                rank = rank + ge
            elif jv > vi:
                rank = rank + gt
            else:
                rank = rank + jnp.where(sub > jr, ge, gt)
        sel_rows.append(jnp.where(rank < TOP_N, 1.0, 0.0))
    sel_rows.append(jnp.zeros((LANES - nblk, tq), F32))
    sel = jnp.concatenate(sel_rows, axis=0).T.astype(BF16)

    c1 = ATT_SCALE * LOG2E
    slope2 = [slopes_ref[g * NSA_G + h] * LOG2E for h in range(NSA_G)]

    _flash_init(m_ref, l_ref, acc_ref)
    e_row = lax.broadcasted_iota(jnp.int32, (LANES, tk), 0)
    e_col = lax.broadcasted_iota(jnp.int32, (LANES, tk), 1)
    rel = lax.broadcasted_iota(jnp.int32, (1, tk), 1)
    krow = lax.broadcasted_iota(jnp.int32, (tq, tk), 0)
    kcol = lax.broadcasted_iota(jnp.int32, (tq, tk), 1)

    def sel_tile(kt, diagonal):
        k0 = pl.multiple_of(kt * tk, tk)
        k = ks_ref[pl.ds(k0, tk), :]
        v = vs_ref[pl.ds(k0, tk), :]
        expand = jnp.where(((k0 + e_col) >> CMP_SHIFT) == e_row, 1.0, 0.0).astype(BF16)
        picked = jnp.dot(sel, expand, preferred_element_type=F32)
        if diagonal:
            picked = jnp.where(k0 + kcol <= q0 + krow, picked, 0.0)
        valid = picked > 0.5
        bias = (k0 - q0 + rel).astype(F32)
        for h in range(NSA_G):
            qh = q_ref[:, h * HEAD_DIM:(h + 1) * HEAD_DIM]
            s = lax.dot_general(qh, k, nt, preferred_element_type=F32) * c1 + slope2[h] * bias
            _flash_step(jnp.where(valid, s, NEG_INF), v, m_ref, l_ref, acc_ref, h)

    kd = q0 // tk
    sel_tile(kd, True)

    def sel_body(i, carry):
        sel_tile(kd - 1 - i, False)
        return carry

    lax.fori_loop(0, kd, sel_body, 0)
    o_sel = [_flash_result(l_ref, acc_ref, h) for h in range(NSA_G)]

    n_band = WINDOW + tq
    w0 = pl.multiple_of(jnp.maximum(q0 - WINDOW, 0), LANES)
    kw = kw_ref[pl.ds(w0, n_band), :]
    vw = vw_ref[pl.ds(w0, n_band), :]
    wdist = (q0 + lax.broadcasted_iota(jnp.int32, (tq, n_band), 0)) - (
        w0 + lax.broadcasted_iota(jnp.int32, (tq, n_band), 1))
    wvalid = (wdist >= 0) & (wdist <= WINDOW)
    wbias = (w0 - q0 + lax.broadcasted_iota(jnp.int32, (1, n_band), 1)).astype(F32)
    o_win = []
    for h in range(NSA_G):
        qh = q_ref[:, h * HEAD_DIM:(h + 1) * HEAD_DIM]
        s = lax.dot_general(qh, kw, nt, preferred_element_type=F32) * c1 + slope2[h] * wbias
        o_win.append(_softmax_pv(jnp.where(wvalid, s, NEG_INF), vw))

    gates = jax.nn.sigmoid(gate_ref[...])
    for h in range(NSA_G):
        o = (gates[:, 3 * h:3 * h + 1] * o_cmp[h] + gates[:, 3 * h + 1:3 * h + 2] * o_sel[h]
             + gates[:, 3 * h + 2:3 * h + 3] * o_win[h])
        o_ref[:, h * HEAD_DIM:(h + 1) * HEAD_DIM] = o.astype(o_ref.dtype)


def nsa_prompt(proj_bf, gate_pre, kcvc, slopes, batch, seq, tq=NSA_TQ, tk=NSA_TK):
    assert tk % tq == 0 and seq % tk == 0 and seq >= WINDOW + tq and seq // CMP_BLOCK <= LANES
    nq = seq // tq
    gw = NSA_G * HEAD_DIM
    kv0 = NSA_Q_W // HEAD_DIM

    def kv_spec(slot):
        return pl.BlockSpec((seq, HEAD_DIM), lambda b, g, i: (b, kv0 + 2 * slot + g))

    return pl.pallas_call(
        functools.partial(_nsa_prompt_kernel, tq=tq, tk=tk, n_blocks=seq // CMP_BLOCK),
        grid=(batch, NSA_KVH, nq),
        in_specs=[pl.BlockSpec(memory_space=pltpu.SMEM),
                  pl.BlockSpec((tq, gw), lambda b, g, i: (b * nq + i, g)),
                  pl.BlockSpec((tq, LANES), lambda b, g, i: (b * nq + i, g)),
                  pl.BlockSpec((1, LANES, HEAD_DIM), lambda b, g, i: (b, 0, g)),
                  pl.BlockSpec((1, LANES, HEAD_DIM), lambda b, g, i: (b, 0, 2 + g)),
                  kv_spec(2), kv_spec(3), kv_spec(4), kv_spec(5)],
        out_specs=pl.BlockSpec((tq, gw), lambda b, g, i: (b * nq + i, g)),
        out_shape=jax.ShapeDtypeStruct((batch * seq, NSA_Q_W), BF16),
        scratch_shapes=[pltpu.VMEM((NSA_G, tq, LANES), F32), pltpu.VMEM((NSA_G, tq, LANES), F32),
                        pltpu.VMEM((NSA_G, tq, HEAD_DIM), F32)],
        compiler_params=_cparams(("parallel", "parallel", "arbitrary")),
        name="nsa_prompt",
    )(slopes, proj_bf, gate_pre, kcvc, kcvc, proj_bf, proj_bf, proj_bf, proj_bf)


def _diff_lambda(lv, lam_init):
    a = jnp.sum(lv[0:1] * lv[1:2], axis=-1, keepdims=True)
    b = jnp.sum(lv[2:3] * lv[3:4], axis=-1, keepdims=True)
    return jnp.exp(a) - jnp.exp(b) + lam_init


def _diff_prompt_kernel(slopes_ref, q_ref, k_ref, v_ref, lam_ref, dn_ref, o_ref, m_ref, l_ref, acc_ref,
                        *, tq, tk, lam_init):
    h = pl.program_id(1)
    qi = pl.program_id(2)
    q0 = qi * tq
    krow = lax.broadcasted_iota(jnp.int32, (tq, tk), 0)
    kcol = lax.broadcasted_iota(jnp.int32, (tq, tk), 1)
    rel = lax.broadcasted_iota(jnp.int32, (1, tk), 1)
    c1 = ATT_SCALE * LOG2E
    slope2 = slopes_ref[h] * LOG2E
    nt = (((1,), (1,)), ((), ()))
    _flash_init(m_ref, l_ref, acc_ref)

    def tile(kt, diagonal):
        k0 = pl.multiple_of(kt * tk, tk)
        k = k_ref[pl.ds(k0, tk), :]
        v = v_ref[pl.ds(k0, tk), :]
        bias = slope2 * (k0 - q0 + rel).astype(F32)
        for c in range(2):
            s = lax.dot_general(q_ref[:, c * HEAD_DIM:(c + 1) * HEAD_DIM], k[:, c * HEAD_DIM:(c + 1) * HEAD_DIM],
                                nt, preferred_element_type=F32) * c1 + bias
            if diagonal:
                s = jnp.where(k0 + kcol <= q0 + krow, s, NEG_INF)
            _flash_step(s, v, m_ref, l_ref, acc_ref, c)

    kd = q0 // tk
    tile(kd, True)

    def body(i, carry):
        tile(kd - 1 - i, False)
        return carry

    lax.fori_loop(0, kd, body, 0)
    lam = _diff_lambda(lam_ref[...], lam_init)
    o = _flash_result(l_ref, acc_ref, 0) - lam * _flash_result(l_ref, acc_ref, 1)
    y = o * lax.rsqrt(jnp.mean(o * o, axis=-1, keepdims=True) + NORM_EPS)
    o_ref[...] = (y * dn_ref[...] * (1.0 - lam_init)).astype(o_ref.dtype)


def diff_prompt(proj_bf, lam_vec, dnorm, slopes, lam_init, batch, seq, tq=DIFF_TQ, tk=DIFF_TK):
    assert tk % tq == 0 and seq % tk == 0
    nq = seq // tq
    w = 2 * HEAD_DIM
    qb = (NSA_Q_W + 6 * NSA_KV_W) // w
    kb = qb + DIFF_H
    vb = kb + DIFF_H
    return pl.pallas_call(
        functools.partial(_diff_prompt_kernel, tq=tq, tk=tk, lam_init=lam_init),
        grid=(batch, DIFF_H, nq),
        in_specs=[pl.BlockSpec(memory_space=pltpu.SMEM),
                  pl.BlockSpec((tq, w), lambda b, h, i: (b * nq + i, qb + h)),
                  pl.BlockSpec((seq, w), lambda b, h, i: (b, kb + h)),
                  pl.BlockSpec((seq, w), lambda b, h, i: (b, vb + h)),
                  pl.BlockSpec((4, HEAD_DIM), lambda b, h, i: (0, 0)),
                  pl.BlockSpec((1, DIFF_VD), lambda b, h, i: (0, 0))],
        out_specs=pl.BlockSpec((tq, w), lambda b, h, i: (b * nq + i, h)),
        out_shape=jax.ShapeDtypeStruct((batch * seq, DIFF_V_W), BF16),
        scratch_shapes=[pltpu.VMEM((2, tq, LANES), F32), pltpu.VMEM((2, tq, LANES), F32),
                        pltpu.VMEM((2, tq, DIFF_VD), F32)],
        compiler_params=_cparams(("parallel", "parallel", "arbitrary")),
        name="diff_prompt",
    )(slopes, proj_bf, proj_bf, proj_bf, lam_vec, dnorm.reshape(1, DIFF_VD))


def _ret_prompt_kernel(lg_ref, q_ref, k_ref, v_ref, g_ref, rn_ref, o_ref, st_ref, *, chunk):
    h = pl.program_id(1)
    c = pl.program_id(2)
    lg = lg_ref[h]

    @pl.when(c == 0)
    def _():
        st_ref[...] = jnp.zeros(st_ref.shape, F32)

    ii = lax.broadcasted_iota(jnp.int32, (chunk, chunk), 0)
    jj = lax.broadcasted_iota(jnp.int32, (chunk, chunk), 1)
    d = (ii - jj).astype(F32)
    decay = jnp.where(d >= 0, jnp.exp(lg * jnp.maximum(d, 0.0)), 0.0)
    ik = lax.broadcasted_iota(jnp.int32, (chunk, RET_DK), 0).astype(F32)
    q_dec = jnp.exp(lg * (ik + 1.0))
    k_dec = jnp.exp(lg * (chunk - 1.0 - ik))
    ones = jnp.ones((1, 1), F32)
    g_chunk = jnp.exp(ones * (lg * chunk))

    q = q_ref[...]
    k = k_ref[...] * (RET_DK ** -0.5)
    v = v_ref[...].astype(BF16)
    state = st_ref[0, 0]
    s = lax.dot_general(q.astype(BF16), k.astype(BF16), (((1,), (1,)), ((), ())),
                        preferred_element_type=F32) * decay
    o = jnp.dot(s.astype(BF16), v, preferred_element_type=F32)
    o = o + jnp.dot((q * q_dec).astype(BF16), state.astype(BF16), preferred_element_type=F32)
    kv = lax.dot_general((k * k_dec).astype(BF16), v, (((0,), (0,)), ((), ())), preferred_element_type=F32)
    st_ref[0, 0] = state * g_chunk + kv

    y = o * lax.rsqrt(jnp.mean(o * o, axis=-1, keepdims=True) + NORM_EPS) * rn_ref[0]
    gate = g_ref[...]
    o_ref[...] = (gate * jax.nn.sigmoid(gate) * y).astype(o_ref.dtype)


def ret_prompt(proj, rnorm, log_g, batch, seq, chunk=RET_BLOCK):
    nc = seq // chunk
    kb = RET_H
    vb = 2 * RET_H * RET_DK // RET_DV
    gb = vb + RET_H
    return pl.pallas_call(
        functools.partial(_ret_prompt_kernel, chunk=chunk),
        grid=(batch, RET_H, nc),
        in_specs=[pl.BlockSpec(memory_space=pltpu.SMEM),
                  pl.BlockSpec((chunk, RET_DK), lambda b, h, c: (b * nc + c, h)),
                  pl.BlockSpec((chunk, RET_DK), lambda b, h, c: (b * nc + c, kb + h)),
                  pl.BlockSpec((chunk, RET_DV), lambda b, h, c: (b * nc + c, vb + h)),
                  pl.BlockSpec((chunk, RET_DV), lambda b, h, c: (b * nc + c, gb + h)),
                  pl.BlockSpec((1, 1, RET_DV), lambda b, h, c: (h, 0, 0))],
        out_specs=[pl.BlockSpec((chunk, RET_DV), lambda b, h, c: (b * nc + c, h)),
                   pl.BlockSpec((1, 1, RET_DK, RET_DV), lambda b, h, c: (b, h, 0, 0))],
        out_shape=[jax.ShapeDtypeStruct((batch * seq, RET_H * RET_DV), BF16),
                   jax.ShapeDtypeStruct((batch, RET_H, RET_DK, RET_DV), F32)],
        compiler_params=_cparams(("parallel", "parallel", "arbitrary")),
        name="ret_prompt",
    )(log_g, proj, proj, proj, proj, rnorm.reshape(RET_H, 1, RET_DV))


NT_DIMS = (((1,), (1,)), ((), ()))


def _compress_sample_kernel(pt_ref, pos_ref, *refs, pp):
    del pt_ref
    o_ref = refs[pp]
    per_page = PAGE_SIZE // CMP_BLOCK
    n_rows = PAGE_SIZE * NSA_KVH
    row = lax.broadcasted_iota(jnp.int32, (SUBLANES, n_rows), 0)
    col = lax.broadcasted_iota(jnp.int32, (SUBLANES, n_rows), 1)
    g_shift = NSA_KVH.bit_length() - 1
    member = ((col & (NSA_KVH - 1)) == (row & (NSA_KVH - 1))) & ((col >> (g_shift + CMP_SHIFT)) == (row >> g_shift))
    weights = []
    for kv in range(2):
        logits = jnp.where(member, pos_ref[kv], NEG_INF)
        e = jnp.exp(logits - jnp.max(logits, axis=-1, keepdims=True))
        weights.append((e / jnp.sum(e, axis=-1, keepdims=True)).astype(BF16))
    for i in range(pp):
        for kv in range(2):
            x = refs[i][0, 0, :, kv].reshape(n_rows, HEAD_DIM).astype(BF16)
            res = jnp.dot(weights[kv], x, preferred_element_type=F32)
            for half in range(per_page):
                for g in range(NSA_KVH):
                    c0 = (kv * NSA_KVH + g) * HEAD_DIM
                    o_ref[0, i * per_page + half:i * per_page + half + 1, c0:c0 + HEAD_DIM] = (
                        res[half * NSA_KVH + g:half * NSA_KVH + g + 1])


def compress_sample(cache, layer, page_table, pos_rows, pp=8):
    db, n_pages = page_table.shape
    per_page = PAGE_SIZE // CMP_BLOCK
    width = 2 * NSA_KVH * HEAD_DIM
    assert per_page * NSA_KVH <= SUBLANES

    def page_spec(i):
        return pl.BlockSpec((1, 1) + cache.shape[2:], lambda b, j, pt: (layer, pt[b, j * pp + i], 0, 0, 0, 0))

    return pl.pallas_call(
        functools.partial(_compress_sample_kernel, pp=pp),
        grid_spec=pltpu.PrefetchScalarGridSpec(
            num_scalar_prefetch=1,
            grid=(db, n_pages // pp),
            in_specs=[pl.BlockSpec(pos_rows.shape, lambda b, j, pt: (0, 0, 0))] + [page_spec(i) for i in range(pp)],
            out_specs=pl.BlockSpec((1, pp * per_page, width), lambda b, j, pt: (b, j, 0))),
        out_shape=jax.ShapeDtypeStruct((db, n_pages * per_page, width), F32),
        compiler_params=_cparams(("parallel", "arbitrary")),
        name="nsa_compress_sample",
    )(page_table, pos_rows, *([cache] * pp))


def _pick_group(x):
    row = lax.broadcasted_iota(jnp.int32, (NSA_H, HEAD_DIM), 0)
    return jnp.where(row < NSA_G, x[:, :HEAD_DIM], x[:, HEAD_DIM:])


def _nsa_decode_cmp_kernel(q_ref, slope_ref, kcvc_ref, win_ref, ocmp_ref, owin_ref, idx_ref, *, past, n_win):
    q8 = q_ref[0]
    kcvc = kcvc_ref[0]
    nb = kcvc.shape[0]
    kvw = NSA_KVH * HEAD_DIM
    lane = lax.broadcasted_iota(jnp.int32, (NSA_H, nb), 1)
    row = lax.broadcasted_iota(jnp.int32, (NSA_H, nb), 0)
    slope = slope_ref[...][:, :1]
    cdist = (past - (lane * CMP_BLOCK + CMP_BLOCK - 1)).astype(F32)
    s = lax.dot_general(q8, kcvc[:, :kvw].astype(BF16), NT_DIMS, preferred_element_type=F32) * ATT_SCALE
    s = s - slope * cdist
    p = jnp.exp(s - jnp.max(s, axis=-1, keepdims=True))
    p = p / jnp.sum(p, axis=-1, keepdims=True)
    ocmp_ref[0] = _pick_group(jnp.dot(p.astype(BF16), kcvc[:, kvw:].astype(BF16), preferred_element_type=F32))

    g0 = p[0:1] + p[1:2] + p[2:3] + p[3:4]
    g1 = p[4:5] + p[5:6] + p[6:7] + p[7:8]
    x = jnp.where(row < NSA_G, g0, g1)
    x = jnp.where(lane == 0, FORCE_SCORE, x)
    out_lane = lax.broadcasted_iota(jnp.int32, (NSA_H, LANES), 1)
    lane_f = lane.astype(F32)
    picked = jnp.zeros((NSA_H, LANES), F32)
    for t in range(TOP_N - 1):
        mx = jnp.max(x, axis=-1, keepdims=True)
        idx = jnp.min(jnp.where(x == mx, lane_f, float(nb)), axis=-1, keepdims=True)
        picked = jnp.where(out_lane == t, idx, picked)
        x = jnp.where(lane_f == idx, -2.0, x)
    idx_ref[0] = picked.astype(jnp.int32)

    win = win_ref[0]
    nw = win.shape[0]
    wl = lax.broadcasted_iota(jnp.int32, (NSA_H, nw), 1)
    wdist = (n_win - 1 - wl)
    valid = wdist >= 0
    s = lax.dot_general(q8, win[:, :kvw].astype(BF16), NT_DIMS, preferred_element_type=F32) * ATT_SCALE
    s = jnp.where(valid, s - slope * wdist.astype(F32), NEG_INF)
    p = jnp.exp(s - jnp.max(s, axis=-1, keepdims=True))
    p = p / jnp.sum(p, axis=-1, keepdims=True)
    owin_ref[0] = _pick_group(jnp.dot(p.astype(BF16), win[:, kvw:].astype(BF16), preferred_element_type=F32))


def nsa_decode_cmp(q8, slope8, kcvc, win_all, past, n_win):
    db = q8.shape[0]
    nb = kcvc.shape[1]
    nw = win_all.shape[1]
    w = kcvc.shape[2]
    head_out = jax.ShapeDtypeStruct((db, NSA_H, HEAD_DIM), F32)
    head_spec = pl.BlockSpec((1, NSA_H, HEAD_DIM), lambda b: (b, 0, 0))
    return pl.pallas_call(
        functools.partial(_nsa_decode_cmp_kernel, past=past, n_win=n_win),
        grid=(db,),
        in_specs=[pl.BlockSpec((1, NSA_H, NSA_KVH * HEAD_DIM), lambda b: (b, 0, 0)),
                  pl.BlockSpec((NSA_H, LANES), lambda b: (0, 0)),
                  pl.BlockSpec((1, nb, w), lambda b: (b, 0, 0)),
                  pl.BlockSpec((1, nw, w), lambda b: (b, 0, 0))],
        out_specs=[head_spec, head_spec, pl.BlockSpec((1, NSA_H, LANES), lambda b: (b, 0, 0))],
        out_shape=[head_out, head_out, jax.ShapeDtypeStruct((db, NSA_H, LANES), jnp.int32)],
        compiler_params=_cparams(("parallel",)),
        name="nsa_decode_cmp",
    )(q8, slope8, kcvc, win_all)


def _nsa_decode_sel_kernel(pt_ref, ids_ref, q_ref, slope_ref, new_ref, pa_ref, pb_ref, ocmp_ref, owin_ref,
                           gate_ref, o_ref, m_ref, l_ref, acc_ref, *, past, n_sel):
    del pt_ref
    b = pl.program_id(0)
    t = pl.program_id(1)
    q8 = q_ref[0]
    row = lax.broadcasted_iota(jnp.int32, (NSA_H, LANES), 0)
    lane = lax.broadcasted_iota(jnp.int32, (NSA_H, LANES), 1)
    slope = slope_ref[...][:, :1]
    kvw = NSA_KVH * HEAD_DIM

    @pl.when(t == 0)
    def _():
        new = new_ref[0].astype(BF16).astype(F32)
        m_ref[...] = jnp.sum(q8.astype(F32) * new[:, :kvw], axis=-1, keepdims=True) * ATT_SCALE + jnp.zeros(
            (NSA_H, LANES), F32)
        l_ref[...] = jnp.ones((NSA_H, LANES), F32)
        acc_ref[...] = jnp.where(row < NSA_G, new[:, kvw:kvw + HEAD_DIM], new[:, kvw + HEAD_DIM:])

    na = ids_ref[b, t]
    nbk = ids_ref[b, n_sel + t]
    blk = jnp.where(row < NSA_G, na, nbk)
    per_page = PAGE_SIZE // CMP_BLOCK
    valid = (lane >> CMP_SHIFT) == (blk & (per_page - 1))
    dist = past - ((blk >> (per_page.bit_length() - 1)) * PAGE_SIZE + lane)
    sa = lax.dot_general(q8[:, :HEAD_DIM], pa_ref[0, 0, :, 0, 0, :].astype(BF16), NT_DIMS,
                         preferred_element_type=F32)
    sb = lax.dot_general(q8[:, HEAD_DIM:], pb_ref[0, 0, :, 0, 1, :].astype(BF16), NT_DIMS,
                         preferred_element_type=F32)
    s = jnp.where(row < NSA_G, sa, sb) * ATT_SCALE - slope * dist.astype(F32)
    s = jnp.where(valid, s, NEG_INF)
    m_prev = m_ref[...]
    m_new = jnp.maximum(m_prev, jnp.max(s, axis=-1, keepdims=True))
    alpha = jnp.exp(m_prev - m_new)
    p = jnp.exp(s - m_new)
    l_ref[...] = alpha * l_ref[...] + jnp.sum(p, axis=-1, keepdims=True)
    m_ref[...] = m_new
    pb16 = p.astype(BF16)
    oa = jnp.dot(pb16, pa_ref[0, 0, :, 1, 0, :].astype(BF16), preferred_element_type=F32)
    ob = jnp.dot(pb16, pb_ref[0, 0, :, 1, 1, :].astype(BF16), preferred_element_type=F32)
    acc_ref[...] = alpha * acc_ref[...] + jnp.where(row < NSA_G, oa, ob)

    @pl.when(t == n_sel - 1)
    def _():
        gates = jax.nn.sigmoid(gate_ref[0])
        o = (gates[:, 0:1] * ocmp_ref[0] + gates[:, 1:2] * (acc_ref[...] / l_ref[...])
             + gates[:, 2:3] * owin_ref[0])
        o_ref[0] = o.astype(o_ref.dtype)


def nsa_decode_sel(cache, layer, page_table, ids, q8, slope8, new_row, o_cmp, o_win, gates, past):
    db = q8.shape[0]
    n_sel = ids.shape[1] // NSA_KVH
    width = 2 * NSA_KVH * HEAD_DIM
    per_page = PAGE_SIZE // CMP_BLOCK
    head_spec = pl.BlockSpec((1, NSA_H, HEAD_DIM), lambda b, t, pt, ids: (b, 0, 0))

    def page_spec(g):
        return pl.BlockSpec((1, 1) + cache.shape[2:],
                            lambda b, t, pt, ids: (layer, pt[b, ids[b, g * n_sel + t] // per_page], 0, 0, 0, 0))

    return pl.pallas_call(
        functools.partial(_nsa_decode_sel_kernel, past=past, n_sel=n_sel),
        grid_spec=pltpu.PrefetchScalarGridSpec(
            num_scalar_prefetch=2,
            grid=(db, n_sel),
            in_specs=[pl.BlockSpec((1, NSA_H, NSA_KVH * HEAD_DIM), lambda b, t, pt, ids: (b, 0, 0)),
                      pl.BlockSpec((NSA_H, LANES), lambda b, t, pt, ids: (0, 0)),
                      pl.BlockSpec((1, 1, width), lambda b, t, pt, ids: (b, 0, 0)),
                      page_spec(0), page_spec(1), head_spec, head_spec, head_spec],
            out_specs=head_spec,
            scratch_shapes=[pltpu.VMEM((NSA_H, LANES), F32), pltpu.VMEM((NSA_H, LANES), F32),
                            pltpu.VMEM((NSA_H, HEAD_DIM), F32)]),
        out_shape=jax.ShapeDtypeStruct((db, NSA_H, HEAD_DIM), BF16),
        compiler_params=_cparams(("parallel", "arbitrary")),
        name="nsa_decode_sel",
    )(page_table, ids, q8, slope8, new_row, cache, cache, o_cmp, o_win, gates)


def _diff_decode_kernel(pt_ref, q_ref, slope_ref, new_ref, lam_ref, dn_ref, *refs, pp, past, lam_init):
    del pt_ref
    page_refs = refs[:pp]
    o_ref, m_ref, l_ref, acc_ref = refs[pp:]
    j = pl.program_id(1)
    rows = 2 * DIFF_H
    n_keys = PAGE_SIZE * DIFF_H
    h_shift = DIFF_H.bit_length() - 1
    q8 = q_ref[0]
    slope2 = slope_ref[...][:, :1] * LOG2E
    row = lax.broadcasted_iota(jnp.int32, (rows, n_keys), 0)
    col = lax.broadcasted_iota(jnp.int32, (rows, n_keys), 1)
    own_head = (col & (DIFF_H - 1)) == (row & (DIFF_H - 1))
    key_in_page = (col >> h_shift).astype(F32)

    @pl.when(j == 0)
    def _():
        k_new = jnp.concatenate([new_ref[0, 0]] * 2, axis=0).astype(BF16).astype(F32)
        m_ref[0] = jnp.sum(q8.astype(F32) * k_new, axis=-1, keepdims=True) * (ATT_SCALE * LOG2E) + jnp.zeros(
            (rows, LANES), F32)
        l_ref[0] = jnp.ones((rows, LANES), F32)
        acc_ref[0] = jnp.concatenate([new_ref[0, 1]] * 2, axis=0).astype(BF16).astype(F32)

    scores, values = [], []
    for i in range(pp):
        page = page_refs[i]
        k0 = (j * pp + i) * PAGE_SIZE
        kx = page[0, 0, :, 0].reshape(n_keys, DIFF_VD).astype(BF16)
        values.append(page[0, 0, :, 1].reshape(n_keys, DIFF_VD).astype(BF16))
        s = lax.dot_general(q8, kx, NT_DIMS, preferred_element_type=F32) * (ATT_SCALE * LOG2E)
        s = s - slope2 * ((past - k0).astype(F32) - key_in_page)
        scores.append(jnp.where(own_head, s, NEG_INF))
    _flash_step(scores, values, m_ref, l_ref, acc_ref, 0)

    @pl.when(j == pl.num_programs(1) - 1)
    def _():
        lam = _diff_lambda(lam_ref[...], lam_init)
        o = _flash_result(l_ref, acc_ref, 0)
        oh = o[:DIFF_H] - lam * o[DIFF_H:]
        y = oh * lax.rsqrt(jnp.mean(oh * oh, axis=-1, keepdims=True) + NORM_EPS)
        o_ref[0] = (y * dn_ref[...] * (1.0 - lam_init)).astype(o_ref.dtype)


def diff_decode(cache, layer, page_table, q8, slope8, new_row, lam_vec, dnorm, lam_init, past, pp=4):
    db, n_pages = page_table.shape
    rows = 2 * DIFF_H

    def page_spec(i):
        return pl.BlockSpec((1, 1) + cache.shape[2:], lambda b, j, pt: (layer, pt[b, j * pp + i], 0, 0, 0, 0))

    return pl.pallas_call(
        functools.partial(_diff_decode_kernel, pp=pp, past=past, lam_init=lam_init),
        grid_spec=pltpu.PrefetchScalarGridSpec(
            num_scalar_prefetch=1,
            grid=(db, n_pages // pp),
            in_specs=[pl.BlockSpec((1, rows, DIFF_VD), lambda b, j, pt: (b, 0, 0)),
                      pl.BlockSpec((rows, LANES), lambda b, j, pt: (0, 0)),
                      pl.BlockSpec((1, 2, DIFF_H, DIFF_VD), lambda b, j, pt: (b, 0, 0, 0)),
                      pl.BlockSpec((4, HEAD_DIM), lambda b, j, pt: (0, 0)),
                      pl.BlockSpec((1, DIFF_VD), lambda b, j, pt: (0, 0))] + [page_spec(i) for i in range(pp)],
            out_specs=pl.BlockSpec((1, DIFF_H, DIFF_VD), lambda b, j, pt: (b, 0, 0)),
            scratch_shapes=[pltpu.VMEM((1, rows, LANES), F32), pltpu.VMEM((1, rows, LANES), F32),
                            pltpu.VMEM((1, rows, DIFF_VD), F32)]),
        out_shape=jax.ShapeDtypeStruct((db, DIFF_H, DIFF_VD), BF16),
        compiler_params=_cparams(("parallel", "arbitrary")),
        name="diff_decode",
    )(page_table, q8, slope8, new_row, lam_vec, dnorm.reshape(1, DIFF_VD), *([cache] * pp))


def _ret_decode_kernel(lg_ref, q_ref, k_ref, v_ref, g_ref, rn_ref, st_ref, o_ref, nst_ref):
    ii = lax.broadcasted_iota(jnp.int32, (RET_DK, RET_DK), 0)
    jj = lax.broadcasted_iota(jnp.int32, (RET_DK, RET_DK), 1)
    ones = jnp.ones((1, 1), F32)
    for h in range(RET_H):
        gamma = jnp.exp(ones * lg_ref[h])
        q = q_ref[0, h:h + 1, :]
        k = k_ref[0, h:h + 1, :] * (RET_DK ** -0.5)
        v = v_ref[0, h:h + 1, :]
        state = st_ref[0, h]
        qb = q.astype(BF16).astype(F32)
        kb = k.astype(BF16).astype(F32)
        s = jnp.sum(qb * kb, axis=-1, keepdims=True)
        q_dec = jnp.broadcast_to(q * gamma, (SUBLANES, RET_DK)).astype(BF16)
        o = s * v + jnp.dot(q_dec, state.astype(BF16), preferred_element_type=F32)[0:1]
        k_col = jnp.sum(jnp.where(ii == jj, k, 0.0), axis=-1, keepdims=True)
        nst_ref[0, h] = state * gamma + k_col * v
        y = o * lax.rsqrt(jnp.mean(o * o, axis=-1, keepdims=True) + NORM_EPS) * rn_ref[h]
        gate = g_ref[0, h:h + 1, :]
        o_ref[0, h:h + 1, :] = (gate * jax.nn.sigmoid(gate) * y).astype(o_ref.dtype)


def ret_decode(q, k, v, g, rnorm, state, log_g):
    db = q.shape[0]
    return pl.pallas_call(
        _ret_decode_kernel,
        grid=(db,),
        in_specs=[pl.BlockSpec(memory_space=pltpu.SMEM),
                  pl.BlockSpec((1, RET_H, RET_DK), lambda b: (b, 0, 0)),
                  pl.BlockSpec((1, RET_H, RET_DK), lambda b: (b, 0, 0)),
                  pl.BlockSpec((1, RET_H, RET_DV), lambda b: (b, 0, 0)),
                  pl.BlockSpec((1, RET_H, RET_DV), lambda b: (b, 0, 0)),
                  pl.BlockSpec((RET_H, 1, RET_DV), lambda b: (0, 0, 0)),
                  pl.BlockSpec((1, RET_H, RET_DK, RET_DV), lambda b: (b, 0, 0, 0))],
        out_specs=[pl.BlockSpec((1, RET_H, RET_DV), lambda b: (b, 0, 0)),
                   pl.BlockSpec((1, RET_H, RET_DK, RET_DV), lambda b: (b, 0, 0, 0))],
        out_shape=[jax.ShapeDtypeStruct((db, RET_H, RET_DV), BF16),
                   jax.ShapeDtypeStruct(state.shape, F32)],
        compiler_params=_cparams(("parallel",)),
        name="ret_decode",
    )(log_g, q, k, v, g, rnorm.reshape(RET_H, 1, RET_DV), state)


def _alibi_slopes(n):
    return jnp.asarray([2.0 ** (-8.0 * (i + 1) / n) for i in range(n)], dtype=F32)


def _even_weights(w_in):
    d = w_in.shape[0]
    main = jnp.concatenate([w_in[:, :GATE_OFF], w_in[:, GATE_OFF + NSA_GATE_W:]], axis=1).astype(BF16)
    per_group = NSA_G * 3
    gate = w_in[:, GATE_OFF:GATE_OFF + NSA_GATE_W].reshape(d, NSA_KVH, per_group)
    gate = jnp.pad(gate, ((0, 0), (0, 0), (0, LANES - per_group))).reshape(d, NSA_KVH * LANES).astype(BF16)
    return main, gate


def _even_layer(e, layer, hp, hs, batch, seq, caches, page_table, w_in, w_out, cmp_pos, lam_vec, dnorm):
    cache_cmp, cache_sel, cache_win, cache_diff = caches
    lam_init = 0.8 - 0.6 * math.exp(-0.3 * layer)
    db = hs.shape[0]
    past = page_table.shape[1] * PAGE_SIZE
    w_main, w_gate = _even_weights(w_in)
    nsa_slopes = _alibi_slopes(NSA_H)
    diff_slopes = _alibi_slopes(DIFF_H)
    kv0 = NSA_Q_W
    d0 = NSA_Q_W + 6 * NSA_KV_W

    proj, proj_bf = matmul(hp, w_main, out_dtypes=(F32, BF16))
    gate_pre = matmul(hp, w_gate, tn=NSA_KVH * LANES)
    pos_cg = jnp.transpose(cmp_pos, (0, 2, 1)).reshape(2 * NSA_KVH, CMP_BLOCK)
    kcvc = compress_prompt(proj_bf, jnp.tile(pos_cg, (1, seq // CMP_BLOCK)), batch, seq)
    o_nsa = nsa_prompt(proj_bf, gate_pre, kcvc, nsa_slopes, batch, seq)
    o_diff = diff_prompt(proj_bf, lam_vec, dnorm, diff_slopes, lam_init, batch, seq)
    mix_p = jnp.concatenate([o_nsa, o_diff], axis=1)
    p4 = proj.reshape(batch, seq, MAIN_W)
    kv_shape = (batch, seq, 2, NSA_KVH, HEAD_DIM)
    new_p = (p4[:, :, kv0:kv0 + 2 * NSA_KV_W].reshape(kv_shape),
             p4[:, :, kv0 + 2 * NSA_KV_W:kv0 + 4 * NSA_KV_W].reshape(kv_shape),
             p4[:, seq - min(WINDOW, seq):, kv0 + 4 * NSA_KV_W:kv0 + 6 * NSA_KV_W].reshape(
                 batch, min(WINDOW, seq), 2, NSA_KVH, HEAD_DIM),
             p4[:, :, d0 + DIFF_Q_W:].reshape(batch, seq, 2, DIFF_H, DIFF_VD))

    ps, ps_bf = matmul(hs, w_main, out_dtypes=(F32, BF16))
    gate_s = matmul(hs, w_gate, tn=NSA_KVH * LANES)
    gates8 = gate_s.reshape(db, NSA_KVH, LANES)[:, :, :NSA_G * 3].reshape(db, NSA_H, 3)
    gates8 = jnp.pad(gates8, ((0, 0), (0, 0), (0, LANES - 3)))
    group_of_head = jnp.asarray(np.arange(NSA_H) // NSA_G)
    onehot_g = jax.nn.one_hot(group_of_head, NSA_KVH, dtype=BF16)
    qn = ps_bf[:, :NSA_Q_W].reshape(db, NSA_H, HEAD_DIM)
    q8 = (qn[:, :, None, :] * onehot_g[None, :, :, None]).reshape(db, NSA_H, NSA_KVH * HEAD_DIM)
    slope8 = jnp.broadcast_to(nsa_slopes[:, None], (NSA_H, LANES))
    pos_rg = jnp.tile(jnp.transpose(cmp_pos, (0, 2, 1)), (1, SUBLANES // NSA_KVH, PAGE_SIZE // CMP_BLOCK))
    pos_rows = jnp.repeat(pos_rg, NSA_KVH, axis=2)
    kcvc_s = compress_sample(cache_cmp, e, page_table, pos_rows)
    w_buf = cache_win.shape[2]
    new_cmp = ps[:, kv0:kv0 + 2 * NSA_KV_W]
    new_sel = ps[:, kv0 + 2 * NSA_KV_W:kv0 + 4 * NSA_KV_W]
    new_win = ps[:, kv0 + 4 * NSA_KV_W:kv0 + 6 * NSA_KV_W]
    win_all = jnp.concatenate([cache_win[e].reshape(db, w_buf, -1), new_win[:, None, :]], axis=1)
    n_win = w_buf + 1
    win_pad = jnp.pad(win_all, ((0, 0), (0, -n_win % LANES), (0, 0)))
    o_cmp, o_win, idx = nsa_decode_cmp(q8, slope8, kcvc_s, win_pad, past, n_win)
    ids = jnp.concatenate([idx[:, 0, :TOP_N - 1], idx[:, NSA_G, :TOP_N - 1]], axis=1)
    o_nsa_s = nsa_decode_sel(cache_sel, e, page_table, ids, q8, slope8, new_sel[:, None, :], o_cmp, o_win, gates8,
                             past)

    qd = ps_bf[:, d0:d0 + DIFF_Q_W].reshape(db, DIFF_H, 2, HEAD_DIM)
    eye_c = jnp.eye(2, dtype=BF16)
    q8d = jnp.einsum('bhcd,cj->bchjd', qd, eye_c).reshape(db, 2 * DIFF_H, DIFF_VD)
    slope8d = jnp.broadcast_to(jnp.tile(diff_slopes, 2)[:, None], (2 * DIFF_H, LANES))
    new_diff = ps[:, d0 + DIFF_Q_W:]
    o_diff_s = diff_decode(cache_diff, e, page_table, q8d, slope8d, new_diff.reshape(db, 2, DIFF_H, DIFF_VD),
                           lam_vec, dnorm, lam_init, past)
    mix_s = jnp.concatenate([o_nsa_s.reshape(db, NSA_Q_W), o_diff_s.reshape(db, DIFF_V_W)], axis=1)
    kv_s = (db, 1, 2, NSA_KVH, HEAD_DIM)
    keep = min(WINDOW, n_win)
    new_s = (new_cmp.reshape(kv_s), new_sel.reshape(kv_s),
             win_all[:, n_win - keep:].reshape(db, keep, 2, NSA_KVH, HEAD_DIM),
             new_diff.reshape(db, 1, 2, DIFF_H, DIFF_VD))
    w_out_bf = w_out.astype(BF16)
    return mix_p, mix_s, w_out_bf, new_p, new_s


def _ret_layer(hp, hs, batch, seq, state, w_in, rnorm, w_out):
    db = hs.shape[0]
    log_g = jnp.log1p(-jnp.exp2(-5.0 - jnp.arange(RET_H, dtype=F32)))
    w_in_bf = w_in.astype(BF16)
    proj = matmul(hp, w_in_bf)
    gated_p, st_p = ret_prompt(proj, rnorm, log_g, batch, seq)
    ps = matmul(hs, w_in_bf)
    hk = RET_H * RET_DK
    hv = RET_H * RET_DV
    q = ps[:, :hk].reshape(db, RET_H, RET_DK)
    k = ps[:, hk:2 * hk].reshape(db, RET_H, RET_DK)
    v = ps[:, 2 * hk:2 * hk + hv].reshape(db, RET_H, RET_DV)
    g = ps[:, 2 * hk + hv:].reshape(db, RET_H, RET_DV)
    gated_s, st_s = ret_decode(q, k, v, g, rnorm, state, log_g)
    return gated_p, gated_s.reshape(db, hv), w_out.astype(BF16), st_p, st_s


def kernel(x_prompt, x_sample, cache_nsa_cmp, cache_nsa_sel, cache_nsa_win, cache_diff, state_ret, page_table,
           norm_mix, norm_ffn, norm_final, even_w_in, even_w_out, nsa_cmp_pos, diff_lambda, diff_norm,
           ret_w_in, ret_norm, ret_w_out, ffn_w13, ffn_w2):
    batch, seq, d = x_prompt.shape
    db = x_sample.shape[0]
    xp = x_prompt.reshape(batch * seq, d)
    xs = x_sample.reshape(db, d)
    new_p = [[] for _ in range(5)]
    new_s = [[] for _ in range(5)]
    caches = (cache_nsa_cmp, cache_nsa_sel, cache_nsa_win, cache_diff)
    for layer in range(DEPTH):
        hp = rmsnorm(xp, norm_mix[layer], BF16)
        hs = rmsnorm(xs, norm_mix[layer], BF16)
        if layer % 2 == 0:
            e = layer // 2
            mix_p, mix_s, w_out, np_e, ns_e = _even_layer(
                e, layer, hp, hs, batch, seq, caches, page_table, even_w_in[e], even_w_out[e],
                nsa_cmp_pos[e], diff_lambda[e], diff_norm[e])
            for i in range(4):
                new_p[i].append(np_e[i])
                new_s[i].append(ns_e[i])
        else:
            o = layer // 2
            mix_p, mix_s, w_out, st_p, st_s = _ret_layer(hp, hs, batch, seq, state_ret[o], ret_w_in[o],
                                                         ret_norm[o], ret_w_out[o])
            new_p[4].append(st_p)
            new_s[4].append(st_s)
        xp = matmul(mix_p, w_out, res=xp)
        xs = matmul(mix_s, w_out, res=xs)
        w13 = ffn_w13[layer].astype(BF16)
        w2 = ffn_w2[layer].astype(BF16)
        xp = matmul(swiglu_up(rmsnorm(xp, norm_ffn[layer], BF16), w13), w2, res=xp)
        xs = matmul(swiglu_up(rmsnorm(xs, norm_ffn[layer], BF16), w13), w2, res=xs)
    y_prompt = rmsnorm(xp, norm_final, F32).reshape(batch, seq, d)
    y_sample = rmsnorm(xs, norm_final, F32).reshape(db, 1, d)
    return (y_prompt, y_sample, *[jnp.stack(t) for t in new_p], *[jnp.stack(t) for t in new_s])
```

```python
import functools
import math

import jax
import jax.numpy as jnp
import numpy as np
from jax import lax
from jax.experimental import pallas as pl
from jax.experimental.pallas import tpu as pltpu

F32 = jnp.float32
BF16 = jnp.bfloat16

D_MODEL = 2048
DEPTH = 4
PAGE_SIZE = 128
HEAD_DIM = 128
NSA_H = 8
NSA_KVH = 2
NSA_G = NSA_H // NSA_KVH
CMP_BLOCK = 64
CMP_SHIFT = CMP_BLOCK.bit_length() - 1
TOP_N = 16
WINDOW = 512
FORCE_SCORE = 1.0e4
DIFF_H = 4
DIFF_VD = 2 * HEAD_DIM
RET_H = 8
RET_DK = D_MODEL // RET_H
RET_DV = 2 * D_MODEL // RET_H
RET_CHUNK = 128
D_FF = ((8 * D_MODEL + 3 * 256 - 1) // (3 * 256)) * 256
NORM_EPS = 1e-6
NEG_INF = -1e30
ATT_SCALE = HEAD_DIM ** -0.5
LOG2E = math.log2(math.e)

NSA_Q_W = NSA_H * HEAD_DIM
NSA_KV_W = NSA_KVH * HEAD_DIM
NSA_GATE_W = NSA_H * 3
DIFF_Q_W = DIFF_H * 2 * HEAD_DIM
DIFF_V_W = DIFF_H * DIFF_VD
MAIN_W = NSA_Q_W + 6 * NSA_KV_W + 2 * DIFF_Q_W + DIFF_V_W
GATE_OFF = NSA_Q_W + 6 * NSA_KV_W

NSA_TQ, NSA_TK = 256, 256
DIFF_TQ, DIFF_TK = 512, 512
RET_BLOCK = 256

LANES = 128
SUBLANES = 8
VMEM_LIMIT = 56 * 1024 * 1024


def _cparams(sem):
    return pltpu.CompilerParams(dimension_semantics=sem, vmem_limit_bytes=VMEM_LIMIT)


def _rmsnorm_kernel(x_ref, g_ref, o_ref):
    x = x_ref[...]
    y = x * lax.rsqrt(jnp.mean(x * x, axis=-1, keepdims=True) + NORM_EPS)
    o_ref[...] = (y * g_ref[...]).astype(o_ref.dtype)


def rmsnorm(x, g, out_dtype):
    m, d = x.shape
    tm = min(512, m)
    return pl.pallas_call(
        _rmsnorm_kernel,
        grid=(m // tm,),
        in_specs=[pl.BlockSpec((tm, d), lambda i: (i, 0)), pl.BlockSpec((1, d), lambda i: (0, 0))],
        out_specs=pl.BlockSpec((tm, d), lambda i: (i, 0)),
        out_shape=jax.ShapeDtypeStruct((m, d), out_dtype),
        compiler_params=_cparams(("parallel",)),
        name="rmsnorm",
    )(x, g.reshape(1, d))


def _dense_kernel(x_ref, *refs, n_w, has_res, has_prev, n_out, cast_w):
    w_refs = refs[:n_w]
    refs = refs[n_w:]
    res_ref = refs[0] if has_res else None
    refs = refs[1:] if has_res else refs
    refs = refs[1:] if has_prev else refs
    out_refs = refs[:n_out]
    wb_refs = refs[n_out:]
    if cast_w:
        @pl.when(pl.program_id(1) == 0)
        def _():
            for w_ref, wb_ref in zip(w_refs, wb_refs):
                wb_ref[...] = w_ref[...].astype(BF16)
        w_refs = wb_refs
    x = x_ref[...]
    acc = jnp.dot(x, w_refs[0][...], preferred_element_type=F32)
    if n_w == 2:
        acc = acc * jax.nn.sigmoid(acc) * jnp.dot(x, w_refs[1][...], preferred_element_type=F32)
    if has_res:
        acc = res_ref[...] + acc
    for o_ref in out_refs:
        o_ref[...] = acc.astype(o_ref.dtype)


def _dense_tiles(m, k, n, n_w):
    tn = min(512, n)
    tm = min(1024 if k * n_w <= 4096 else 512, m)
    return tm, tn


def dense(x, w, n, col0=0, lead=None, col3=None, res=None, out_dtypes=(F32,), stack=None):
    m, k = x.shape
    n_w = 1 if col3 is None else 2
    tm, tn = _dense_tiles(m, k, n, n_w)
    assert m % tm == 0 and n % tn == 0 and col0 % tn == 0 and (col3 is None or col3 % tn == 0)
    cast_w = w.dtype != BF16

    def w_spec(c0):
        if w.ndim == 3:
            return pl.BlockSpec((None, k, tn), lambda j, i: (lead, 0, c0 // tn + j))
        return pl.BlockSpec((k, tn), lambda j, i: (0, c0 // tn + j))

    in_specs = [pl.BlockSpec((tm, k), lambda j, i: (i, 0)), w_spec(col0)]
    args = [x, w]
    if n_w == 2:
        in_specs.append(w_spec(col3))
        args.append(w)
    if res is not None:
        in_specs.append(pl.BlockSpec((tm, tn), lambda j, i: (i, j)))
        args.append(res)
    out_specs = [pl.BlockSpec((tm, tn), lambda j, i: (i, j)) for _ in out_dtypes]
    out_shape = [jax.ShapeDtypeStruct((m, n), dt) for dt in out_dtypes]
    aliases = {}
    has_prev = False
    if stack is not None:
        depth, slot, prev = stack
        out_specs[0] = pl.BlockSpec((None, tm, tn), lambda j, i: (slot, i, j))
        out_shape[0] = jax.ShapeDtypeStruct((depth, m, n), out_dtypes[0])
        if prev is not None:
            has_prev = True
            aliases = {len(args): 0}
            in_specs.append(pl.BlockSpec(memory_space=pl.ANY))
            args.append(prev)
    outs = pl.pallas_call(
        functools.partial(_dense_kernel, n_w=n_w, has_res=res is not None, has_prev=has_prev,
                          n_out=len(out_dtypes), cast_w=cast_w),
        grid=(n // tn, m // tm),
        in_specs=in_specs,
        out_specs=out_specs,
        out_shape=out_shape,
        scratch_shapes=[pltpu.VMEM((k, tn), BF16)] * (n_w if cast_w else 0),
        input_output_aliases=aliases,
        compiler_params=_cparams(("parallel", "arbitrary")),
        name="dense",
    )(*args)
    return outs[0] if len(out_dtypes) == 1 else outs


def _softmax_rows64(logits):
    e = jnp.exp(logits - jnp.max(logits, axis=-1, keepdims=True))
    reps = logits.shape[-1] // CMP_BLOCK
    return e * (reps / jnp.sum(e, axis=-1, keepdims=True))


def _compress_prompt_kernel(x_ref, pos_ref, o_ref, *, seq):
    w = _softmax_rows64(pos_ref[...])
    blk = lax.broadcasted_iota(jnp.int32, (LANES, seq), 0)
    key_blk = lax.broadcasted_iota(jnp.int32, (LANES, seq), 1) >> CMP_SHIFT
    onblk = blk == key_blk
    for c in range(4):
        wb = jnp.where(onblk, w[c:c + 1, :], 0.0).astype(BF16)
        o_ref[0, :, c * LANES:(c + 1) * LANES] = jnp.dot(
            wb, x_ref[:, c * LANES:(c + 1) * LANES], preferred_element_type=F32).astype(BF16)


def compress_prompt(cmp_bf, pos_tiled, batch, seq):
    return pl.pallas_call(
        functools.partial(_compress_prompt_kernel, seq=seq),
        grid=(batch,),
        in_specs=[pl.BlockSpec((seq, 4 * LANES), lambda b: (b, 0)),
                  pl.BlockSpec((4, seq), lambda b: (0, 0))],
        out_specs=pl.BlockSpec((1, LANES, 4 * LANES), lambda b: (b, 0, 0)),
        out_shape=jax.ShapeDtypeStruct((batch, LANES, 4 * LANES), BF16),
        compiler_params=_cparams(("parallel",)),
        name="nsa_compress_prompt",
    )(cmp_bf, pos_tiled)


def _lane_tile(x, width):
    return x if width == LANES else jnp.concatenate([x] * (width // LANES), axis=-1)


def _flash_step(s, v, m_ref, l_ref, acc_ref, idx):
    ss, vs = (s, v) if isinstance(s, (list, tuple)) else ([s], [v])
    tiles = [[t[:, j * LANES:(j + 1) * LANES] for j in range(t.shape[1] // LANES)] for t in ss]
    blocks = [blk for tile in tiles for blk in tile]
    mx = blocks[0]
    for blk in blocks[1:]:
        mx = jnp.maximum(mx, blk)
    m_prev = m_ref[idx]
    m_new = jnp.maximum(m_prev, jnp.max(mx, axis=-1, keepdims=True))
    alpha = jnp.exp2(m_prev - m_new)
    ps = [[jnp.exp2(blk - m_new) for blk in tile] for tile in tiles]
    row_sum = None
    for tile in ps:
        for p in tile:
            row_sum = p if row_sum is None else row_sum + p
    l_ref[idx] = alpha * l_ref[idx] + jnp.sum(row_sum, axis=-1, keepdims=True)
    m_ref[idx] = m_new
    acc = _lane_tile(alpha, vs[0].shape[-1]) * acc_ref[idx]
    for tile, vt in zip(ps, vs):
        acc = acc + jnp.dot(jnp.concatenate([p.astype(BF16) for p in tile], axis=-1), vt,
                            preferred_element_type=F32)
    acc_ref[idx] = acc


def _flash_init(m_ref, l_ref, acc_ref):
    m_ref[...] = jnp.full(m_ref.shape, NEG_INF, F32)
    l_ref[...] = jnp.zeros(l_ref.shape, F32)
    acc_ref[...] = jnp.zeros(acc_ref.shape, F32)


def _flash_result(l_ref, acc_ref, idx):
    return acc_ref[idx] / _lane_tile(l_ref[idx], acc_ref.shape[-1])


def _softmax_pv(s, v):
    p = jnp.exp2(s - jnp.max(s, axis=-1, keepdims=True))
    l = jnp.sum(p, axis=-1, keepdims=True)
    return jnp.dot(p.astype(BF16), v, preferred_element_type=F32) / l


def _nsa_prompt_kernel(slopes_ref, q_ref, gate_ref, kc_ref, vc_ref, ks_ref, vs_ref, kw_ref, vw_ref,
                       o_ref, m_ref, l_ref, acc_ref, *, tq, tk, n_blocks):
    g = pl.program_id(1)
    qi = pl.program_id(2)
    q0 = qi * tq
    row = lax.broadcasted_iota(jnp.int32, (tq, LANES), 0)
    col = lax.broadcasted_iota(jnp.int32, (tq, LANES), 1)
    qpos = q0 + row
    nt = (((1,), (1,)), ((), ()))

    kc = kc_ref[0]
    vc = vc_ref[0]
    cdist = qpos - (col * CMP_BLOCK + (CMP_BLOCK - 1))
    cmask = cdist >= 0
    cdist_f = cdist.astype(F32)
    imp = jnp.zeros((tq, LANES), F32)
    o_cmp = []
    for h in range(NSA_G):
        slope = slopes_ref[g * NSA_G + h]
        qh = q_ref[:, h * HEAD_DIM:(h + 1) * HEAD_DIM]
        s = lax.dot_general(qh, kc, nt, preferred_element_type=F32) * ATT_SCALE - slope * cdist_f
        s = jnp.where(cmask, s, NEG_INF)
        p = jnp.exp(s - jnp.max(s, axis=-1, keepdims=True))
        p = jnp.where(cmask, p / jnp.sum(p, axis=-1, keepdims=True), 0.0)
        imp = imp + p
        o_cmp.append(jnp.dot(p.astype(BF16), vc, preferred_element_type=F32))

    imp = jnp.where(cmask, imp, -1.0)
    forced = (col == (qpos >> CMP_SHIFT)) | (col == 0)
    imp = jnp.where(forced, FORCE_SCORE, imp)
    nblk = -(-n_blocks // SUBLANES) * SUBLANES
    imp_t = imp.T[:nblk]
    sub = lax.broadcasted_iota(jnp.int32, (SUBLANES, tq), 0)
    sel_rows = []
    for vi in range(nblk // SUBLANES):
        x = imp_t[vi * SUBLANES:(vi + 1) * SUBLANES]
        rank = jnp.zeros((SUBLANES, tq), F32)
        for j in range(nblk):
            r = imp_t[j:j + 1, :]
            jv, jr = divmod(j, SUBLANES)
            ge = jnp.where(r >= x, 1.0, 0.0)
            gt = jnp.where(r > x, 1.0, 0.0)
            if jv < vi:
                rank = rank + ge
            elif jv > vi:
                rank = rank + gt
            else:
                rank = rank + jnp.where(sub > jr, ge, gt)
        sel_rows.append(jnp.where(rank < TOP_N, 1.0, 0.0))
    sel_rows.append(jnp.zeros((LANES - nblk, tq), F32))
    sel = jnp.concatenate(sel_rows, axis=0).T.astype(BF16)

    c1 = ATT_SCALE * LOG2E
    slope2 = [slopes_ref[g * NSA_G + h] * LOG2E for h in range(NSA_G)]

    _flash_init(m_ref, l_ref, acc_ref)
    e_row = lax.broadcasted_iota(jnp.int32, (LANES, tk), 0)
    e_col = lax.broadcasted_iota(jnp.int32, (LANES, tk), 1)
    rel = lax.broadcasted_iota(jnp.int32, (1, tk), 1)
    krow = lax.broadcasted_iota(jnp.int32, (tq, tk), 0)
    kcol = lax.broadcasted_iota(jnp.int32, (tq, tk), 1)

    def sel_tile(kt, diagonal):
        k0 = pl.multiple_of(kt * tk, tk)
        k = ks_ref[pl.ds(k0, tk), :]
        v = vs_ref[pl.ds(k0, tk), :]
        expand = jnp.where(((k0 + e_col) >> CMP_SHIFT) == e_row, 1.0, 0.0).astype(BF16)
        picked = jnp.dot(sel, expand, preferred_element_type=F32)
        if diagonal:
            picked = jnp.where(k0 + kcol <= q0 + krow, picked, 0.0)
        valid = picked > 0.5
        bias = (k0 - q0 + rel).astype(F32)
        for h in range(NSA_G):
            qh = q_ref[:, h * HEAD_DIM:(h + 1) * HEAD_DIM]
            s = lax.dot_general(qh, k, nt, preferred_element_type=F32) * c1 + slope2[h] * bias
            _flash_step(jnp.where(valid, s, NEG_INF), v, m_ref, l_ref, acc_ref, h)

    kd = q0 // tk
    sel_tile(kd, True)

    def sel_body(i, carry):
        sel_tile(kd - 1 - i, False)
        return carry

    lax.fori_loop(0, kd, sel_body, 0)
    o_sel = [_flash_result(l_ref, acc_ref, h) for h in range(NSA_G)]

    n_band = WINDOW + tq
    w0 = pl.multiple_of(jnp.maximum(q0 - WINDOW, 0), LANES)
    kw = kw_ref[pl.ds(w0, n_band), :]
    vw = vw_ref[pl.ds(w0, n_band), :]
    wdist = (q0 + lax.broadcasted_iota(jnp.int32, (tq, n_band), 0)) - (
        w0 + lax.broadcasted_iota(jnp.int32, (tq, n_band), 1))
    wvalid = (wdist >= 0) & (wdist <= WINDOW)
    wbias = (w0 - q0 + lax.broadcasted_iota(jnp.int32, (1, n_band), 1)).astype(F32)
    o_win = []
    for h in range(NSA_G):
        qh = q_ref[:, h * HEAD_DIM:(h + 1) * HEAD_DIM]
        s = lax.dot_general(qh, kw, nt, preferred_element_type=F32) * c1 + slope2[h] * wbias
        o_win.append(_softmax_pv(jnp.where(wvalid, s, NEG_INF), vw))

    gates = jax.nn.sigmoid(gate_ref[...])
    for h in range(NSA_G):
        o = (gates[:, 3 * h:3 * h + 1] * o_cmp[h] + gates[:, 3 * h + 1:3 * h + 2] * o_sel[h]
             + gates[:, 3 * h + 2:3 * h + 3] * o_win[h])
        o_ref[:, h * HEAD_DIM:(h + 1) * HEAD_DIM] = o.astype(o_ref.dtype)


def nsa_prompt(qn_bf, gate_pre, kcvc, sel_bf, win_bf, slopes, batch, seq, tq=NSA_TQ, tk=NSA_TK):
    assert tk % tq == 0 and seq % tk == 0 and seq >= WINDOW + tq and seq // CMP_BLOCK <= LANES
    nq = seq // tq
    gw = NSA_G * HEAD_DIM

    def kv_spec(kv):
        return pl.BlockSpec((seq, HEAD_DIM), lambda b, g, i: (b, NSA_KVH * kv + g))

    return pl.pallas_call(
        functools.partial(_nsa_prompt_kernel, tq=tq, tk=tk, n_blocks=seq // CMP_BLOCK),
        grid=(batch, NSA_KVH, nq),
        in_specs=[pl.BlockSpec(memory_space=pltpu.SMEM),
                  pl.BlockSpec((tq, gw), lambda b, g, i: (b * nq + i, g)),
                  pl.BlockSpec((tq, LANES), lambda b, g, i: (b * nq + i, g)),
                  pl.BlockSpec((1, LANES, HEAD_DIM), lambda b, g, i: (b, 0, g)),
                  pl.BlockSpec((1, LANES, HEAD_DIM), lambda b, g, i: (b, 0, 2 + g)),
                  kv_spec(0), kv_spec(1), kv_spec(0), kv_spec(1)],
        out_specs=pl.BlockSpec((tq, gw), lambda b, g, i: (b * nq + i, g)),
        out_shape=jax.ShapeDtypeStruct((batch * seq, NSA_Q_W), BF16),
        scratch_shapes=[pltpu.VMEM((NSA_G, tq, LANES), F32), pltpu.VMEM((NSA_G, tq, LANES), F32),
                        pltpu.VMEM((NSA_G, tq, HEAD_DIM), F32)],
        compiler_params=_cparams(("parallel", "parallel", "arbitrary")),
        name="nsa_prompt",
    )(slopes, qn_bf, gate_pre, kcvc, kcvc, sel_bf, sel_bf, win_bf, win_bf)


def _diff_lambda(lv, lam_init):
    a = jnp.sum(lv[0:1] * lv[1:2], axis=-1, keepdims=True)
    b = jnp.sum(lv[2:3] * lv[3:4], axis=-1, keepdims=True)
    return jnp.exp(a) - jnp.exp(b) + lam_init


def _diff_prompt_kernel(slopes_ref, q_ref, k_ref, v_ref, lam_ref, dn_ref, o_ref, m_ref, l_ref, acc_ref,
                        *, tq, tk, lam_init):
    h = pl.program_id(1)
    qi = pl.program_id(2)
    q0 = qi * tq
    krow = lax.broadcasted_iota(jnp.int32, (tq, tk), 0)
    kcol = lax.broadcasted_iota(jnp.int32, (tq, tk), 1)
    rel = lax.broadcasted_iota(jnp.int32, (1, tk), 1)
    c1 = ATT_SCALE * LOG2E
    slope2 = slopes_ref[h] * LOG2E
    nt = (((1,), (1,)), ((), ()))
    _flash_init(m_ref, l_ref, acc_ref)

    def tile(kt, diagonal):
        k0 = pl.multiple_of(kt * tk, tk)
        k = k_ref[pl.ds(k0, tk), :]
        v = v_ref[pl.ds(k0, tk), :]
        bias = slope2 * (k0 - q0 + rel).astype(F32)
        for c in range(2):
            s = lax.dot_general(q_ref[:, c * HEAD_DIM:(c + 1) * HEAD_DIM], k[:, c * HEAD_DIM:(c + 1) * HEAD_DIM],
                                nt, preferred_element_type=F32) * c1 + bias
            if diagonal:
                s = jnp.where(k0 + kcol <= q0 + krow, s, NEG_INF)
            _flash_step(s, v, m_ref, l_ref, acc_ref, c)

    kd = q0 // tk
    tile(kd, True)

    def body(i, carry):
        tile(kd - 1 - i, False)
        return carry

    lax.fori_loop(0, kd, body, 0)
    lam = _diff_lambda(lam_ref[...], lam_init)
    o = _flash_result(l_ref, acc_ref, 0) - lam * _flash_result(l_ref, acc_ref, 1)
    y = o * lax.rsqrt(jnp.mean(o * o, axis=-1, keepdims=True) + NORM_EPS)
    o_ref[...] = (y * dn_ref[...] * (1.0 - lam_init)).astype(o_ref.dtype)


def diff_prompt(qd_bf, kvd_bf, lam_vec, dnorm, slopes, lam_init, batch, seq, tq=DIFF_TQ, tk=DIFF_TK):
    assert tk % tq == 0 and seq % tk == 0
    nq = seq // tq
    w = 2 * HEAD_DIM
    qb, kb, vb = 0, 0, DIFF_H
    return pl.pallas_call(
        functools.partial(_diff_prompt_kernel, tq=tq, tk=tk, lam_init=lam_init),
        grid=(batch, DIFF_H, nq),
        in_specs=[pl.BlockSpec(memory_space=pltpu.SMEM),
                  pl.BlockSpec((tq, w), lambda b, h, i: (b * nq + i, qb + h)),
                  pl.BlockSpec((seq, w), lambda b, h, i: (b, kb + h)),
                  pl.BlockSpec((seq, w), lambda b, h, i: (b, vb + h)),
                  pl.BlockSpec((4, HEAD_DIM), lambda b, h, i: (0, 0)),
                  pl.BlockSpec((1, DIFF_VD), lambda b, h, i: (0, 0))],
        out_specs=pl.BlockSpec((tq, w), lambda b, h, i: (b * nq + i, h)),
        out_shape=jax.ShapeDtypeStruct((batch * seq, DIFF_V_W), BF16),
        scratch_shapes=[pltpu.VMEM((2, tq, LANES), F32), pltpu.VMEM((2, tq, LANES), F32),
                        pltpu.VMEM((2, tq, DIFF_VD), F32)],
        compiler_params=_cparams(("parallel", "parallel", "arbitrary")),
        name="diff_prompt",
    )(slopes, qd_bf, kvd_bf, kvd_bf, lam_vec, dnorm.reshape(1, DIFF_VD))


def _ret_prompt_kernel(lg_ref, q_ref, k_ref, v_ref, g_ref, rn_ref, o_ref, st_ref, *, chunk):
    h = pl.program_id(1)
    c = pl.program_id(2)
    lg = lg_ref[h]

    @pl.when(c == 0)
    def _():
        st_ref[...] = jnp.zeros(st_ref.shape, F32)

    ii = lax.broadcasted_iota(jnp.int32, (chunk, chunk), 0)
    jj = lax.broadcasted_iota(jnp.int32, (chunk, chunk), 1)
    d = (ii - jj).astype(F32)
    decay = jnp.where(d >= 0, jnp.exp(lg * jnp.maximum(d, 0.0)), 0.0)
    ik = lax.broadcasted_iota(jnp.int32, (chunk, RET_DK), 0).astype(F32)
    q_dec = jnp.exp(lg * (ik + 1.0))
    k_dec = jnp.exp(lg * (chunk - 1.0 - ik))
    ones = jnp.ones((1, 1), F32)
    g_chunk = jnp.exp(ones * (lg * chunk))

    q = q_ref[...]
    k = k_ref[...] * (RET_DK ** -0.5)
    v = v_ref[...].astype(BF16)
    state = st_ref[0, 0]
    s = lax.dot_general(q.astype(BF16), k.astype(BF16), (((1,), (1,)), ((), ())),
                        preferred_element_type=F32) * decay
    o = jnp.dot(s.astype(BF16), v, preferred_element_type=F32)
    o = o + jnp.dot((q * q_dec).astype(BF16), state.astype(BF16), preferred_element_type=F32)
    kv = lax.dot_general((k * k_dec).astype(BF16), v, (((0,), (0,)), ((), ())), preferred_element_type=F32)
    st_ref[0, 0] = state * g_chunk + kv

    y = o * lax.rsqrt(jnp.mean(o * o, axis=-1, keepdims=True) + NORM_EPS) * rn_ref[0]
    gate = g_ref[...]
    o_ref[...] = (gate * jax.nn.sigmoid(gate) * y).astype(o_ref.dtype)


def ret_prompt(proj, rnorm, log_g, batch, seq, chunk=RET_BLOCK):
    nc = seq // chunk
    kb = RET_H
    vb = 2 * RET_H * RET_DK // RET_DV
    gb = vb + RET_H
    return pl.pallas_call(
        functools.partial(_ret_prompt_kernel, chunk=chunk),
        grid=(batch, RET_H, nc),
        in_specs=[pl.BlockSpec(memory_space=pltpu.SMEM),
                  pl.BlockSpec((chunk, RET_DK), lambda b, h, c: (b * nc + c, h)),
                  pl.BlockSpec((chunk, RET_DK), lambda b, h, c: (b * nc + c, kb + h)),
                  pl.BlockSpec((chunk, RET_DV), lambda b, h, c: (b * nc + c, vb + h)),
                  pl.BlockSpec((chunk, RET_DV), lambda b, h, c: (b * nc + c, gb + h)),
                  pl.BlockSpec((1, 1, RET_DV), lambda b, h, c: (h, 0, 0))],
        out_specs=[pl.BlockSpec((chunk, RET_DV), lambda b, h, c: (b * nc + c, h)),
                   pl.BlockSpec((1, 1, RET_DK, RET_DV), lambda b, h, c: (b, h, 0, 0))],
        out_shape=[jax.ShapeDtypeStruct((batch * seq, RET_H * RET_DV), BF16),
                   jax.ShapeDtypeStruct((batch, RET_H, RET_DK, RET_DV), F32)],
        compiler_params=_cparams(("parallel", "parallel", "arbitrary")),
        name="ret_prompt",
    )(log_g, proj, proj, proj, proj, rnorm.reshape(RET_H, 1, RET_DV))


NT_DIMS = (((1,), (1,)), ((), ()))


def _compress_sample_kernel(pt_ref, pos_ref, *refs, pp):
    del pt_ref
    o_ref = refs[pp]
    per_page = PAGE_SIZE // CMP_BLOCK
    n_rows = PAGE_SIZE * NSA_KVH
    row = lax.broadcasted_iota(jnp.int32, (SUBLANES, n_rows), 0)
    col = lax.broadcasted_iota(jnp.int32, (SUBLANES, n_rows), 1)
    g_shift = NSA_KVH.bit_length() - 1
    member = ((col & (NSA_KVH - 1)) == (row & (NSA_KVH - 1))) & ((col >> (g_shift + CMP_SHIFT)) == (row >> g_shift))
    weights = []
    for kv in range(2):
        logits = jnp.where(member, pos_ref[kv], NEG_INF)
        e = jnp.exp(logits - jnp.max(logits, axis=-1, keepdims=True))
        weights.append((e / jnp.sum(e, axis=-1, keepdims=True)).astype(BF16))
    for i in range(pp):
        for kv in range(2):
            x = refs[i][0, 0, :, kv].reshape(n_rows, HEAD_DIM).astype(BF16)
            res = jnp.dot(weights[kv], x, preferred_element_type=F32)
            for half in range(per_page):
                for g in range(NSA_KVH):
                    c0 = (kv * NSA_KVH + g) * HEAD_DIM
                    o_ref[0, i * per_page + half:i * per_page + half + 1, c0:c0 + HEAD_DIM] = (
                        res[half * NSA_KVH + g:half * NSA_KVH + g + 1])


def compress_sample(cache, layer, page_table, pos_rows, pp=8):
    db, n_pages = page_table.shape
    per_page = PAGE_SIZE // CMP_BLOCK
    width = 2 * NSA_KVH * HEAD_DIM
    assert per_page * NSA_KVH <= SUBLANES

    def page_spec(i):
        return pl.BlockSpec((1, 1) + cache.shape[2:], lambda b, j, pt: (layer, pt[b, j * pp + i], 0, 0, 0, 0))

    return pl.pallas_call(
        functools.partial(_compress_sample_kernel, pp=pp),
        grid_spec=pltpu.PrefetchScalarGridSpec(
            num_scalar_prefetch=1,
            grid=(db, n_pages // pp),
            in_specs=[pl.BlockSpec(pos_rows.shape, lambda b, j, pt: (0, 0, 0))] + [page_spec(i) for i in range(pp)],
            out_specs=pl.BlockSpec((1, pp * per_page, width), lambda b, j, pt: (b, j, 0))),
        out_shape=jax.ShapeDtypeStruct((db, n_pages * per_page, width), F32),
        compiler_params=_cparams(("parallel", "arbitrary")),
        name="nsa_compress_sample",
    )(page_table, pos_rows, *([cache] * pp))


def _pick_group(x):
    row = lax.broadcasted_iota(jnp.int32, (NSA_H, HEAD_DIM), 0)
    return jnp.where(row < NSA_G, x[:, :HEAD_DIM], x[:, HEAD_DIM:])


def _nsa_decode_cmp_kernel(q_ref, slope_ref, kcvc_ref, win_ref, ocmp_ref, owin_ref, idx_ref, *, past, n_win):
    q8 = q_ref[0]
    kcvc = kcvc_ref[0]
    nb = kcvc.shape[0]
    kvw = NSA_KVH * HEAD_DIM
    lane = lax.broadcasted_iota(jnp.int32, (NSA_H, nb), 1)
    row = lax.broadcasted_iota(jnp.int32, (NSA_H, nb), 0)
    slope = slope_ref[...][:, :1]
    cdist = (past - (lane * CMP_BLOCK + CMP_BLOCK - 1)).astype(F32)
    s = lax.dot_general(q8, kcvc[:, :kvw].astype(BF16), NT_DIMS, preferred_element_type=F32) * ATT_SCALE
    s = s - slope * cdist
    p = jnp.exp(s - jnp.max(s, axis=-1, keepdims=True))
    p = p / jnp.sum(p, axis=-1, keepdims=True)
    ocmp_ref[0] = _pick_group(jnp.dot(p.astype(BF16), kcvc[:, kvw:].astype(BF16), preferred_element_type=F32))

    g0 = p[0:1] + p[1:2] + p[2:3] + p[3:4]
    g1 = p[4:5] + p[5:6] + p[6:7] + p[7:8]
    x = jnp.where(row < NSA_G, g0, g1)
    x = jnp.where(lane == 0, FORCE_SCORE, x)
    out_lane = lax.broadcasted_iota(jnp.int32, (NSA_H, LANES), 1)
    lane_f = lane.astype(F32)
    picked = jnp.zeros((NSA_H, LANES), F32)
    for t in range(TOP_N - 1):
        mx = jnp.max(x, axis=-1, keepdims=True)
        idx = jnp.min(jnp.where(x == mx, lane_f, float(nb)), axis=-1, keepdims=True)
        picked = jnp.where(out_lane == t, idx, picked)
        x = jnp.where(lane_f == idx, -2.0, x)
    idx_ref[0] = picked.astype(jnp.int32)

    win = win_ref[0]
    nw = win.shape[0]
    wl = lax.broadcasted_iota(jnp.int32, (NSA_H, nw), 1)
    wdist = (n_win - 1 - wl)
    valid = wdist >= 0
    s = lax.dot_general(q8, win[:, :kvw].astype(BF16), NT_DIMS, preferred_element_type=F32) * ATT_SCALE
    s = jnp.where(valid, s - slope * wdist.astype(F32), NEG_INF)
    p = jnp.exp(s - jnp.max(s, axis=-1, keepdims=True))
    p = p / jnp.sum(p, axis=-1, keepdims=True)
    owin_ref[0] = _pick_group(jnp.dot(p.astype(BF16), win[:, kvw:].astype(BF16), preferred_element_type=F32))


def nsa_decode_cmp(q8, slope8, kcvc, win_all, past, n_win):
    db = q8.shape[0]
    nb = kcvc.shape[1]
    nw = win_all.shape[1]
    w = kcvc.shape[2]
    head_out = jax.ShapeDtypeStruct((db, NSA_H, HEAD_DIM), F32)
    head_spec = pl.BlockSpec((1, NSA_H, HEAD_DIM), lambda b: (b, 0, 0))
    return pl.pallas_call(
        functools.partial(_nsa_decode_cmp_kernel, past=past, n_win=n_win),
        grid=(db,),
        in_specs=[pl.BlockSpec((1, NSA_H, NSA_KVH * HEAD_DIM), lambda b: (b, 0, 0)),
                  pl.BlockSpec((NSA_H, LANES), lambda b: (0, 0)),
                  pl.BlockSpec((1, nb, w), lambda b: (b, 0, 0)),
                  pl.BlockSpec((1, nw, w), lambda b: (b, 0, 0))],
        out_specs=[head_spec, head_spec, pl.BlockSpec((1, NSA_H, LANES), lambda b: (b, 0, 0))],
        out_shape=[head_out, head_out, jax.ShapeDtypeStruct((db, NSA_H, LANES), jnp.int32)],
        compiler_params=_cparams(("parallel",)),
        name="nsa_decode_cmp",
    )(q8, slope8, kcvc, win_all)


def _nsa_decode_sel_kernel(pt_ref, ids_ref, q_ref, slope_ref, new_ref, pa_ref, pb_ref, ocmp_ref, owin_ref,
                           gate_ref, o_ref, m_ref, l_ref, acc_ref, *, past, n_sel):
    del pt_ref
    b = pl.program_id(0)
    t = pl.program_id(1)
    q8 = q_ref[0]
    row = lax.broadcasted_iota(jnp.int32, (NSA_H, LANES), 0)
    lane = lax.broadcasted_iota(jnp.int32, (NSA_H, LANES), 1)
    slope = slope_ref[...][:, :1]
    kvw = NSA_KVH * HEAD_DIM

    @pl.when(t == 0)
    def _():
        new = new_ref[0].astype(BF16).astype(F32)
        m_ref[...] = jnp.sum(q8.astype(F32) * new[:, :kvw], axis=-1, keepdims=True) * ATT_SCALE + jnp.zeros(
            (NSA_H, LANES), F32)
        l_ref[...] = jnp.ones((NSA_H, LANES), F32)
        acc_ref[...] = jnp.where(row < NSA_G, new[:, kvw:kvw + HEAD_DIM], new[:, kvw + HEAD_DIM:])

    na = ids_ref[b, t]
    nbk = ids_ref[b, n_sel + t]
    blk = jnp.where(row < NSA_G, na, nbk)
    per_page = PAGE_SIZE // CMP_BLOCK
    valid = (lane >> CMP_SHIFT) == (blk & (per_page - 1))
    dist = past - ((blk >> (per_page.bit_length() - 1)) * PAGE_SIZE + lane)
    sa = lax.dot_general(q8[:, :HEAD_DIM], pa_ref[0, 0, :, 0, 0, :].astype(BF16), NT_DIMS,
                         preferred_element_type=F32)
    sb = lax.dot_general(q8[:, HEAD_DIM:], pb_ref[0, 0, :, 0, 1, :].astype(BF16), NT_DIMS,
                         preferred_element_type=F32)
    s = jnp.where(row < NSA_G, sa, sb) * ATT_SCALE - slope * dist.astype(F32)
    s = jnp.where(valid, s, NEG_INF)
    m_prev = m_ref[...]
    m_new = jnp.maximum(m_prev, jnp.max(s, axis=-1, keepdims=True))
    alpha = jnp.exp(m_prev - m_new)
    p = jnp.exp(s - m_new)
    l_ref[...] = alpha * l_ref[...] + jnp.sum(p, axis=-1, keepdims=True)
    m_ref[...] = m_new
    pb16 = p.astype(BF16)
    oa = jnp.dot(pb16, pa_ref[0, 0, :, 1, 0, :].astype(BF16), preferred_element_type=F32)
    ob = jnp.dot(pb16, pb_ref[0, 0, :, 1, 1, :].astype(BF16), preferred_element_type=F32)
    acc_ref[...] = alpha * acc_ref[...] + jnp.where(row < NSA_G, oa, ob)

    @pl.when(t == n_sel - 1)
    def _():
        gates = jax.nn.sigmoid(gate_ref[0])
        o = (gates[:, 0:1] * ocmp_ref[0] + gates[:, 1:2] * (acc_ref[...] / l_ref[...])
             + gates[:, 2:3] * owin_ref[0])
        o_ref[0] = o.astype(o_ref.dtype)


def nsa_decode_sel(cache, layer, page_table, ids, q8, slope8, new_row, o_cmp, o_win, gates, past):
    db = q8.shape[0]
    n_sel = ids.shape[1] // NSA_KVH
    width = 2 * NSA_KVH * HEAD_DIM
    per_page = PAGE_SIZE // CMP_BLOCK
    head_spec = pl.BlockSpec((1, NSA_H, HEAD_DIM), lambda b, t, pt, ids: (b, 0, 0))

    def page_spec(g):
        return pl.BlockSpec((1, 1) + cache.shape[2:],
                            lambda b, t, pt, ids: (layer, pt[b, ids[b, g * n_sel + t] // per_page], 0, 0, 0, 0))

    return pl.pallas_call(
        functools.partial(_nsa_decode_sel_kernel, past=past, n_sel=n_sel),
        grid_spec=pltpu.PrefetchScalarGridSpec(
            num_scalar_prefetch=2,
            grid=(db, n_sel),
            in_specs=[pl.BlockSpec((1, NSA_H, NSA_KVH * HEAD_DIM), lambda b, t, pt, ids: (b, 0, 0)),
                      pl.BlockSpec((NSA_H, LANES), lambda b, t, pt, ids: (0, 0)),
                      pl.BlockSpec((1, 1, width), lambda b, t, pt, ids: (b, 0, 0)),
                      page_spec(0), page_spec(1), head_spec, head_spec, head_spec],
            out_specs=head_spec,
            scratch_shapes=[pltpu.VMEM((NSA_H, LANES), F32), pltpu.VMEM((NSA_H, LANES), F32),
                            pltpu.VMEM((NSA_H, HEAD_DIM), F32)]),
        out_shape=jax.ShapeDtypeStruct((db, NSA_H, HEAD_DIM), BF16),
        compiler_params=_cparams(("parallel", "arbitrary")),
        name="nsa_decode_sel",
    )(page_table, ids, q8, slope8, new_row, cache, cache, o_cmp, o_win, gates)


def _diff_decode_kernel(pt_ref, q_ref, slope_ref, new_ref, lam_ref, dn_ref, *refs, pp, past, lam_init):
    del pt_ref
    page_refs = refs[:pp]
    o_ref, m_ref, l_ref, acc_ref = refs[pp:]
    j = pl.program_id(1)
    rows = 2 * DIFF_H
    n_keys = PAGE_SIZE * DIFF_H
    h_shift = DIFF_H.bit_length() - 1
    q8 = q_ref[0]
    slope2 = slope_ref[...][:, :1] * LOG2E
    row = lax.broadcasted_iota(jnp.int32, (rows, n_keys), 0)
    col = lax.broadcasted_iota(jnp.int32, (rows, n_keys), 1)
    own_head = (col & (DIFF_H - 1)) == (row & (DIFF_H - 1))
    key_in_page = (col >> h_shift).astype(F32)

    @pl.when(j == 0)
    def _():
        k_new = jnp.concatenate([new_ref[0, 0]] * 2, axis=0).astype(BF16).astype(F32)
        m_ref[0] = jnp.sum(q8.astype(F32) * k_new, axis=-1, keepdims=True) * (ATT_SCALE * LOG2E) + jnp.zeros(
            (rows, LANES), F32)
        l_ref[0] = jnp.ones((rows, LANES), F32)
        acc_ref[0] = jnp.concatenate([new_ref[0, 1]] * 2, axis=0).astype(BF16).astype(F32)

    scores, values = [], []
    for i in range(pp):
        page = page_refs[i]
        k0 = (j * pp + i) * PAGE_SIZE
        kx = page[0, 0, :, 0].reshape(n_keys, DIFF_VD).astype(BF16)
        values.append(page[0, 0, :, 1].reshape(n_keys, DIFF_VD).astype(BF16))
        s = lax.dot_general(q8, kx, NT_DIMS, preferred_element_type=F32) * (ATT_SCALE * LOG2E)
        s = s - slope2 * ((past - k0).astype(F32) - key_in_page)
        scores.append(jnp.where(own_head, s, NEG_INF))
    _flash_step(scores, values, m_ref, l_ref, acc_ref, 0)

    @pl.when(j == pl.num_programs(1) - 1)
    def _():
        lam = _diff_lambda(lam_ref[...], lam_init)
        o = _flash_result(l_ref, acc_ref, 0)
        oh = o[:DIFF_H] - lam * o[DIFF_H:]
        y = oh * lax.rsqrt(jnp.mean(oh * oh, axis=-1, keepdims=True) + NORM_EPS)
        o_ref[0] = (y * dn_ref[...] * (1.0 - lam_init)).astype(o_ref.dtype)


def diff_decode(cache, layer, page_table, q8, slope8, new_row, lam_vec, dnorm, lam_init, past, pp=4):
    db, n_pages = page_table.shape
    rows = 2 * DIFF_H

    def page_spec(i):
        return pl.BlockSpec((1, 1) + cache.shape[2:], lambda b, j, pt: (layer, pt[b, j * pp + i], 0, 0, 0, 0))

    return pl.pallas_call(
        functools.partial(_diff_decode_kernel, pp=pp, past=past, lam_init=lam_init),
        grid_spec=pltpu.PrefetchScalarGridSpec(
            num_scalar_prefetch=1,
            grid=(db, n_pages // pp),
            in_specs=[pl.BlockSpec((1, rows, DIFF_VD), lambda b, j, pt: (b, 0, 0)),
                      pl.BlockSpec((rows, LANES), lambda b, j, pt: (0, 0)),
                      pl.BlockSpec((1, 2, DIFF_H, DIFF_VD), lambda b, j, pt: (b, 0, 0, 0)),
                      pl.BlockSpec((4, HEAD_DIM), lambda b, j, pt: (0, 0)),
                      pl.BlockSpec((1, DIFF_VD), lambda b, j, pt: (0, 0))] + [page_spec(i) for i in range(pp)],
            out_specs=pl.BlockSpec((1, DIFF_H, DIFF_VD), lambda b, j, pt: (b, 0, 0)),
            scratch_shapes=[pltpu.VMEM((1, rows, LANES), F32), pltpu.VMEM((1, rows, LANES), F32),
                            pltpu.VMEM((1, rows, DIFF_VD), F32)]),
        out_shape=jax.ShapeDtypeStruct((db, DIFF_H, DIFF_VD), BF16),
        compiler_params=_cparams(("parallel", "arbitrary")),
        name="diff_decode",
    )(page_table, q8, slope8, new_row, lam_vec, dnorm.reshape(1, DIFF_VD), *([cache] * pp))


def _ret_decode_kernel(lg_ref, q_ref, k_ref, v_ref, g_ref, rn_ref, st_ref, o_ref, nst_ref):
    ii = lax.broadcasted_iota(jnp.int32, (RET_DK, RET_DK), 0)
    jj = lax.broadcasted_iota(jnp.int32, (RET_DK, RET_DK), 1)
    ones = jnp.ones((1, 1), F32)
    for h in range(RET_H):
        gamma = jnp.exp(ones * lg_ref[h])
        q = q_ref[0, h:h + 1, :]
        k = k_ref[0, h:h + 1, :] * (RET_DK ** -0.5)
        v = v_ref[0, h:h + 1, :]
        state = st_ref[0, h]
        qb = q.astype(BF16).astype(F32)
        kb = k.astype(BF16).astype(F32)
        s = jnp.sum(qb * kb, axis=-1, keepdims=True)
        q_dec = jnp.broadcast_to(q * gamma, (SUBLANES, RET_DK)).astype(BF16)
        o = s * v + jnp.dot(q_dec, state.astype(BF16), preferred_element_type=F32)[0:1]
        k_col = jnp.sum(jnp.where(ii == jj, k, 0.0), axis=-1, keepdims=True)
        nst_ref[0, h] = state * gamma + k_col * v
        y = o * lax.rsqrt(jnp.mean(o * o, axis=-1, keepdims=True) + NORM_EPS) * rn_ref[h]
        gate = g_ref[0, h:h + 1, :]
        o_ref[0, h:h + 1, :] = (gate * jax.nn.sigmoid(gate) * y).astype(o_ref.dtype)


def ret_decode(q, k, v, g, rnorm, state, log_g):
    db = q.shape[0]
    return pl.pallas_call(
        _ret_decode_kernel,
        grid=(db,),
        in_specs=[pl.BlockSpec(memory_space=pltpu.SMEM),
                  pl.BlockSpec((1, RET_H, RET_DK), lambda b: (b, 0, 0)),
                  pl.BlockSpec((1, RET_H, RET_DK), lambda b: (b, 0, 0)),
                  pl.BlockSpec((1, RET_H, RET_DV), lambda b: (b, 0, 0)),
                  pl.BlockSpec((1, RET_H, RET_DV), lambda b: (b, 0, 0)),
                  pl.BlockSpec((RET_H, 1, RET_DV), lambda b: (0, 0, 0)),
                  pl.BlockSpec((1, RET_H, RET_DK, RET_DV), lambda b: (b, 0, 0, 0))],
        out_specs=[pl.BlockSpec((1, RET_H, RET_DV), lambda b: (b, 0, 0)),
                   pl.BlockSpec((1, RET_H, RET_DK, RET_DV), lambda b: (b, 0, 0, 0))],
        out_shape=[jax.ShapeDtypeStruct((db, RET_H, RET_DV), BF16),
                   jax.ShapeDtypeStruct(state.shape, F32)],
        compiler_params=_cparams(("parallel",)),
        name="ret_decode",
    )(log_g, q, k, v, g, rnorm.reshape(RET_H, 1, RET_DV), state)


def _alibi_slopes(n):
    return jnp.asarray([2.0 ** (-8.0 * (i + 1) / n) for i in range(n)], dtype=F32)


def _even_layer(e, layer, hp, hs, batch, seq, caches, page_table, even_w_in, cmp_pos, lam_vec, dnorm, stacks):
    cache_cmp, cache_sel, cache_win, cache_diff = caches
    n_even = even_w_in.shape[0]
    d = even_w_in.shape[1]
    lam_init = 0.8 - 0.6 * math.exp(-0.3 * layer)
    db = hs.shape[0]
    past = page_table.shape[1] * PAGE_SIZE
    nsa_slopes = _alibi_slopes(NSA_H)
    diff_slopes = _alibi_slopes(DIFF_H)
    kvw = 2 * NSA_KV_W
    per_group = NSA_G * 3
    w_gate = even_w_in[e, :, GATE_OFF:GATE_OFF + NSA_GATE_W].reshape(d, NSA_KVH, per_group)
    w_gate = jnp.pad(w_gate, ((0, 0), (0, 0), (0, LANES - per_group))).reshape(d, NSA_KVH * LANES).astype(BF16)
    w_tail = even_w_in[e, :, GATE_OFF + NSA_GATE_W:].astype(BF16)

    qn_bf = dense(hp, even_w_in, NSA_Q_W, lead=e, out_dtypes=(BF16,))
    cmp_st, cmp_bf = dense(hp, even_w_in, kvw, col0=NSA_Q_W, lead=e, out_dtypes=(F32, BF16),
                           stack=(n_even, e, stacks[0]))
    sel_st, sel_bf = dense(hp, even_w_in, kvw, col0=NSA_Q_W + kvw, lead=e, out_dtypes=(F32, BF16),
                           stack=(n_even, e, stacks[1]))
    win_f, win_bf = dense(hp, even_w_in, kvw, col0=NSA_Q_W + 2 * kvw, lead=e, out_dtypes=(F32, BF16))
    gate_pre = dense(hp, w_gate, NSA_KVH * LANES)
    qd_bf = dense(hp, w_tail, DIFF_Q_W, out_dtypes=(BF16,))
    kvd_st, kvd_bf = dense(hp, w_tail, DIFF_Q_W + DIFF_V_W, col0=DIFF_Q_W, out_dtypes=(F32, BF16),
                           stack=(n_even, e, stacks[2]))
    pos_cg = jnp.transpose(cmp_pos, (0, 2, 1)).reshape(2 * NSA_KVH, CMP_BLOCK)
    kcvc = compress_prompt(cmp_bf, jnp.tile(pos_cg, (1, seq // CMP_BLOCK)), batch, seq)
    o_nsa = nsa_prompt(qn_bf, gate_pre, kcvc, sel_bf, win_bf, nsa_slopes, batch, seq)
    o_diff = diff_prompt(qd_bf, kvd_bf, lam_vec, dnorm, diff_slopes, lam_init, batch, seq)
    mix_p = jnp.concatenate([o_nsa, o_diff], axis=1)
    keep_p = min(WINDOW, seq)
    win_p = win_f.reshape(batch, seq, kvw)[:, seq - keep_p:].reshape(batch, keep_p, 2, NSA_KVH, HEAD_DIM)

    ps, ps_bf = dense(hs, even_w_in, GATE_OFF, lead=e, out_dtypes=(F32, BF16))
    gate_s = dense(hs, w_gate, NSA_KVH * LANES)
    pt, pt_bf = dense(hs, w_tail, w_tail.shape[1], out_dtypes=(F32, BF16))
    gates8 = gate_s.reshape(db, NSA_KVH, LANES)[:, :, :per_group].reshape(db, NSA_H, 3)
    gates8 = jnp.pad(gates8, ((0, 0), (0, 0), (0, LANES - 3)))
    group_of_head = jnp.asarray(np.arange(NSA_H) // NSA_G)
    onehot_g = jax.nn.one_hot(group_of_head, NSA_KVH, dtype=BF16)
    qn = ps_bf[:, :NSA_Q_W].reshape(db, NSA_H, HEAD_DIM)
    q8 = (qn[:, :, None, :] * onehot_g[None, :, :, None]).reshape(db, NSA_H, NSA_KVH * HEAD_DIM)
    slope8 = jnp.broadcast_to(nsa_slopes[:, None], (NSA_H, LANES))
    pos_rg = jnp.tile(jnp.transpose(cmp_pos, (0, 2, 1)), (1, SUBLANES // NSA_KVH, PAGE_SIZE // CMP_BLOCK))
    pos_rows = jnp.repeat(pos_rg, NSA_KVH, axis=2)
    kcvc_s = compress_sample(cache_cmp, e, page_table, pos_rows)
    w_buf = cache_win.shape[2]
    new_cmp = ps[:, NSA_Q_W:NSA_Q_W + kvw]
    new_sel = ps[:, NSA_Q_W + kvw:NSA_Q_W + 2 * kvw]
    new_win = ps[:, NSA_Q_W + 2 * kvw:]
    win_all = jnp.concatenate([cache_win[e].reshape(db, w_buf, -1), new_win[:, None, :]], axis=1)
    n_win = w_buf + 1
    win_pad = jnp.pad(win_all, ((0, 0), (0, -n_win % LANES), (0, 0)))
    o_cmp, o_win, idx = nsa_decode_cmp(q8, slope8, kcvc_s, win_pad, past, n_win)
    ids = jnp.concatenate([idx[:, 0, :TOP_N - 1], idx[:, NSA_G, :TOP_N - 1]], axis=1)
    o_nsa_s = nsa_decode_sel(cache_sel, e, page_table, ids, q8, slope8, new_sel[:, None, :], o_cmp, o_win, gates8,
                             past)

    qd = pt_bf[:, :DIFF_Q_W].reshape(db, DIFF_H, 2, HEAD_DIM)
    eye_c = jnp.eye(2, dtype=BF16)
    q8d = jnp.einsum('bhcd,cj->bchjd', qd, eye_c).reshape(db, 2 * DIFF_H, DIFF_VD)
    slope8d = jnp.broadcast_to(jnp.tile(diff_slopes, 2)[:, None], (2 * DIFF_H, LANES))
    new_diff = pt[:, DIFF_Q_W:]
    o_diff_s = diff_decode(cache_diff, e, page_table, q8d, slope8d, new_diff.reshape(db, 2, DIFF_H, DIFF_VD),
                           lam_vec, dnorm, lam_init, past)
    mix_s = jnp.concatenate([o_nsa_s.reshape(db, NSA_Q_W), o_diff_s.reshape(db, DIFF_V_W)], axis=1)
    kv_s = (db, 1, 2, NSA_KVH, HEAD_DIM)
    keep = min(WINDOW, n_win)
    small = (win_p, new_cmp.reshape(kv_s), new_sel.reshape(kv_s),
             win_all[:, n_win - keep:].reshape(db, keep, 2, NSA_KVH, HEAD_DIM),
             new_diff.reshape(db, 1, 2, DIFF_H, DIFF_VD))
    return mix_p, mix_s, (cmp_st, sel_st, kvd_st), small


def _ret_layer(o, hp, hs, batch, seq, state, ret_w_in, rnorm):
    db = hs.shape[0]
    log_g = jnp.log1p(-jnp.exp2(-5.0 - jnp.arange(RET_H, dtype=F32)))
    n_in = ret_w_in.shape[2]
    proj = dense(hp, ret_w_in, n_in, lead=o)
    gated_p, st_p = ret_prompt(proj, rnorm, log_g, batch, seq)
    ps = dense(hs, ret_w_in, n_in, lead=o)
    hk = RET_H * RET_DK
    hv = RET_H * RET_DV
    q = ps[:, :hk].reshape(db, RET_H, RET_DK)
    k = ps[:, hk:2 * hk].reshape(db, RET_H, RET_DK)
    v = ps[:, 2 * hk:2 * hk + hv].reshape(db, RET_H, RET_DV)
    g = ps[:, 2 * hk + hv:].reshape(db, RET_H, RET_DV)
    gated_s, st_s = ret_decode(q, k, v, g, rnorm, state, log_g)
    return gated_p, gated_s.reshape(db, hv), st_p, st_s


def kernel(x_prompt, x_sample, cache_nsa_cmp, cache_nsa_sel, cache_nsa_win, cache_diff, state_ret, page_table,
           norm_mix, norm_ffn, norm_final, even_w_in, even_w_out, nsa_cmp_pos, diff_lambda, diff_norm,
           ret_w_in, ret_norm, ret_w_out, ffn_w13, ffn_w2):
    batch, seq, d = x_prompt.shape
    db = x_sample.shape[0]
    d_ff = ffn_w2.shape[1]
    xp = x_prompt.reshape(batch * seq, d)
    xs = x_sample.reshape(db, d)
    caches = (cache_nsa_cmp, cache_nsa_sel, cache_nsa_win, cache_diff)
    stacks = (None, None, None)
    small = [[] for _ in range(5)]
    ret_p, ret_s = [], []
    for layer in range(DEPTH):
        hp = rmsnorm(xp, norm_mix[layer], BF16)
        hs = rmsnorm(xs, norm_mix[layer], BF16)
        if layer % 2 == 0:
            e = layer // 2
            mix_p, mix_s, stacks, small_e = _even_layer(
                e, layer, hp, hs, batch, seq, caches, page_table, even_w_in, nsa_cmp_pos[e], diff_lambda[e],
                diff_norm[e], stacks)
            for lst, item in zip(small, small_e):
                lst.append(item)
            w_out, lead = even_w_out, e
        else:
            o = layer // 2
            mix_p, mix_s, st_p, st_s = _ret_layer(o, hp, hs, batch, seq, state_ret[o], ret_w_in, ret_norm[o])
            ret_p.append(st_p)
            ret_s.append(st_s)
            w_out, lead = ret_w_out, o
        xp = dense(mix_p, w_out, d, lead=lead, res=xp)
        xs = dense(mix_s, w_out, d, lead=lead, res=xs)
        up_p = dense(rmsnorm(xp, norm_ffn[layer], BF16), ffn_w13, d_ff, lead=layer, col3=d_ff, out_dtypes=(BF16,))
        xp = dense(up_p, ffn_w2, d, lead=layer, res=xp)
        up_s = dense(rmsnorm(xs, norm_ffn[layer], BF16), ffn_w13, d_ff, lead=layer, col3=d_ff, out_dtypes=(BF16,))
        xs = dense(up_s, ffn_w2, d, lead=layer, res=xs)
    y_prompt = rmsnorm(xp, norm_final, F32).reshape(batch, seq, d)
    y_sample = rmsnorm(xs, norm_final, F32).reshape(db, 1, d)
    n_even = even_w_in.shape[0]
    cmp_st, sel_st, kvd_st = stacks
    win_p, cmp_s, sel_s, win_s, diff_s = [jnp.stack(t) for t in small]
    return (y_prompt, y_sample,
            cmp_st.reshape(n_even, batch, seq, 2, NSA_KVH, HEAD_DIM),
            sel_st.reshape(n_even, batch, seq, 2, NSA_KVH, HEAD_DIM), win_p,
            kvd_st.reshape(n_even, batch, seq, 2, DIFF_H, DIFF_VD), jnp.stack(ret_p),
            cmp_s, sel_s, win_s, diff_s, jnp.stack(ret_s))
```

```python
import functools
import math

import jax
import jax.numpy as jnp
import numpy as np
from jax import lax
from jax.experimental import pallas as pl
from jax.experimental.pallas import tpu as pltpu

F32 = jnp.float32
BF16 = jnp.bfloat16

D_MODEL = 2048
DEPTH = 4
PAGE_SIZE = 128
HEAD_DIM = 128
NSA_H = 8
NSA_KVH = 2
NSA_G = NSA_H // NSA_KVH
CMP_BLOCK = 64
CMP_SHIFT = CMP_BLOCK.bit_length() - 1
TOP_N = 16
WINDOW = 512
FORCE_SCORE = 1.0e4
DIFF_H = 4
DIFF_VD = 2 * HEAD_DIM
RET_H = 8
RET_DK = D_MODEL // RET_H
RET_DV = 2 * D_MODEL // RET_H
RET_CHUNK = 128
D_FF = ((8 * D_MODEL + 3 * 256 - 1) // (3 * 256)) * 256
NORM_EPS = 1e-6
NEG_INF = -1e30
ATT_SCALE = HEAD_DIM ** -0.5
LOG2E = math.log2(math.e)

NSA_Q_W = NSA_H * HEAD_DIM
NSA_KV_W = NSA_KVH * HEAD_DIM
NSA_GATE_W = NSA_H * 3
DIFF_Q_W = DIFF_H * 2 * HEAD_DIM
DIFF_V_W = DIFF_H * DIFF_VD
MAIN_W = NSA_Q_W + 6 * NSA_KV_W + 2 * DIFF_Q_W + DIFF_V_W
GATE_OFF = NSA_Q_W + 6 * NSA_KV_W

NSA_TQ, NSA_TK = 256, 256
DIFF_TQ, DIFF_TK = 512, 512
RET_BLOCK = 256

LANES = 128
SUBLANES = 8
VMEM_LIMIT = 56 * 1024 * 1024


def _cparams(sem):
    return pltpu.CompilerParams(dimension_semantics=sem, vmem_limit_bytes=VMEM_LIMIT)


def _rmsnorm_kernel(x_ref, g_ref, o_ref):
    x = x_ref[...]
    y = x * lax.rsqrt(jnp.mean(x * x, axis=-1, keepdims=True) + NORM_EPS)
    o_ref[...] = (y * g_ref[...]).astype(o_ref.dtype)


def rmsnorm(x, g, out_dtype):
    m, d = x.shape
    tm = min(512, m)
    return pl.pallas_call(
        _rmsnorm_kernel,
        grid=(m // tm,),
        in_specs=[pl.BlockSpec((tm, d), lambda i: (i, 0)), pl.BlockSpec((1, d), lambda i: (0, 0))],
        out_specs=pl.BlockSpec((tm, d), lambda i: (i, 0)),
        out_shape=jax.ShapeDtypeStruct((m, d), out_dtype),
        compiler_params=_cparams(("parallel",)),
        name="rmsnorm",
    )(x, g.reshape(1, d))


def _dense_kernel(*refs, n_w, has_res, has_prev, has_s, n_out, cast_w, n_i):
    refs = list(refs)
    x_ref = refs.pop(0)
    xs_ref = refs.pop(0) if has_s else None
    w_refs = [refs.pop(0) for _ in range(n_w)]
    res_ref = refs.pop(0) if has_res else None
    res_s_ref = refs.pop(0) if (has_res and has_s) else None
    if has_prev:
        refs.pop(0)
    out_refs = [refs.pop(0) for _ in range(n_out)]
    out_s_refs = [refs.pop(0) for _ in range(n_out)] if has_s else []
    wb_refs = refs
    i = pl.program_id(1)
    if cast_w:
        @pl.when(i == 0)
        def _():
            for w_ref, wb_ref in zip(w_refs, wb_refs):
                wb_ref[...] = w_ref[...].astype(BF16)
        w_refs = wb_refs

    def run(src_ref, r_ref, dst_refs):
        x = src_ref[...]
        acc = jnp.dot(x, w_refs[0][...], preferred_element_type=F32)
        if n_w == 2:
            acc = acc * jax.nn.sigmoid(acc) * jnp.dot(x, w_refs[1][...], preferred_element_type=F32)
        if r_ref is not None:
            acc = r_ref[...] + acc
        for o_ref in dst_refs:
            o_ref[...] = acc.astype(o_ref.dtype)

    if has_s:
        pl.when(i < n_i)(lambda: run(x_ref, res_ref, out_refs))
        pl.when(i == n_i)(lambda: run(xs_ref, res_s_ref, out_s_refs))
    else:
        run(x_ref, res_ref, out_refs)


def _dense_tiles(m, k, n, n_w):
    tn = min(1024 if (k * n_w <= 2048 and n % 1024 == 0) else 512, n)
    tm = min(1024 if k * n_w <= 4096 else 512, m)
    return tm, tn


def dense(x, w, n, col0=0, lead=None, col3=None, res=None, out_dtypes=(F32,), stack=None, xs=None, res_s=None):
    m, k = x.shape
    n_w = 1 if col3 is None else 2
    tm, tn = _dense_tiles(m, k, n, n_w)
    assert m % tm == 0 and n % tn == 0 and col0 % tn == 0 and (col3 is None or col3 % tn == 0)
    cast_w = w.dtype != BF16
    has_s = xs is not None
    n_i = m // tm
    last = n_i - 1

    def rows(i):
        return jnp.minimum(i, last) if has_s else i

    def w_spec(c0):
        if w.ndim == 3:
            return pl.BlockSpec((None, k, tn), lambda j, i: (lead, 0, c0 // tn + j))
        return pl.BlockSpec((k, tn), lambda j, i: (0, c0 // tn + j))

    in_specs = [pl.BlockSpec((tm, k), lambda j, i: (rows(i), 0))]
    args = [x]
    if has_s:
        ms = xs.shape[0]
        in_specs.append(pl.BlockSpec((ms, k), lambda j, i: (0, 0)))
        args.append(xs)
    in_specs.append(w_spec(col0))
    args.append(w)
    if n_w == 2:
        in_specs.append(w_spec(col3))
        args.append(w)
    if res is not None:
        in_specs.append(pl.BlockSpec((tm, tn), lambda j, i: (rows(i), j)))
        args.append(res)
        if has_s:
            in_specs.append(pl.BlockSpec((ms, tn), lambda j, i: (0, j)))
            args.append(res_s)
    out_specs = [pl.BlockSpec((tm, tn), lambda j, i: (rows(i), j)) for _ in out_dtypes]
    out_shape = [jax.ShapeDtypeStruct((m, n), dt) for dt in out_dtypes]
    aliases = {}
    has_prev = False
    if stack is not None:
        depth, slot, prev = stack
        out_specs[0] = pl.BlockSpec((None, tm, tn), lambda j, i: (slot, rows(i), j))
        out_shape[0] = jax.ShapeDtypeStruct((depth, m, n), out_dtypes[0])
        if prev is not None:
            has_prev = True
            aliases = {len(args): 0}
            in_specs.append(pl.BlockSpec(memory_space=pl.ANY))
            args.append(prev)
    if has_s:
        out_specs += [pl.BlockSpec((ms, tn), lambda j, i: (0, j)) for _ in out_dtypes]
        out_shape += [jax.ShapeDtypeStruct((ms, n), dt) for dt in out_dtypes]
    outs = pl.pallas_call(
        functools.partial(_dense_kernel, n_w=n_w, has_res=res is not None, has_prev=has_prev, has_s=has_s,
                          n_out=len(out_dtypes), cast_w=cast_w, n_i=n_i),
        grid=(n // tn, n_i + (1 if has_s else 0)),
        in_specs=in_specs,
        out_specs=out_specs,
        out_shape=out_shape,
        scratch_shapes=[pltpu.VMEM((k, tn), BF16)] * (n_w if cast_w else 0),
        input_output_aliases=aliases,
        compiler_params=_cparams(("parallel", "arbitrary")),
        name="dense",
    )(*args)
    n_out = len(out_dtypes)
    unwrap = lambda t: t[0] if n_out == 1 else tuple(t)
    if has_s:
        return unwrap(outs[:n_out]), unwrap(outs[n_out:])
    return unwrap(outs)


def _softmax_rows64(logits):
    e = jnp.exp(logits - jnp.max(logits, axis=-1, keepdims=True))
    reps = logits.shape[-1] // CMP_BLOCK
    return e * (reps / jnp.sum(e, axis=-1, keepdims=True))


def _compress_prompt_kernel(x_ref, pos_ref, o_ref, *, seq):
    w = _softmax_rows64(pos_ref[...])
    blk = lax.broadcasted_iota(jnp.int32, (LANES, seq), 0)
    key_blk = lax.broadcasted_iota(jnp.int32, (LANES, seq), 1) >> CMP_SHIFT
    onblk = blk == key_blk
    for c in range(4):
        wb = jnp.where(onblk, w[c:c + 1, :], 0.0).astype(BF16)
        o_ref[0, :, c * LANES:(c + 1) * LANES] = jnp.dot(
            wb, x_ref[:, c * LANES:(c + 1) * LANES], preferred_element_type=F32).astype(BF16)


def compress_prompt(cmp_bf, pos_tiled, batch, seq):
    return pl.pallas_call(
        functools.partial(_compress_prompt_kernel, seq=seq),
        grid=(batch,),
        in_specs=[pl.BlockSpec((seq, 4 * LANES), lambda b: (b, 0)),
                  pl.BlockSpec((4, seq), lambda b: (0, 0))],
        out_specs=pl.BlockSpec((1, LANES, 4 * LANES), lambda b: (b, 0, 0)),
        out_shape=jax.ShapeDtypeStruct((batch, LANES, 4 * LANES), BF16),
        compiler_params=_cparams(("parallel",)),
        name="nsa_compress_prompt",
    )(cmp_bf, pos_tiled)


def _lane_tile(x, width):
    return x if width == LANES else jnp.concatenate([x] * (width // LANES), axis=-1)


def _flash_step(s, v, m_ref, l_ref, acc_ref, idx):
    ss, vs = (s, v) if isinstance(s, (list, tuple)) else ([s], [v])
    tiles = [[t[:, j * LANES:(j + 1) * LANES] for j in range(t.shape[1] // LANES)] for t in ss]
    blocks = [blk for tile in tiles for blk in tile]
    mx = blocks[0]
    for blk in blocks[1:]:
        mx = jnp.maximum(mx, blk)
    m_prev = m_ref[idx]
    m_new = jnp.maximum(m_prev, jnp.max(mx, axis=-1, keepdims=True))
    alpha = jnp.exp2(m_prev - m_new)
    ps = [[jnp.exp2(blk - m_new) for blk in tile] for tile in tiles]
    row_sum = None
    for tile in ps:
        for p in tile:
            row_sum = p if row_sum is None else row_sum + p
    l_ref[idx] = alpha * l_ref[idx] + jnp.sum(row_sum, axis=-1, keepdims=True)
    m_ref[idx] = m_new
    acc = _lane_tile(alpha, vs[0].shape[-1]) * acc_ref[idx]
    for tile, vt in zip(ps, vs):
        acc = acc + jnp.dot(jnp.concatenate([p.astype(BF16) for p in tile], axis=-1), vt,
                            preferred_element_type=F32)
    acc_ref[idx] = acc


def _flash_init(m_ref, l_ref, acc_ref):
    m_ref[...] = jnp.full(m_ref.shape, NEG_INF, F32)
    l_ref[...] = jnp.zeros(l_ref.shape, F32)
    acc_ref[...] = jnp.zeros(acc_ref.shape, F32)


def _flash_result(l_ref, acc_ref, idx):
    return acc_ref[idx] / _lane_tile(l_ref[idx], acc_ref.shape[-1])


def _softmax_pv(s, v):
    p = jnp.exp2(s - jnp.max(s, axis=-1, keepdims=True))
    l = jnp.sum(p, axis=-1, keepdims=True)
    return jnp.dot(p.astype(BF16), v, preferred_element_type=F32) / l


def _nsa_prompt_kernel(slopes_ref, q_ref, gate_ref, kc_ref, vc_ref, ks_ref, vs_ref, kw_ref, vw_ref,
                       o_ref, m_ref, l_ref, acc_ref, flag_ref, *, tq, tk, n_blocks):
    g = pl.program_id(1)
    qi = pl.program_id(2)
    q0 = qi * tq
    row = lax.broadcasted_iota(jnp.int32, (tq, LANES), 0)
    col = lax.broadcasted_iota(jnp.int32, (tq, LANES), 1)
    qpos = q0 + row
    nt = (((1,), (1,)), ((), ()))

    kc = kc_ref[0]
    vc = vc_ref[0]
    cdist = qpos - (col * CMP_BLOCK + (CMP_BLOCK - 1))
    cmask = cdist >= 0
    cdist_f = cdist.astype(F32)
    imp = jnp.zeros((tq, LANES), F32)
    o_cmp = []
    for h in range(NSA_G):
        slope = slopes_ref[g * NSA_G + h]
        qh = q_ref[:, h * HEAD_DIM:(h + 1) * HEAD_DIM]
        s = lax.dot_general(qh, kc, nt, preferred_element_type=F32) * ATT_SCALE - slope * cdist_f
        s = jnp.where(cmask, s, NEG_INF)
        p = jnp.exp(s - jnp.max(s, axis=-1, keepdims=True))
        p = jnp.where(cmask, p / jnp.sum(p, axis=-1, keepdims=True), 0.0)
        imp = imp + p
        o_cmp.append(jnp.dot(p.astype(BF16), vc, preferred_element_type=F32))

    imp = jnp.where(cmask, imp, -1.0)
    forced = (col == (qpos >> CMP_SHIFT)) | (col == 0)
    imp = jnp.where(forced, FORCE_SCORE, imp)
    nblk = -(-n_blocks // SUBLANES) * SUBLANES
    imp_t = imp.T[:nblk]
    sub = lax.broadcasted_iota(jnp.int32, (SUBLANES, tq), 0)
    sel_rows = []
    for vi in range(nblk // SUBLANES):
        x = imp_t[vi * SUBLANES:(vi + 1) * SUBLANES]
        rank = jnp.zeros((SUBLANES, tq), F32)
        for j in range(nblk):
            r = imp_t[j:j + 1, :]
            jv, jr = divmod(j, SUBLANES)
            ge = jnp.where(r >= x, 1.0, 0.0)
            gt = jnp.where(r > x, 1.0, 0.0)
            if jv < vi:
                rank = rank + ge
            elif jv > vi:
                rank = rank + gt
            else:
                rank = rank + jnp.where(sub > jr, ge, gt)
        sel_rows.append(jnp.where(rank < TOP_N, 1.0, 0.0))
    sel_rows.append(jnp.zeros((LANES - nblk, tq), F32))
    sel_f = jnp.concatenate(sel_rows, axis=0).T
    sel = sel_f.astype(BF16)
    picked_any = jnp.max(sel_f, axis=0, keepdims=True)
    tile_of_block = lax.broadcasted_iota(jnp.int32, (1, LANES), 1) >> ((tk // CMP_BLOCK).bit_length() - 1)
    for kt in range(n_blocks * CMP_BLOCK // tk):
        hit = jnp.max(jnp.where(tile_of_block == kt, picked_any, 0.0))
        flag_ref[kt] = (hit > 0.5).astype(jnp.int32)

    c1 = ATT_SCALE * LOG2E
    slope2 = [slopes_ref[g * NSA_G + h] * LOG2E for h in range(NSA_G)]

    _flash_init(m_ref, l_ref, acc_ref)
    e_row = lax.broadcasted_iota(jnp.int32, (LANES, tk), 0)
    e_col = lax.broadcasted_iota(jnp.int32, (LANES, tk), 1)
    rel = lax.broadcasted_iota(jnp.int32, (1, tk), 1)
    krow = lax.broadcasted_iota(jnp.int32, (tq, tk), 0)
    kcol = lax.broadcasted_iota(jnp.int32, (tq, tk), 1)

    def sel_tile(kt, diagonal):
        k0 = pl.multiple_of(kt * tk, tk)
        k = ks_ref[pl.ds(k0, tk), :]
        v = vs_ref[pl.ds(k0, tk), :]
        expand = jnp.where(((k0 + e_col) >> CMP_SHIFT) == e_row, 1.0, 0.0).astype(BF16)
        picked = jnp.dot(sel, expand, preferred_element_type=F32)
        if diagonal:
            picked = jnp.where(k0 + kcol <= q0 + krow, picked, 0.0)
        valid = picked > 0.5
        bias = (k0 - q0 + rel).astype(F32)
        for h in range(NSA_G):
            qh = q_ref[:, h * HEAD_DIM:(h + 1) * HEAD_DIM]
            s = lax.dot_general(qh, k, nt, preferred_element_type=F32) * c1 + slope2[h] * bias
            _flash_step(jnp.where(valid, s, NEG_INF), v, m_ref, l_ref, acc_ref, h)

    kd = q0 // tk
    sel_tile(kd, True)

    def sel_body(i, carry):
        kt = kd - 1 - i
        pl.when(flag_ref[kt] > 0)(lambda: sel_tile(kt, False))
        return carry

    lax.fori_loop(0, kd, sel_body, 0)
    o_sel = [_flash_result(l_ref, acc_ref, h) for h in range(NSA_G)]

    n_band = WINDOW + tq
    w0 = pl.multiple_of(jnp.maximum(q0 - WINDOW, 0), LANES)
    kw = kw_ref[pl.ds(w0, n_band), :]
    vw = vw_ref[pl.ds(w0, n_band), :]
    wdist = (q0 + lax.broadcasted_iota(jnp.int32, (tq, n_band), 0)) - (
        w0 + lax.broadcasted_iota(jnp.int32, (tq, n_band), 1))
    wvalid = (wdist >= 0) & (wdist <= WINDOW)
    wbias = (w0 - q0 + lax.broadcasted_iota(jnp.int32, (1, n_band), 1)).astype(F32)
    o_win = []
    for h in range(NSA_G):
        qh = q_ref[:, h * HEAD_DIM:(h + 1) * HEAD_DIM]
        s = lax.dot_general(qh, kw, nt, preferred_element_type=F32) * c1 + slope2[h] * wbias
        o_win.append(_softmax_pv(jnp.where(wvalid, s, NEG_INF), vw))

    gates = jax.nn.sigmoid(gate_ref[...])
    for h in range(NSA_G):
        o = (gates[:, 3 * h:3 * h + 1] * o_cmp[h] + gates[:, 3 * h + 1:3 * h + 2] * o_sel[h]
             + gates[:, 3 * h + 2:3 * h + 3] * o_win[h])
        o_ref[:, h * HEAD_DIM:(h + 1) * HEAD_DIM] = o.astype(o_ref.dtype)


def nsa_prompt(qn_bf, gate_pre, kcvc, sel_bf, win_bf, slopes, batch, seq, tq=NSA_TQ, tk=NSA_TK):
    assert tk % tq == 0 and seq % tk == 0 and seq >= WINDOW + tq and seq // CMP_BLOCK <= LANES
    nq = seq // tq
    gw = NSA_G * HEAD_DIM

    def kv_spec(kv):
        return pl.BlockSpec((seq, HEAD_DIM), lambda b, g, i: (b, NSA_KVH * kv + g))

    return pl.pallas_call(
        functools.partial(_nsa_prompt_kernel, tq=tq, tk=tk, n_blocks=seq // CMP_BLOCK),
        grid=(batch, NSA_KVH, nq),
        in_specs=[pl.BlockSpec(memory_space=pltpu.SMEM),
                  pl.BlockSpec((tq, gw), lambda b, g, i: (b * nq + i, g)),
                  pl.BlockSpec((tq, LANES), lambda b, g, i: (b * nq + i, g)),
                  pl.BlockSpec((1, LANES, HEAD_DIM), lambda b, g, i: (b, 0, g)),
                  pl.BlockSpec((1, LANES, HEAD_DIM), lambda b, g, i: (b, 0, 2 + g)),
                  kv_spec(0), kv_spec(1), kv_spec(0), kv_spec(1)],
        out_specs=pl.BlockSpec((tq, gw), lambda b, g, i: (b * nq + i, g)),
        out_shape=jax.ShapeDtypeStruct((batch * seq, NSA_Q_W), BF16),
        scratch_shapes=[pltpu.VMEM((NSA_G, tq, LANES), F32), pltpu.VMEM((NSA_G, tq, LANES), F32),
                        pltpu.VMEM((NSA_G, tq, HEAD_DIM), F32), pltpu.SMEM((seq // tk,), jnp.int32)],
        compiler_params=_cparams(("parallel", "parallel", "arbitrary")),
        name="nsa_prompt",
    )(slopes, qn_bf, gate_pre, kcvc, kcvc, sel_bf, sel_bf, win_bf, win_bf)


def _diff_lambda(lv, lam_init):
    a = jnp.sum(lv[0:1] * lv[1:2], axis=-1, keepdims=True)
    b = jnp.sum(lv[2:3] * lv[3:4], axis=-1, keepdims=True)
    return jnp.exp(a) - jnp.exp(b) + lam_init


def _diff_prompt_kernel(slopes_ref, q_ref, k_ref, v_ref, lam_ref, dn_ref, o_ref, m_ref, l_ref, acc_ref,
                        *, tq, tk, lam_init):
    h = pl.program_id(1)
    qi = pl.program_id(2)
    q0 = qi * tq
    krow = lax.broadcasted_iota(jnp.int32, (tq, tk), 0)
    kcol = lax.broadcasted_iota(jnp.int32, (tq, tk), 1)
    rel = lax.broadcasted_iota(jnp.int32, (1, tk), 1)
    c1 = ATT_SCALE * LOG2E
    slope2 = slopes_ref[h] * LOG2E
    nt = (((1,), (1,)), ((), ()))
    _flash_init(m_ref, l_ref, acc_ref)

    def tile(kt, diagonal):
        k0 = pl.multiple_of(kt * tk, tk)
        k = k_ref[pl.ds(k0, tk), :]
        v = v_ref[pl.ds(k0, tk), :]
        bias = slope2 * (k0 - q0 + rel).astype(F32)
        for c in range(2):
            s = lax.dot_general(q_ref[:, c * HEAD_DIM:(c + 1) * HEAD_DIM], k[:, c * HEAD_DIM:(c + 1) * HEAD_DIM],
                                nt, preferred_element_type=F32) * c1 + bias
            if diagonal:
                s = jnp.where(k0 + kcol <= q0 + krow, s, NEG_INF)
            _flash_step(s, v, m_ref, l_ref, acc_ref, c)

    kd = q0 // tk
    tile(kd, True)

    def body(i, carry):
        tile(kd - 1 - i, False)
        return carry

    lax.fori_loop(0, kd, body, 0)
    lam = _diff_lambda(lam_ref[...], lam_init)
    o = _flash_result(l_ref, acc_ref, 0) - lam * _flash_result(l_ref, acc_ref, 1)
    y = o * lax.rsqrt(jnp.mean(o * o, axis=-1, keepdims=True) + NORM_EPS)
    o_ref[...] = (y * dn_ref[...] * (1.0 - lam_init)).astype(o_ref.dtype)


def diff_prompt(qd_bf, kvd_bf, lam_vec, dnorm, slopes, lam_init, batch, seq, tq=DIFF_TQ, tk=DIFF_TK):
    assert tk % tq == 0 and seq % tk == 0
    nq = seq // tq
    w = 2 * HEAD_DIM
    qb, kb, vb = 0, 0, DIFF_H
    return pl.pallas_call(
        functools.partial(_diff_prompt_kernel, tq=tq, tk=tk, lam_init=lam_init),
        grid=(batch, DIFF_H, nq),
        in_specs=[pl.BlockSpec(memory_space=pltpu.SMEM),
                  pl.BlockSpec((tq, w), lambda b, h, i: (b * nq + i, qb + h)),
                  pl.BlockSpec((seq, w), lambda b, h, i: (b, kb + h)),
                  pl.BlockSpec((seq, w), lambda b, h, i: (b, vb + h)),
                  pl.BlockSpec((4, HEAD_DIM), lambda b, h, i: (0, 0)),
                  pl.BlockSpec((1, DIFF_VD), lambda b, h, i: (0, 0))],
        out_specs=pl.BlockSpec((tq, w), lambda b, h, i: (b * nq + i, h)),
        out_shape=jax.ShapeDtypeStruct((batch * seq, DIFF_V_W), BF16),
        scratch_shapes=[pltpu.VMEM((2, tq, LANES), F32), pltpu.VMEM((2, tq, LANES), F32),
                        pltpu.VMEM((2, tq, DIFF_VD), F32)],
        compiler_params=_cparams(("parallel", "parallel", "arbitrary")),
        name="diff_prompt",
    )(slopes, qd_bf, kvd_bf, kvd_bf, lam_vec, dnorm.reshape(1, DIFF_VD))


def _ret_prompt_kernel(lg_ref, q_ref, k_ref, v_ref, g_ref, rn_ref, o_ref, st_ref, *, chunk):
    h = pl.program_id(1)
    c = pl.program_id(2)
    lg = lg_ref[h]

    @pl.when(c == 0)
    def _():
        st_ref[...] = jnp.zeros(st_ref.shape, F32)

    ii = lax.broadcasted_iota(jnp.int32, (chunk, chunk), 0)
    jj = lax.broadcasted_iota(jnp.int32, (chunk, chunk), 1)
    d = (ii - jj).astype(F32)
    decay = jnp.where(d >= 0, jnp.exp(lg * jnp.maximum(d, 0.0)), 0.0)
    ik = lax.broadcasted_iota(jnp.int32, (chunk, RET_DK), 0).astype(F32)
    q_dec = jnp.exp(lg * (ik + 1.0))
    k_dec = jnp.exp(lg * (chunk - 1.0 - ik))
    ones = jnp.ones((1, 1), F32)
    g_chunk = jnp.exp(ones * (lg * chunk))

    q = q_ref[...]
    k = k_ref[...] * (RET_DK ** -0.5)
    v = v_ref[...].astype(BF16)
    state = st_ref[0, 0]
    s = lax.dot_general(q.astype(BF16), k.astype(BF16), (((1,), (1,)), ((), ())),
                        preferred_element_type=F32) * decay
    o = jnp.dot(s.astype(BF16), v, preferred_element_type=F32)
    o = o + jnp.dot((q * q_dec).astype(BF16), state.astype(BF16), preferred_element_type=F32)
    kv = lax.dot_general((k * k_dec).astype(BF16), v, (((0,), (0,)), ((), ())), preferred_element_type=F32)
    st_ref[0, 0] = state * g_chunk + kv

    y = o * lax.rsqrt(jnp.mean(o * o, axis=-1, keepdims=True) + NORM_EPS) * rn_ref[0]
    gate = g_ref[...]
    o_ref[...] = (gate * jax.nn.sigmoid(gate) * y).astype(o_ref.dtype)


def ret_prompt(proj, rnorm, log_g, batch, seq, chunk=RET_BLOCK):
    nc = seq // chunk
    kb = RET_H
    vb = 2 * RET_H * RET_DK // RET_DV
    gb = vb + RET_H
    return pl.pallas_call(
        functools.partial(_ret_prompt_kernel, chunk=chunk),
        grid=(batch, RET_H, nc),
        in_specs=[pl.BlockSpec(memory_space=pltpu.SMEM),
                  pl.BlockSpec((chunk, RET_DK), lambda b, h, c: (b * nc + c, h)),
                  pl.BlockSpec((chunk, RET_DK), lambda b, h, c: (b * nc + c, kb + h)),
                  pl.BlockSpec((chunk, RET_DV), lambda b, h, c: (b * nc + c, vb + h)),
                  pl.BlockSpec((chunk, RET_DV), lambda b, h, c: (b * nc + c, gb + h)),
                  pl.BlockSpec((1, 1, RET_DV), lambda b, h, c: (h, 0, 0))],
        out_specs=[pl.BlockSpec((chunk, RET_DV), lambda b, h, c: (b * nc + c, h)),
                   pl.BlockSpec((1, 1, RET_DK, RET_DV), lambda b, h, c: (b, h, 0, 0))],
        out_shape=[jax.ShapeDtypeStruct((batch * seq, RET_H * RET_DV), BF16),
                   jax.ShapeDtypeStruct((batch, RET_H, RET_DK, RET_DV), F32)],
        compiler_params=_cparams(("parallel", "parallel", "arbitrary")),
        name="ret_prompt",
    )(log_g, proj, proj, proj, proj, rnorm.reshape(RET_H, 1, RET_DV))


NT_DIMS = (((1,), (1,)), ((), ()))


def _compress_sample_kernel(pt_ref, pos_ref, *refs, pp):
    del pt_ref
    o_ref = refs[pp]
    per_page = PAGE_SIZE // CMP_BLOCK
    n_rows = PAGE_SIZE * NSA_KVH
    row = lax.broadcasted_iota(jnp.int32, (SUBLANES, n_rows), 0)
    col = lax.broadcasted_iota(jnp.int32, (SUBLANES, n_rows), 1)
    g_shift = NSA_KVH.bit_length() - 1
    member = ((col & (NSA_KVH - 1)) == (row & (NSA_KVH - 1))) & ((col >> (g_shift + CMP_SHIFT)) == (row >> g_shift))
    weights = []
    for kv in range(2):
        logits = jnp.where(member, pos_ref[kv], NEG_INF)
        e = jnp.exp(logits - jnp.max(logits, axis=-1, keepdims=True))
        weights.append((e / jnp.sum(e, axis=-1, keepdims=True)).astype(BF16))
    for i in range(pp):
        for kv in range(2):
            x = refs[i][0, 0, :, kv].reshape(n_rows, HEAD_DIM).astype(BF16)
            res = jnp.dot(weights[kv], x, preferred_element_type=F32)
            for half in range(per_page):
                for g in range(NSA_KVH):
                    c0 = (kv * NSA_KVH + g) * HEAD_DIM
                    o_ref[0, i * per_page + half:i * per_page + half + 1, c0:c0 + HEAD_DIM] = (
                        res[half * NSA_KVH + g:half * NSA_KVH + g + 1])


def compress_sample(cache, layer, page_table, pos_rows, pp=8):
    db, n_pages = page_table.shape
    per_page = PAGE_SIZE // CMP_BLOCK
    width = 2 * NSA_KVH * HEAD_DIM
    assert per_page * NSA_KVH <= SUBLANES

    def page_spec(i):
        return pl.BlockSpec((1, 1) + cache.shape[2:], lambda b, j, pt: (layer, pt[b, j * pp + i], 0, 0, 0, 0))

    return pl.pallas_call(
        functools.partial(_compress_sample_kernel, pp=pp),
        grid_spec=pltpu.PrefetchScalarGridSpec(
            num_scalar_prefetch=1,
            grid=(db, n_pages // pp),
            in_specs=[pl.BlockSpec(pos_rows.shape, lambda b, j, pt: (0, 0, 0))] + [page_spec(i) for i in range(pp)],
            out_specs=pl.BlockSpec((1, pp * per_page, width), lambda b, j, pt: (b, j, 0))),
        out_shape=jax.ShapeDtypeStruct((db, n_pages * per_page, width), F32),
        compiler_params=_cparams(("parallel", "arbitrary")),
        name="nsa_compress_sample",
    )(page_table, pos_rows, *([cache] * pp))


def _pick_group(x):
    row = lax.broadcasted_iota(jnp.int32, (NSA_H, HEAD_DIM), 0)
    return jnp.where(row < NSA_G, x[:, :HEAD_DIM], x[:, HEAD_DIM:])


def _nsa_decode_cmp_kernel(q_ref, slope_ref, kcvc_ref, win_ref, ocmp_ref, owin_ref, idx_ref, *, past, n_win):
    q8 = q_ref[0]
    kcvc = kcvc_ref[0]
    nb = kcvc.shape[0]
    kvw = NSA_KVH * HEAD_DIM
    lane = lax.broadcasted_iota(jnp.int32, (NSA_H, nb), 1)
    row = lax.broadcasted_iota(jnp.int32, (NSA_H, nb), 0)
    slope = slope_ref[...][:, :1]
    cdist = (past - (lane * CMP_BLOCK + CMP_BLOCK - 1)).astype(F32)
    s = lax.dot_general(q8, kcvc[:, :kvw].astype(BF16), NT_DIMS, preferred_element_type=F32) * ATT_SCALE
    s = s - slope * cdist
    p = jnp.exp(s - jnp.max(s, axis=-1, keepdims=True))
    p = p / jnp.sum(p, axis=-1, keepdims=True)
    ocmp_ref[0] = _pick_group(jnp.dot(p.astype(BF16), kcvc[:, kvw:].astype(BF16), preferred_element_type=F32))

    g0 = p[0:1] + p[1:2] + p[2:3] + p[3:4]
    g1 = p[4:5] + p[5:6] + p[6:7] + p[7:8]
    x = jnp.where(row < NSA_G, g0, g1)
    x = jnp.where(lane == 0, FORCE_SCORE, x)
    out_lane = lax.broadcasted_iota(jnp.int32, (NSA_H, LANES), 1)
    lane_f = lane.astype(F32)
    picked = jnp.zeros((NSA_H, LANES), F32)
    for t in range(TOP_N - 1):
        mx = jnp.max(x, axis=-1, keepdims=True)
        idx = jnp.min(jnp.where(x == mx, lane_f, float(nb)), axis=-1, keepdims=True)
        picked = jnp.where(out_lane == t, idx, picked)
        x = jnp.where(lane_f == idx, -2.0, x)
    idx_ref[0] = picked.astype(jnp.int32)

    win = win_ref[0]
    nw = win.shape[0]
    wl = lax.broadcasted_iota(jnp.int32, (NSA_H, nw), 1)
    wdist = (n_win - 1 - wl)
    valid = wdist >= 0
    s = lax.dot_general(q8, win[:, :kvw].astype(BF16), NT_DIMS, preferred_element_type=F32) * ATT_SCALE
    s = jnp.where(valid, s - slope * wdist.astype(F32), NEG_INF)
    p = jnp.exp(s - jnp.max(s, axis=-1, keepdims=True))
    p = p / jnp.sum(p, axis=-1, keepdims=True)
    owin_ref[0] = _pick_group(jnp.dot(p.astype(BF16), win[:, kvw:].astype(BF16), preferred_element_type=F32))


def nsa_decode_cmp(q8, slope8, kcvc, win_all, past, n_win):
    db = q8.shape[0]
    nb = kcvc.shape[1]
    nw = win_all.shape[1]
    w = kcvc.shape[2]
    head_out = jax.ShapeDtypeStruct((db, NSA_H, HEAD_DIM), F32)
    head_spec = pl.BlockSpec((1, NSA_H, HEAD_DIM), lambda b: (b, 0, 0))
    return pl.pallas_call(
        functools.partial(_nsa_decode_cmp_kernel, past=past, n_win=n_win),
        grid=(db,),
        in_specs=[pl.BlockSpec((1, NSA_H, NSA_KVH * HEAD_DIM), lambda b: (b, 0, 0)),
                  pl.BlockSpec((NSA_H, LANES), lambda b: (0, 0)),
                  pl.BlockSpec((1, nb, w), lambda b: (b, 0, 0)),
                  pl.BlockSpec((1, nw, w), lambda b: (b, 0, 0))],
        out_specs=[head_spec, head_spec, pl.BlockSpec((1, NSA_H, LANES), lambda b: (b, 0, 0))],
        out_shape=[head_out, head_out, jax.ShapeDtypeStruct((db, NSA_H, LANES), jnp.int32)],
        compiler_params=_cparams(("parallel",)),
        name="nsa_decode_cmp",
    )(q8, slope8, kcvc, win_all)


def _nsa_decode_sel_kernel(pt_ref, ids_ref, q_ref, slope_ref, new_ref, pa_ref, pb_ref, ocmp_ref, owin_ref,
                           gate_ref, o_ref, m_ref, l_ref, acc_ref, *, past, n_sel):
    del pt_ref
    b = pl.program_id(0)
    t = pl.program_id(1)
    q8 = q_ref[0]
    row = lax.broadcasted_iota(jnp.int32, (NSA_H, LANES), 0)
    lane = lax.broadcasted_iota(jnp.int32, (NSA_H, LANES), 1)
    slope = slope_ref[...][:, :1]
    kvw = NSA_KVH * HEAD_DIM

    @pl.when(t == 0)
    def _():
        new = new_ref[0].astype(BF16).astype(F32)
        m_ref[...] = jnp.sum(q8.astype(F32) * new[:, :kvw], axis=-1, keepdims=True) * ATT_SCALE + jnp.zeros(
            (NSA_H, LANES), F32)
        l_ref[...] = jnp.ones((NSA_H, LANES), F32)
        acc_ref[...] = jnp.where(row < NSA_G, new[:, kvw:kvw + HEAD_DIM], new[:, kvw + HEAD_DIM:])

    na = ids_ref[b, t]
    nbk = ids_ref[b, n_sel + t]
    blk = jnp.where(row < NSA_G, na, nbk)
    per_page = PAGE_SIZE // CMP_BLOCK
    valid = (lane >> CMP_SHIFT) == (blk & (per_page - 1))
    dist = past - ((blk >> (per_page.bit_length() - 1)) * PAGE_SIZE + lane)
    sa = lax.dot_general(q8[:, :HEAD_DIM], pa_ref[0, 0, :, 0, 0, :].astype(BF16), NT_DIMS,
                         preferred_element_type=F32)
    sb = lax.dot_general(q8[:, HEAD_DIM:], pb_ref[0, 0, :, 0, 1, :].astype(BF16), NT_DIMS,
                         preferred_element_type=F32)
    s = jnp.where(row < NSA_G, sa, sb) * ATT_SCALE - slope * dist.astype(F32)
    s = jnp.where(valid, s, NEG_INF)
    m_prev = m_ref[...]
    m_new = jnp.maximum(m_prev, jnp.max(s, axis=-1, keepdims=True))
    alpha = jnp.exp(m_prev - m_new)
    p = jnp.exp(s - m_new)
    l_ref[...] = alpha * l_ref[...] + jnp.sum(p, axis=-1, keepdims=True)
    m_ref[...] = m_new
    pb16 = p.astype(BF16)
    oa = jnp.dot(pb16, pa_ref[0, 0, :, 1, 0, :].astype(BF16), preferred_element_type=F32)
    ob = jnp.dot(pb16, pb_ref[0, 0, :, 1, 1, :].astype(BF16), preferred_element_type=F32)
    acc_ref[...] = alpha * acc_ref[...] + jnp.where(row < NSA_G, oa, ob)

    @pl.when(t == n_sel - 1)
    def _():
        gates = jax.nn.sigmoid(gate_ref[0])
        o = (gates[:, 0:1] * ocmp_ref[0] + gates[:, 1:2] * (acc_ref[...] / l_ref[...])
             + gates[:, 2:3] * owin_ref[0])
        o_ref[0] = o.astype(o_ref.dtype)


def nsa_decode_sel(cache, layer, page_table, ids, q8, slope8, new_row, o_cmp, o_win, gates, past):
    db = q8.shape[0]
    n_sel = ids.shape[1] // NSA_KVH
    width = 2 * NSA_KVH * HEAD_DIM
    per_page = PAGE_SIZE // CMP_BLOCK
    head_spec = pl.BlockSpec((1, NSA_H, HEAD_DIM), lambda b, t, pt, ids: (b, 0, 0))

    def page_spec(g):
        return pl.BlockSpec((1, 1) + cache.shape[2:],
                            lambda b, t, pt, ids: (layer, pt[b, ids[b, g * n_sel + t] // per_page], 0, 0, 0, 0))

    return pl.pallas_call(
        functools.partial(_nsa_decode_sel_kernel, past=past, n_sel=n_sel),
        grid_spec=pltpu.PrefetchScalarGridSpec(
            num_scalar_prefetch=2,
            grid=(db, n_sel),
            in_specs=[pl.BlockSpec((1, NSA_H, NSA_KVH * HEAD_DIM), lambda b, t, pt, ids: (b, 0, 0)),
                      pl.BlockSpec((NSA_H, LANES), lambda b, t, pt, ids: (0, 0)),
                      pl.BlockSpec((1, 1, width), lambda b, t, pt, ids: (b, 0, 0)),
                      page_spec(0), page_spec(1), head_spec, head_spec, head_spec],
            out_specs=head_spec,
            scratch_shapes=[pltpu.VMEM((NSA_H, LANES), F32), pltpu.VMEM((NSA_H, LANES), F32),
                            pltpu.VMEM((NSA_H, HEAD_DIM), F32)]),
        out_shape=jax.ShapeDtypeStruct((db, NSA_H, HEAD_DIM), BF16),
        compiler_params=_cparams(("parallel", "arbitrary")),
        name="nsa_decode_sel",
    )(page_table, ids, q8, slope8, new_row, cache, cache, o_cmp, o_win, gates)


def _diff_decode_kernel(pt_ref, q_ref, slope_ref, new_ref, lam_ref, dn_ref, *refs, pp, past, lam_init):
    del pt_ref
    page_refs = refs[:pp]
    o_ref, m_ref, l_ref, acc_ref = refs[pp:]
    j = pl.program_id(1)
    rows = 2 * DIFF_H
    n_keys = PAGE_SIZE * DIFF_H
    h_shift = DIFF_H.bit_length() - 1
    q8 = q_ref[0]
    slope2 = slope_ref[...][:, :1] * LOG2E
    row = lax.broadcasted_iota(jnp.int32, (rows, n_keys), 0)
    col = lax.broadcasted_iota(jnp.int32, (rows, n_keys), 1)
    own_head = (col & (DIFF_H - 1)) == (row & (DIFF_H - 1))
    key_in_page = (col >> h_shift).astype(F32)

    @pl.when(j == 0)
    def _():
        k_new = jnp.concatenate([new_ref[0, 0]] * 2, axis=0).astype(BF16).astype(F32)
        m_ref[0] = jnp.sum(q8.astype(F32) * k_new, axis=-1, keepdims=True) * (ATT_SCALE * LOG2E) + jnp.zeros(
            (rows, LANES), F32)
        l_ref[0] = jnp.ones((rows, LANES), F32)
        acc_ref[0] = jnp.concatenate([new_ref[0, 1]] * 2, axis=0).astype(BF16).astype(F32)

    scores, values = [], []
    for i in range(pp):
        page = page_refs[i]
        k0 = (j * pp + i) * PAGE_SIZE
        kx = page[0, 0, :, 0].reshape(n_keys, DIFF_VD).astype(BF16)
        values.append(page[0, 0, :, 1].reshape(n_keys, DIFF_VD).astype(BF16))
        s = lax.dot_general(q8, kx, NT_DIMS, preferred_element_type=F32) * (ATT_SCALE * LOG2E)
        s = s - slope2 * ((past - k0).astype(F32) - key_in_page)
        scores.append(jnp.where(own_head, s, NEG_INF))
    _flash_step(scores, values, m_ref, l_ref, acc_ref, 0)

    @pl.when(j == pl.num_programs(1) - 1)
    def _():
        lam = _diff_lambda(lam_ref[...], lam_init)
        o = _flash_result(l_ref, acc_ref, 0)
        oh = o[:DIFF_H] - lam * o[DIFF_H:]
        y = oh * lax.rsqrt(jnp.mean(oh * oh, axis=-1, keepdims=True) + NORM_EPS)
        o_ref[0] = (y * dn_ref[...] * (1.0 - lam_init)).astype(o_ref.dtype)


def diff_decode(cache, layer, page_table, q8, slope8, new_row, lam_vec, dnorm, lam_init, past, pp=8):
    db, n_pages = page_table.shape
    rows = 2 * DIFF_H

    def page_spec(i):
        return pl.BlockSpec((1, 1) + cache.shape[2:], lambda b, j, pt: (layer, pt[b, j * pp + i], 0, 0, 0, 0))

    return pl.pallas_call(
        functools.partial(_diff_decode_kernel, pp=pp, past=past, lam_init=lam_init),
        grid_spec=pltpu.PrefetchScalarGridSpec(
            num_scalar_prefetch=1,
            grid=(db, n_pages // pp),
            in_specs=[pl.BlockSpec((1, rows, DIFF_VD), lambda b, j, pt: (b, 0, 0)),
                      pl.BlockSpec((rows, LANES), lambda b, j, pt: (0, 0)),
                      pl.BlockSpec((1, 2, DIFF_H, DIFF_VD), lambda b, j, pt: (b, 0, 0, 0)),
                      pl.BlockSpec((4, HEAD_DIM), lambda b, j, pt: (0, 0)),
                      pl.BlockSpec((1, DIFF_VD), lambda b, j, pt: (0, 0))] + [page_spec(i) for i in range(pp)],
            out_specs=pl.BlockSpec((1, DIFF_H, DIFF_VD), lambda b, j, pt: (b, 0, 0)),
            scratch_shapes=[pltpu.VMEM((1, rows, LANES), F32), pltpu.VMEM((1, rows, LANES), F32),
                            pltpu.VMEM((1, rows, DIFF_VD), F32)]),
        out_shape=jax.ShapeDtypeStruct((db, DIFF_H, DIFF_VD), BF16),
        compiler_params=_cparams(("parallel", "arbitrary")),
        name="diff_decode",
    )(page_table, q8, slope8, new_row, lam_vec, dnorm.reshape(1, DIFF_VD), *([cache] * pp))


def _ret_decode_kernel(lg_ref, q_ref, k_ref, v_ref, g_ref, rn_ref, st_ref, o_ref, nst_ref):
    ii = lax.broadcasted_iota(jnp.int32, (RET_DK, RET_DK), 0)
    jj = lax.broadcasted_iota(jnp.int32, (RET_DK, RET_DK), 1)
    ones = jnp.ones((1, 1), F32)
    for h in range(RET_H):
        gamma = jnp.exp(ones * lg_ref[h])
        q = q_ref[0, h:h + 1, :]
        k = k_ref[0, h:h + 1, :] * (RET_DK ** -0.5)
        v = v_ref[0, h:h + 1, :]
        state = st_ref[0, h]
        qb = q.astype(BF16).astype(F32)
        kb = k.astype(BF16).astype(F32)
        s = jnp.sum(qb * kb, axis=-1, keepdims=True)
        q_dec = jnp.broadcast_to(q * gamma, (SUBLANES, RET_DK)).astype(BF16)
        o = s * v + jnp.dot(q_dec, state.astype(BF16), preferred_element_type=F32)[0:1]
        k_col = jnp.sum(jnp.where(ii == jj, k, 0.0), axis=-1, keepdims=True)
        nst_ref[0, h] = state * gamma + k_col * v
        y = o * lax.rsqrt(jnp.mean(o * o, axis=-1, keepdims=True) + NORM_EPS) * rn_ref[h]
        gate = g_ref[0, h:h + 1, :]
        o_ref[0, h:h + 1, :] = (gate * jax.nn.sigmoid(gate) * y).astype(o_ref.dtype)


def ret_decode(q, k, v, g, rnorm, state, log_g):
    db = q.shape[0]
    return pl.pallas_call(
        _ret_decode_kernel,
        grid=(db,),
        in_specs=[pl.BlockSpec(memory_space=pltpu.SMEM),
                  pl.BlockSpec((1, RET_H, RET_DK), lambda b: (b, 0, 0)),
                  pl.BlockSpec((1, RET_H, RET_DK), lambda b: (b, 0, 0)),
                  pl.BlockSpec((1, RET_H, RET_DV), lambda b: (b, 0, 0)),
                  pl.BlockSpec((1, RET_H, RET_DV), lambda b: (b, 0, 0)),
                  pl.BlockSpec((RET_H, 1, RET_DV), lambda b: (0, 0, 0)),
                  pl.BlockSpec((1, RET_H, RET_DK, RET_DV), lambda b: (b, 0, 0, 0))],
        out_specs=[pl.BlockSpec((1, RET_H, RET_DV), lambda b: (b, 0, 0)),
                   pl.BlockSpec((1, RET_H, RET_DK, RET_DV), lambda b: (b, 0, 0, 0))],
        out_shape=[jax.ShapeDtypeStruct((db, RET_H, RET_DV), BF16),
                   jax.ShapeDtypeStruct(state.shape, F32)],
        compiler_params=_cparams(("parallel",)),
        name="ret_decode",
    )(log_g, q, k, v, g, rnorm.reshape(RET_H, 1, RET_DV), state)


def _alibi_slopes(n):
    return jnp.asarray([2.0 ** (-8.0 * (i + 1) / n) for i in range(n)], dtype=F32)


def _even_layer(e, layer, hp, hs, batch, seq, caches, page_table, even_w_in, cmp_pos, lam_vec, dnorm, stacks):
    cache_cmp, cache_sel, cache_win, cache_diff = caches
    n_even = even_w_in.shape[0]
    d = even_w_in.shape[1]
    lam_init = 0.8 - 0.6 * math.exp(-0.3 * layer)
    db = hs.shape[0]
    past = page_table.shape[1] * PAGE_SIZE
    nsa_slopes = _alibi_slopes(NSA_H)
    diff_slopes = _alibi_slopes(DIFF_H)
    kvw = 2 * NSA_KV_W
    per_group = NSA_G * 3
    w_gate = even_w_in[e, :, GATE_OFF:GATE_OFF + NSA_GATE_W].reshape(d, NSA_KVH, per_group)
    w_gate = jnp.pad(w_gate, ((0, 0), (0, 0), (0, LANES - per_group))).reshape(d, NSA_KVH * LANES).astype(BF16)
    w_tail = even_w_in[e, :, GATE_OFF + NSA_GATE_W:].astype(BF16)

    qn_bf, qn_s = dense(hp, even_w_in, NSA_Q_W, lead=e, out_dtypes=(BF16,), xs=hs)
    (cmp_st, cmp_bf), (new_cmp, _) = dense(hp, even_w_in, kvw, col0=NSA_Q_W, lead=e, out_dtypes=(F32, BF16),
                                           stack=(n_even, e, stacks[0]), xs=hs)
    (sel_st, sel_bf), (new_sel, _) = dense(hp, even_w_in, kvw, col0=NSA_Q_W + kvw, lead=e,
                                           out_dtypes=(F32, BF16), stack=(n_even, e, stacks[1]), xs=hs)
    (win_f, win_bf), (new_win, _) = dense(hp, even_w_in, kvw, col0=NSA_Q_W + 2 * kvw, lead=e,
                                          out_dtypes=(F32, BF16), xs=hs)
    gate_pre, gate_s = dense(hp, w_gate, NSA_KVH * LANES, xs=hs)
    qd_bf, qd_s = dense(hp, w_tail, DIFF_Q_W, out_dtypes=(BF16,), xs=hs)
    (kvd_st, kvd_bf), (new_diff, _) = dense(hp, w_tail, DIFF_Q_W + DIFF_V_W, col0=DIFF_Q_W,
                                            out_dtypes=(F32, BF16), stack=(n_even, e, stacks[2]), xs=hs)

    pos_cg = jnp.transpose(cmp_pos, (0, 2, 1)).reshape(2 * NSA_KVH, CMP_BLOCK)
    kcvc = compress_prompt(cmp_bf, jnp.tile(pos_cg, (1, seq // CMP_BLOCK)), batch, seq)
    o_nsa = nsa_prompt(qn_bf, gate_pre, kcvc, sel_bf, win_bf, nsa_slopes, batch, seq)
    o_diff = diff_prompt(qd_bf, kvd_bf, lam_vec, dnorm, diff_slopes, lam_init, batch, seq)
    mix_p = jnp.concatenate([o_nsa, o_diff], axis=1)
    keep_p = min(WINDOW, seq)
    win_p = win_f.reshape(batch, seq, kvw)[:, seq - keep_p:].reshape(batch, keep_p, 2, NSA_KVH, HEAD_DIM)

    gates8 = gate_s.reshape(db, NSA_KVH, LANES)[:, :, :per_group].reshape(db, NSA_H, 3)
    gates8 = jnp.pad(gates8, ((0, 0), (0, 0), (0, LANES - 3)))
    group_of_head = jnp.asarray(np.arange(NSA_H) // NSA_G)
    onehot_g = jax.nn.one_hot(group_of_head, NSA_KVH, dtype=BF16)
    qn = qn_s.reshape(db, NSA_H, HEAD_DIM)
    q8 = (qn[:, :, None, :] * onehot_g[None, :, :, None]).reshape(db, NSA_H, NSA_KVH * HEAD_DIM)
    slope8 = jnp.broadcast_to(nsa_slopes[:, None], (NSA_H, LANES))
    pos_rg = jnp.tile(jnp.transpose(cmp_pos, (0, 2, 1)), (1, SUBLANES // NSA_KVH, PAGE_SIZE // CMP_BLOCK))
    pos_rows = jnp.repeat(pos_rg, NSA_KVH, axis=2)
    kcvc_s = compress_sample(cache_cmp, e, page_table, pos_rows)
    w_buf = cache_win.shape[2]
    win_all = jnp.concatenate([cache_win[e].reshape(db, w_buf, -1), new_win[:, None, :]], axis=1)
    n_win = w_buf + 1
    win_pad = jnp.pad(win_all, ((0, 0), (0, -n_win % LANES), (0, 0)))
    o_cmp, o_win, idx = nsa_decode_cmp(q8, slope8, kcvc_s, win_pad, past, n_win)
    ids = jnp.concatenate([idx[:, 0, :TOP_N - 1], idx[:, NSA_G, :TOP_N - 1]], axis=1)
    o_nsa_s = nsa_decode_sel(cache_sel, e, page_table, ids, q8, slope8, new_sel[:, None, :], o_cmp, o_win, gates8,
                             past)

    qd = qd_s.reshape(db, DIFF_H, 2, HEAD_DIM)
    eye_c = jnp.eye(2, dtype=BF16)
    q8d = jnp.einsum('bhcd,cj->bchjd', qd, eye_c).reshape(db, 2 * DIFF_H, DIFF_VD)
    slope8d = jnp.broadcast_to(jnp.tile(diff_slopes, 2)[:, None], (2 * DIFF_H, LANES))
    o_diff_s = diff_decode(cache_diff, e, page_table, q8d, slope8d, new_diff.reshape(db, 2, DIFF_H, DIFF_VD),
                           lam_vec, dnorm, lam_init, past)
    mix_s = jnp.concatenate([o_nsa_s.reshape(db, NSA_Q_W), o_diff_s.reshape(db, DIFF_V_W)], axis=1)
    kv_s = (db, 1, 2, NSA_KVH, HEAD_DIM)
    keep = min(WINDOW, n_win)
    small = (win_p, new_cmp.reshape(kv_s), new_sel.reshape(kv_s),
             win_all[:, n_win - keep:].reshape(db, keep, 2, NSA_KVH, HEAD_DIM),
             new_diff.reshape(db, 1, 2, DIFF_H, DIFF_VD))
    return mix_p, mix_s, (cmp_st, sel_st, kvd_st), small


def _ret_layer(o, hp, hs, batch, seq, state, ret_w_in, rnorm):
    db = hs.shape[0]
    log_g = jnp.log1p(-jnp.exp2(-5.0 - jnp.arange(RET_H, dtype=F32)))
    n_in = ret_w_in.shape[2]
    proj, ps = dense(hp, ret_w_in, n_in, lead=o, xs=hs)
    gated_p, st_p = ret_prompt(proj, rnorm, log_g, batch, seq)
    hk = RET_H * RET_DK
    hv = RET_H * RET_DV
    q = ps[:, :hk].reshape(db, RET_H, RET_DK)
    k = ps[:, hk:2 * hk].reshape(db, RET_H, RET_DK)
    v = ps[:, 2 * hk:2 * hk + hv].reshape(db, RET_H, RET_DV)
    g = ps[:, 2 * hk + hv:].reshape(db, RET_H, RET_DV)
    gated_s, st_s = ret_decode(q, k, v, g, rnorm, state, log_g)
    return gated_p, gated_s.reshape(db, hv), st_p, st_s


def kernel(x_prompt, x_sample, cache_nsa_cmp, cache_nsa_sel, cache_nsa_win, cache_diff, state_ret, page_table,
           norm_mix, norm_ffn, norm_final, even_w_in, even_w_out, nsa_cmp_pos, diff_lambda, diff_norm,
           ret_w_in, ret_norm, ret_w_out, ffn_w13, ffn_w2):
    batch, seq, d = x_prompt.shape
    db = x_sample.shape[0]
    d_ff = ffn_w2.shape[1]
    xp = x_prompt.reshape(batch * seq, d)
    xs = x_sample.reshape(db, d)
    caches = (cache_nsa_cmp, cache_nsa_sel, cache_nsa_win, cache_diff)
    stacks = (None, None, None)
    small = [[] for _ in range(5)]
    ret_p, ret_s = [], []
    for layer in range(DEPTH):
        hp = rmsnorm(xp, norm_mix[layer], BF16)
        hs = rmsnorm(xs, norm_mix[layer], BF16)
        if layer % 2 == 0:
            e = layer // 2
            mix_p, mix_s, stacks, small_e = _even_layer(
                e, layer, hp, hs, batch, seq, caches, page_table, even_w_in, nsa_cmp_pos[e], diff_lambda[e],
                diff_norm[e], stacks)
            for lst, item in zip(small, small_e):
                lst.append(item)
            w_out, lead = even_w_out, e
        else:
            o = layer // 2
            mix_p, mix_s, st_p, st_s = _ret_layer(o, hp, hs, batch, seq, state_ret[o], ret_w_in, ret_norm[o])
            ret_p.append(st_p)
            ret_s.append(st_s)
            w_out, lead = ret_w_out, o
        xp, xs = dense(mix_p, w_out, d, lead=lead, res=xp, xs=mix_s, res_s=xs)
        up_p, up_s = dense(rmsnorm(xp, norm_ffn[layer], BF16), ffn_w13, d_ff, lead=layer, col3=d_ff,
                           out_dtypes=(BF16,), xs=rmsnorm(xs, norm_ffn[layer], BF16))
        xp, xs = dense(up_p, ffn_w2, d, lead=layer, res=xp, xs=up_s, res_s=xs)
    y_prompt = rmsnorm(xp, norm_final, F32).reshape(batch, seq, d)
    y_sample = rmsnorm(xs, norm_final, F32).reshape(db, 1, d)
    n_even = even_w_in.shape[0]
    cmp_st, sel_st, kvd_st = stacks
    win_p, cmp_s, sel_s, win_s, diff_s = [jnp.stack(t) for t in small]
    return (y_prompt, y_sample,
            cmp_st.reshape(n_even, batch, seq, 2, NSA_KVH, HEAD_DIM),
            sel_st.reshape(n_even, batch, seq, 2, NSA_KVH, HEAD_DIM), win_p,
            kvd_st.reshape(n_even, batch, seq, 2, DIFF_H, DIFF_VD), jnp.stack(ret_p),
            cmp_s, sel_s, win_s, diff_s, jnp.stack(ret_s))
```

```python
import functools
import math

import jax
import jax.numpy as jnp
import numpy as np
from jax import lax
from jax.experimental import pallas as pl
from jax.experimental.pallas import tpu as pltpu

F32 = jnp.float32
BF16 = jnp.bfloat16

D_MODEL = 2048
DEPTH = 4
PAGE_SIZE = 128
HEAD_DIM = 128
NSA_H = 8
NSA_KVH = 2
NSA_G = NSA_H // NSA_KVH
CMP_BLOCK = 64
CMP_SHIFT = CMP_BLOCK.bit_length() - 1
TOP_N = 16
WINDOW = 512
FORCE_SCORE = 1.0e4
DIFF_H = 4
DIFF_VD = 2 * HEAD_DIM
RET_H = 8
RET_DK = D_MODEL // RET_H
RET_DV = 2 * D_MODEL // RET_H
RET_CHUNK = 128
D_FF = ((8 * D_MODEL + 3 * 256 - 1) // (3 * 256)) * 256
NORM_EPS = 1e-6
NEG_INF = -1e30
ATT_SCALE = HEAD_DIM ** -0.5
LOG2E = math.log2(math.e)

NSA_Q_W = NSA_H * HEAD_DIM
NSA_KV_W = NSA_KVH * HEAD_DIM
NSA_GATE_W = NSA_H * 3
DIFF_Q_W = DIFF_H * 2 * HEAD_DIM
DIFF_V_W = DIFF_H * DIFF_VD
MAIN_W = NSA_Q_W + 6 * NSA_KV_W + 2 * DIFF_Q_W + DIFF_V_W
GATE_OFF = NSA_Q_W + 6 * NSA_KV_W

NSA_TQ, NSA_TK = 256, 256
DIFF_TQ, DIFF_TK = 512, 512
RET_BLOCK = 512

LANES = 128
SUBLANES = 8
SAMPLE_ROWS = 2 * SUBLANES
VMEM_LIMIT = 56 * 1024 * 1024


def _cparams(sem):
    return pltpu.CompilerParams(dimension_semantics=sem, vmem_limit_bytes=VMEM_LIMIT)


def _rmsnorm_kernel(x_ref, g_ref, o_ref):
    x = x_ref[...]
    y = x * lax.rsqrt(jnp.mean(x * x, axis=-1, keepdims=True) + NORM_EPS)
    o_ref[...] = (y * g_ref[...]).astype(o_ref.dtype)


def rmsnorm(x, g, out_dtype):
    m, d = x.shape
    tm = min(512, m)
    return pl.pallas_call(
        _rmsnorm_kernel,
        grid=(m // tm,),
        in_specs=[pl.BlockSpec((tm, d), lambda i: (i, 0)), pl.BlockSpec((1, d), lambda i: (0, 0))],
        out_specs=pl.BlockSpec((tm, d), lambda i: (i, 0)),
        out_shape=jax.ShapeDtypeStruct((m, d), out_dtype),
        compiler_params=_cparams(("parallel",)),
        name="rmsnorm",
    )(x, g.reshape(1, d))


def _dense_kernel(*refs, n_w, has_res, has_prev, has_s, n_out, cast_w, n_i):
    refs = list(refs)
    x_ref = refs.pop(0)
    xs_ref = refs.pop(0) if has_s else None
    w_refs = [refs.pop(0) for _ in range(n_w)]
    res_ref = refs.pop(0) if has_res else None
    res_s_ref = refs.pop(0) if (has_res and has_s) else None
    if has_prev:
        refs.pop(0)
    out_refs = [refs.pop(0) for _ in range(n_out)]
    out_s_refs = [refs.pop(0) for _ in range(n_out)] if has_s else []
    wb_refs = refs
    i = pl.program_id(1)
    if cast_w:
        @pl.when(i == 0)
        def _():
            for w_ref, wb_ref in zip(w_refs, wb_refs):
                wb_ref[...] = w_ref[...].astype(BF16)
        w_refs = wb_refs

    def product(x):
        acc = jnp.dot(x, w_refs[0][...], preferred_element_type=F32)
        if n_w == 2:
            acc = acc * jax.nn.sigmoid(acc) * jnp.dot(x, w_refs[1][...], preferred_element_type=F32)
        return acc

    def emit(acc, r_ref, dst_refs):
        if r_ref is not None:
            acc = r_ref[...] + acc
        for o_ref in dst_refs:
            o_ref[...] = acc.astype(o_ref.dtype)

    def plain():
        emit(product(x_ref[...]), res_ref, out_refs)

    def with_sample_rows():
        tm = x_ref.shape[0]
        acc = product(jnp.concatenate([x_ref[...], xs_ref[...]], axis=0))
        emit(acc[:tm], res_ref, out_refs)
        emit(acc[tm:], res_s_ref, out_s_refs)

    if has_s:
        pl.when(i < n_i - 1)(plain)
        pl.when(i == n_i - 1)(with_sample_rows)
    else:
        plain()


def _dense_tiles(m, k, n, n_w):
    tn = min(1024 if (k * n_w <= 2048 and n % 1024 == 0) else 512, n)
    tm = min(1024 if k * n_w <= 4096 else 512, m)
    return tm, tn


def dense(x, w, n, col0=0, lead=None, col3=None, res=None, out_dtypes=(F32,), stack=None, xs=None, res_s=None):
    m, k = x.shape
    n_w = 1 if col3 is None else 2
    tm, tn = _dense_tiles(m, k, n, n_w)
    assert m % tm == 0 and n % tn == 0 and col0 % tn == 0 and (col3 is None or col3 % tn == 0)
    cast_w = w.dtype != BF16
    has_s = xs is not None
    n_i = m // tm

    def rows(i):
        return i

    def w_spec(c0):
        if w.ndim == 3:
            return pl.BlockSpec((None, k, tn), lambda j, i: (lead, 0, c0 // tn + j))
        return pl.BlockSpec((k, tn), lambda j, i: (0, c0 // tn + j))

    in_specs = [pl.BlockSpec((tm, k), lambda j, i: (rows(i), 0))]
    args = [x]
    if has_s:
        ms = xs.shape[0]
        in_specs.append(pl.BlockSpec((ms, k), lambda j, i: (0, 0)))
        args.append(xs)
    in_specs.append(w_spec(col0))
    args.append(w)
    if n_w == 2:
        in_specs.append(w_spec(col3))
        args.append(w)
    if res is not None:
        in_specs.append(pl.BlockSpec((tm, tn), lambda j, i: (rows(i), j)))
        args.append(res)
        if has_s:
            in_specs.append(pl.BlockSpec((ms, tn), lambda j, i: (0, j)))
            args.append(res_s)
    out_specs = [pl.BlockSpec((tm, tn), lambda j, i: (rows(i), j)) for _ in out_dtypes]
    out_shape = [jax.ShapeDtypeStruct((m, n), dt) for dt in out_dtypes]
    aliases = {}
    has_prev = False
    if stack is not None:
        depth, slot, prev = stack
        out_specs[0] = pl.BlockSpec((None, tm, tn), lambda j, i: (slot, rows(i), j))
        out_shape[0] = jax.ShapeDtypeStruct((depth, m, n), out_dtypes[0])
        if prev is not None:
            has_prev = True
            aliases = {len(args): 0}
            in_specs.append(pl.BlockSpec(memory_space=pl.ANY))
            args.append(prev)
    if has_s:
        out_specs += [pl.BlockSpec((ms, tn), lambda j, i: (0, j)) for _ in out_dtypes]
        out_shape += [jax.ShapeDtypeStruct((ms, n), dt) for dt in out_dtypes]
    outs = pl.pallas_call(
        functools.partial(_dense_kernel, n_w=n_w, has_res=res is not None, has_prev=has_prev, has_s=has_s,
                          n_out=len(out_dtypes), cast_w=cast_w, n_i=n_i),
        grid=(n // tn, n_i),
        in_specs=in_specs,
        out_specs=out_specs,
        out_shape=out_shape,
        scratch_shapes=[pltpu.VMEM((k, tn), BF16)] * (n_w if cast_w else 0),
        input_output_aliases=aliases,
        compiler_params=_cparams(("parallel", "arbitrary")),
        name="dense",
    )(*args)
    n_out = len(out_dtypes)
    unwrap = lambda t: t[0] if n_out == 1 else tuple(t)
    if has_s:
        return unwrap(outs[:n_out]), unwrap(outs[n_out:])
    return unwrap(outs)


def _softmax_rows64(logits):
    e = jnp.exp(logits - jnp.max(logits, axis=-1, keepdims=True))
    reps = logits.shape[-1] // CMP_BLOCK
    return e * (reps / jnp.sum(e, axis=-1, keepdims=True))


def _compress_prompt_kernel(x_ref, pos_ref, o_ref, *, seq):
    w = _softmax_rows64(pos_ref[...])
    blk = lax.broadcasted_iota(jnp.int32, (LANES, seq), 0)
    key_blk = lax.broadcasted_iota(jnp.int32, (LANES, seq), 1) >> CMP_SHIFT
    onblk = blk == key_blk
    for c in range(4):
        wb = jnp.where(onblk, w[c:c + 1, :], 0.0).astype(BF16)
        o_ref[0, :, c * LANES:(c + 1) * LANES] = jnp.dot(
            wb, x_ref[:, c * LANES:(c + 1) * LANES], preferred_element_type=F32).astype(BF16)


def compress_prompt(cmp_bf, pos_tiled, batch, seq):
    return pl.pallas_call(
        functools.partial(_compress_prompt_kernel, seq=seq),
        grid=(batch,),
        in_specs=[pl.BlockSpec((seq, 4 * LANES), lambda b: (b, 0)),
                  pl.BlockSpec((4, seq), lambda b: (0, 0))],
        out_specs=pl.BlockSpec((1, LANES, 4 * LANES), lambda b: (b, 0, 0)),
        out_shape=jax.ShapeDtypeStruct((batch, LANES, 4 * LANES), BF16),
        compiler_params=_cparams(("parallel",)),
        name="nsa_compress_prompt",
    )(cmp_bf, pos_tiled)


def _lane_tile(x, width):
    return x if width == LANES else jnp.concatenate([x] * (width // LANES), axis=-1)


def _flash_step(s, v, m_ref, l_ref, acc_ref, idx):
    ss, vs = (s, v) if isinstance(s, (list, tuple)) else ([s], [v])
    tiles = [[t[:, j * LANES:(j + 1) * LANES] for j in range(t.shape[1] // LANES)] for t in ss]
    blocks = [blk for tile in tiles for blk in tile]
    mx = blocks[0]
    for blk in blocks[1:]:
        mx = jnp.maximum(mx, blk)
    m_prev = m_ref[idx]
    m_new = jnp.maximum(m_prev, jnp.max(mx, axis=-1, keepdims=True))
    alpha = jnp.exp2(m_prev - m_new)
    ps = [[jnp.exp2(blk - m_new) for blk in tile] for tile in tiles]
    row_sum = None
    for tile in ps:
        for p in tile:
            row_sum = p if row_sum is None else row_sum + p
    l_ref[idx] = alpha * l_ref[idx] + jnp.sum(row_sum, axis=-1, keepdims=True)
    m_ref[idx] = m_new
    acc = _lane_tile(alpha, vs[0].shape[-1]) * acc_ref[idx]
    for tile, vt in zip(ps, vs):
        acc = acc + jnp.dot(jnp.concatenate([p.astype(BF16) for p in tile], axis=-1), vt,
                            preferred_element_type=F32)
    acc_ref[idx] = acc


def _flash_init(m_ref, l_ref, acc_ref):
    m_ref[...] = jnp.full(m_ref.shape, NEG_INF, F32)
    l_ref[...] = jnp.zeros(l_ref.shape, F32)
    acc_ref[...] = jnp.zeros(acc_ref.shape, F32)


def _flash_result(l_ref, acc_ref, idx):
    return acc_ref[idx] / _lane_tile(l_ref[idx], acc_ref.shape[-1])


def _softmax_pv(s, v):
    p = jnp.exp2(s - jnp.max(s, axis=-1, keepdims=True))
    l = jnp.sum(p, axis=-1, keepdims=True)
    return jnp.dot(p.astype(BF16), v, preferred_element_type=F32) / l


def _nsa_prompt_kernel(slopes_ref, q_ref, gate_ref, kc_ref, vc_ref, ks_ref, vs_ref, kw_ref, vw_ref,
                       o_ref, m_ref, l_ref, acc_ref, flag_ref, *, tq, tk, n_blocks):
    g = pl.program_id(1)
    qi = pl.program_id(2)
    q0 = qi * tq
    row = lax.broadcasted_iota(jnp.int32, (tq, LANES), 0)
    col = lax.broadcasted_iota(jnp.int32, (tq, LANES), 1)
    qpos = q0 + row
    nt = (((1,), (1,)), ((), ()))

    kc = kc_ref[0]
    vc = vc_ref[0]
    cdist = qpos - (col * CMP_BLOCK + (CMP_BLOCK - 1))
    cmask = cdist >= 0
    cdist_f = cdist.astype(F32)
    imp = jnp.zeros((tq, LANES), F32)
    o_cmp = []
    for h in range(NSA_G):
        slope = slopes_ref[g * NSA_G + h]
        qh = q_ref[:, h * HEAD_DIM:(h + 1) * HEAD_DIM]
        s = lax.dot_general(qh, kc, nt, preferred_element_type=F32) * ATT_SCALE - slope * cdist_f
        s = jnp.where(cmask, s, NEG_INF)
        p = jnp.exp(s - jnp.max(s, axis=-1, keepdims=True))
        p = jnp.where(cmask, p / jnp.sum(p, axis=-1, keepdims=True), 0.0)
        imp = imp + p
        o_cmp.append(jnp.dot(p.astype(BF16), vc, preferred_element_type=F32))

    imp = jnp.where(cmask, imp, -1.0)
    forced = (col == (qpos >> CMP_SHIFT)) | (col == 0)
    imp = jnp.where(forced, FORCE_SCORE, imp)
    nblk = -(-n_blocks // SUBLANES) * SUBLANES
    imp_t = imp.T[:nblk]
    sub = lax.broadcasted_iota(jnp.int32, (SUBLANES, tq), 0)
    sel_rows = []
    for vi in range(nblk // SUBLANES):
        x = imp_t[vi * SUBLANES:(vi + 1) * SUBLANES]
        rank = jnp.zeros((SUBLANES, tq), F32)
        for j in range(nblk):
            r = imp_t[j:j + 1, :]
            jv, jr = divmod(j, SUBLANES)
            ge = jnp.where(r >= x, 1.0, 0.0)
            gt = jnp.where(r > x, 1.0, 0.0)
            if jv < vi:
                rank = rank + ge
            elif jv > vi:
                rank = rank + gt
            else:
                rank = rank + jnp.where(sub > jr, ge, gt)
        sel_rows.append(jnp.where(rank < TOP_N, 1.0, 0.0))
    sel_rows.append(jnp.zeros((LANES - nblk, tq), F32))
    sel_f = jnp.concatenate(sel_rows, axis=0).T
    sel = sel_f.astype(BF16)
    picked_any = jnp.max(sel_f, axis=0, keepdims=True)
    tile_of_block = lax.broadcasted_iota(jnp.int32, (1, LANES), 1) >> ((tk // CMP_BLOCK).bit_length() - 1)
    for kt in range(n_blocks * CMP_BLOCK // tk):
        hit = jnp.max(jnp.where(tile_of_block == kt, picked_any, 0.0))
        flag_ref[kt] = (hit > 0.5).astype(jnp.int32)

    c1 = ATT_SCALE * LOG2E
    slope2 = [slopes_ref[g * NSA_G + h] * LOG2E for h in range(NSA_G)]

    _flash_init(m_ref, l_ref, acc_ref)
    e_row = lax.broadcasted_iota(jnp.int32, (LANES, tk), 0)
    e_col = lax.broadcasted_iota(jnp.int32, (LANES, tk), 1)
    rel = lax.broadcasted_iota(jnp.int32, (1, tk), 1)
    krow = lax.broadcasted_iota(jnp.int32, (tq, tk), 0)
    kcol = lax.broadcasted_iota(jnp.int32, (tq, tk), 1)

    def sel_tile(kt, diagonal):
        k0 = pl.multiple_of(kt * tk, tk)
        k = ks_ref[pl.ds(k0, tk), :]
        v = vs_ref[pl.ds(k0, tk), :]
        expand = jnp.where(((k0 + e_col) >> CMP_SHIFT) == e_row, 1.0, 0.0).astype(BF16)
        picked = jnp.dot(sel, expand, preferred_element_type=F32)
        if diagonal:
            picked = jnp.where(k0 + kcol <= q0 + krow, picked, 0.0)
        valid = picked > 0.5
        bias = (k0 - q0 + rel).astype(F32)
        for h in range(NSA_G):
            qh = q_ref[:, h * HEAD_DIM:(h + 1) * HEAD_DIM]
            s = lax.dot_general(qh, k, nt, preferred_element_type=F32) * c1 + slope2[h] * bias
            _flash_step(jnp.where(valid, s, NEG_INF), v, m_ref, l_ref, acc_ref, h)

    kd = q0 // tk
    sel_tile(kd, True)

    def sel_body(i, carry):
        kt = kd - 1 - i
        pl.when(flag_ref[kt] > 0)(lambda: sel_tile(kt, False))
        return carry

    lax.fori_loop(0, kd, sel_body, 0)
    o_sel = [_flash_result(l_ref, acc_ref, h) for h in range(NSA_G)]

    n_band = WINDOW + tq
    w0 = pl.multiple_of(jnp.maximum(q0 - WINDOW, 0), LANES)
    kw = kw_ref[pl.ds(w0, n_band), :]
    vw = vw_ref[pl.ds(w0, n_band), :]
    wdist = (q0 + lax.broadcasted_iota(jnp.int32, (tq, n_band), 0)) - (
        w0 + lax.broadcasted_iota(jnp.int32, (tq, n_band), 1))
    wvalid = (wdist >= 0) & (wdist <= WINDOW)
    wbias = (w0 - q0 + lax.broadcasted_iota(jnp.int32, (1, n_band), 1)).astype(F32)
    o_win = []
    for h in range(NSA_G):
        qh = q_ref[:, h * HEAD_DIM:(h + 1) * HEAD_DIM]
        s = lax.dot_general(qh, kw, nt, preferred_element_type=F32) * c1 + slope2[h] * wbias
        o_win.append(_softmax_pv(jnp.where(wvalid, s, NEG_INF), vw))

    gates = jax.nn.sigmoid(gate_ref[...])
    for h in range(NSA_G):
        o = (gates[:, 3 * h:3 * h + 1] * o_cmp[h] + gates[:, 3 * h + 1:3 * h + 2] * o_sel[h]
             + gates[:, 3 * h + 2:3 * h + 3] * o_win[h])
        o_ref[:, h * HEAD_DIM:(h + 1) * HEAD_DIM] = o.astype(o_ref.dtype)


def nsa_prompt(qn_bf, gate_pre, kcvc, sel_bf, win_bf, slopes, batch, seq, tq=NSA_TQ, tk=NSA_TK):
    assert tk % tq == 0 and seq % tk == 0 and seq >= WINDOW + tq and seq // CMP_BLOCK <= LANES
    nq = seq // tq
    gw = NSA_G * HEAD_DIM

    def kv_spec(kv):
        return pl.BlockSpec((seq, HEAD_DIM), lambda b, g, i: (b, NSA_KVH * kv + g))

    return pl.pallas_call(
        functools.partial(_nsa_prompt_kernel, tq=tq, tk=tk, n_blocks=seq // CMP_BLOCK),
        grid=(batch, NSA_KVH, nq),
        in_specs=[pl.BlockSpec(memory_space=pltpu.SMEM),
                  pl.BlockSpec((tq, gw), lambda b, g, i: (b * nq + i, g)),
                  pl.BlockSpec((tq, LANES), lambda b, g, i: (b * nq + i, g)),
                  pl.BlockSpec((1, LANES, HEAD_DIM), lambda b, g, i: (b, 0, g)),
                  pl.BlockSpec((1, LANES, HEAD_DIM), lambda b, g, i: (b, 0, 2 + g)),
                  kv_spec(0), kv_spec(1), kv_spec(0), kv_spec(1)],
        out_specs=pl.BlockSpec((tq, gw), lambda b, g, i: (b * nq + i, g)),
        out_shape=jax.ShapeDtypeStruct((batch * seq, NSA_Q_W), BF16),
        scratch_shapes=[pltpu.VMEM((NSA_G, tq, LANES), F32), pltpu.VMEM((NSA_G, tq, LANES), F32),
                        pltpu.VMEM((NSA_G, tq, HEAD_DIM), F32), pltpu.SMEM((seq // tk,), jnp.int32)],
        compiler_params=_cparams(("parallel", "parallel", "arbitrary")),
        name="nsa_prompt",
    )(slopes, qn_bf, gate_pre, kcvc, kcvc, sel_bf, sel_bf, win_bf, win_bf)


def _diff_lambda(lv, lam_init):
    a = jnp.sum(lv[0:1] * lv[1:2], axis=-1, keepdims=True)
    b = jnp.sum(lv[2:3] * lv[3:4], axis=-1, keepdims=True)
    return jnp.exp(a) - jnp.exp(b) + lam_init


def _diff_prompt_kernel(slopes_ref, q_ref, k_ref, v_ref, lam_ref, dn_ref, o_ref, m_ref, l_ref, acc_ref,
                        *, tq, tk, lam_init):
    h = pl.program_id(1)
    qi = pl.program_id(2)
    q0 = qi * tq
    krow = lax.broadcasted_iota(jnp.int32, (tq, tk), 0)
    kcol = lax.broadcasted_iota(jnp.int32, (tq, tk), 1)
    rel = lax.broadcasted_iota(jnp.int32, (1, tk), 1)
    c1 = ATT_SCALE * LOG2E
    slope2 = slopes_ref[h] * LOG2E
    nt = (((1,), (1,)), ((), ()))
    _flash_init(m_ref, l_ref, acc_ref)

    def tile(kt, diagonal):
        k0 = pl.multiple_of(kt * tk, tk)
        k = k_ref[pl.ds(k0, tk), :]
        v = v_ref[pl.ds(k0, tk), :]
        bias = slope2 * (k0 - q0 + rel).astype(F32)
        for c in range(2):
            s = lax.dot_general(q_ref[:, c * HEAD_DIM:(c + 1) * HEAD_DIM], k[:, c * HEAD_DIM:(c + 1) * HEAD_DIM],
                                nt, preferred_element_type=F32) * c1 + bias
            if diagonal:
                s = jnp.where(k0 + kcol <= q0 + krow, s, NEG_INF)
            _flash_step(s, v, m_ref, l_ref, acc_ref, c)

    kd = q0 // tk
    tile(kd, True)

    def body(i, carry):
        tile(kd - 1 - i, False)
        return carry

    lax.fori_loop(0, kd, body, 0)
    lam = _diff_lambda(lam_ref[...], lam_init)
    o = _flash_result(l_ref, acc_ref, 0) - lam * _flash_result(l_ref, acc_ref, 1)
    y = o * lax.rsqrt(jnp.mean(o * o, axis=-1, keepdims=True) + NORM_EPS)
    o_ref[...] = (y * dn_ref[...] * (1.0 - lam_init)).astype(o_ref.dtype)


def diff_prompt(qd_bf, kvd_bf, lam_vec, dnorm, slopes, lam_init, batch, seq, tq=DIFF_TQ, tk=DIFF_TK):
    assert tk % tq == 0 and seq % tk == 0
    nq = seq // tq
    w = 2 * HEAD_DIM
    qb, kb, vb = 0, 0, DIFF_H
    return pl.pallas_call(
        functools.partial(_diff_prompt_kernel, tq=tq, tk=tk, lam_init=lam_init),
        grid=(batch, DIFF_H, nq),
        in_specs=[pl.BlockSpec(memory_space=pltpu.SMEM),
                  pl.BlockSpec((tq, w), lambda b, h, i: (b * nq + i, qb + h)),
                  pl.BlockSpec((seq, w), lambda b, h, i: (b, kb + h)),
                  pl.BlockSpec((seq, w), lambda b, h, i: (b, vb + h)),
                  pl.BlockSpec((4, HEAD_DIM), lambda b, h, i: (0, 0)),
                  pl.BlockSpec((1, DIFF_VD), lambda b, h, i: (0, 0))],
        out_specs=pl.BlockSpec((tq, w), lambda b, h, i: (b * nq + i, h)),
        out_shape=jax.ShapeDtypeStruct((batch * seq, DIFF_V_W), BF16),
        scratch_shapes=[pltpu.VMEM((2, tq, LANES), F32), pltpu.VMEM((2, tq, LANES), F32),
                        pltpu.VMEM((2, tq, DIFF_VD), F32)],
        compiler_params=_cparams(("parallel", "parallel", "arbitrary")),
        name="diff_prompt",
    )(slopes, qd_bf, kvd_bf, kvd_bf, lam_vec, dnorm.reshape(1, DIFF_VD))


def _ret_prompt_kernel(lg_ref, q_ref, k_ref, v_ref, g_ref, rn_ref, o_ref, st_ref, *, chunk):
    h = pl.program_id(1)
    c = pl.program_id(2)
    lg = lg_ref[h]

    @pl.when(c == 0)
    def _():
        st_ref[...] = jnp.zeros(st_ref.shape, F32)

    ii = lax.broadcasted_iota(jnp.int32, (chunk, chunk), 0)
    jj = lax.broadcasted_iota(jnp.int32, (chunk, chunk), 1)
    d = (ii - jj).astype(F32)
    decay = jnp.where(d >= 0, jnp.exp(lg * jnp.maximum(d, 0.0)), 0.0)
    ik = lax.broadcasted_iota(jnp.int32, (chunk, RET_DK), 0).astype(F32)
    q_dec = jnp.exp(lg * (ik + 1.0))
    k_dec = jnp.exp(lg * (chunk - 1.0 - ik))
    ones = jnp.ones((1, 1), F32)
    g_chunk = jnp.exp(ones * (lg * chunk))

    q = q_ref[...]
    k = k_ref[...] * (RET_DK ** -0.5)
    v = v_ref[...].astype(BF16)
    state = st_ref[0, 0]
    s = lax.dot_general(q.astype(BF16), k.astype(BF16), (((1,), (1,)), ((), ())),
                        preferred_element_type=F32) * decay
    o = jnp.dot(s.astype(BF16), v, preferred_element_type=F32)
    o = o + jnp.dot((q * q_dec).astype(BF16), state.astype(BF16), preferred_element_type=F32)
    kv = lax.dot_general((k * k_dec).astype(BF16), v, (((0,), (0,)), ((), ())), preferred_element_type=F32)
    st_ref[0, 0] = state * g_chunk + kv

    y = o * lax.rsqrt(jnp.mean(o * o, axis=-1, keepdims=True) + NORM_EPS) * rn_ref[0]
    gate = g_ref[...]
    o_ref[...] = (gate * jax.nn.sigmoid(gate) * y).astype(o_ref.dtype)


def ret_prompt(proj, rnorm, log_g, batch, seq, chunk=RET_BLOCK):
    nc = seq // chunk
    kb = RET_H
    vb = 2 * RET_H * RET_DK // RET_DV
    gb = vb + RET_H
    return pl.pallas_call(
        functools.partial(_ret_prompt_kernel, chunk=chunk),
        grid=(batch, RET_H, nc),
        in_specs=[pl.BlockSpec(memory_space=pltpu.SMEM),
                  pl.BlockSpec((chunk, RET_DK), lambda b, h, c: (b * nc + c, h)),
                  pl.BlockSpec((chunk, RET_DK), lambda b, h, c: (b * nc + c, kb + h)),
                  pl.BlockSpec((chunk, RET_DV), lambda b, h, c: (b * nc + c, vb + h)),
                  pl.BlockSpec((chunk, RET_DV), lambda b, h, c: (b * nc + c, gb + h)),
                  pl.BlockSpec((1, 1, RET_DV), lambda b, h, c: (h, 0, 0))],
        out_specs=[pl.BlockSpec((chunk, RET_DV), lambda b, h, c: (b * nc + c, h)),
                   pl.BlockSpec((1, 1, RET_DK, RET_DV), lambda b, h, c: (b, h, 0, 0))],
        out_shape=[jax.ShapeDtypeStruct((batch * seq, RET_H * RET_DV), BF16),
                   jax.ShapeDtypeStruct((batch, RET_H, RET_DK, RET_DV), F32)],
        compiler_params=_cparams(("parallel", "parallel", "arbitrary")),
        name="ret_prompt",
    )(log_g, proj, proj, proj, proj, rnorm.reshape(RET_H, 1, RET_DV))


NT_DIMS = (((1,), (1,)), ((), ()))


def _compress_sample_kernel(pt_ref, pos_ref, *refs, pp):
    del pt_ref
    o_ref = refs[pp]
    per_page = PAGE_SIZE // CMP_BLOCK
    n_rows = PAGE_SIZE * NSA_KVH
    row = lax.broadcasted_iota(jnp.int32, (SUBLANES, n_rows), 0)
    col = lax.broadcasted_iota(jnp.int32, (SUBLANES, n_rows), 1)
    g_shift = NSA_KVH.bit_length() - 1
    member = ((col & (NSA_KVH - 1)) == (row & (NSA_KVH - 1))) & ((col >> (g_shift + CMP_SHIFT)) == (row >> g_shift))
    weights = []
    for kv in range(2):
        logits = jnp.where(member, pos_ref[kv], NEG_INF)
        e = jnp.exp(logits - jnp.max(logits, axis=-1, keepdims=True))
        weights.append((e / jnp.sum(e, axis=-1, keepdims=True)).astype(BF16))
    for i in range(pp):
        for kv in range(2):
            x = refs[i][0, 0, :, kv].reshape(n_rows, HEAD_DIM).astype(BF16)
            res = jnp.dot(weights[kv], x, preferred_element_type=F32)
            for half in range(per_page):
                for g in range(NSA_KVH):
                    c0 = (kv * NSA_KVH + g) * HEAD_DIM
                    o_ref[0, i * per_page + half:i * per_page + half + 1, c0:c0 + HEAD_DIM] = (
                        res[half * NSA_KVH + g:half * NSA_KVH + g + 1])


def compress_sample(cache, layer, page_table, pos_rows, pp=16):
    db, n_pages = page_table.shape
    per_page = PAGE_SIZE // CMP_BLOCK
    width = 2 * NSA_KVH * HEAD_DIM
    assert per_page * NSA_KVH <= SUBLANES

    def page_spec(i):
        return pl.BlockSpec((1, 1) + cache.shape[2:], lambda b, j, pt: (layer, pt[b, j * pp + i], 0, 0, 0, 0))

    return pl.pallas_call(
        functools.partial(_compress_sample_kernel, pp=pp),
        grid_spec=pltpu.PrefetchScalarGridSpec(
            num_scalar_prefetch=1,
            grid=(db, n_pages // pp),
            in_specs=[pl.BlockSpec(pos_rows.shape, lambda b, j, pt: (0, 0, 0))] + [page_spec(i) for i in range(pp)],
            out_specs=pl.BlockSpec((1, pp * per_page, width), lambda b, j, pt: (b, j, 0))),
        out_shape=jax.ShapeDtypeStruct((db, n_pages * per_page, width), F32),
        compiler_params=_cparams(("parallel", "arbitrary")),
        name="nsa_compress_sample",
    )(page_table, pos_rows, *([cache] * pp))


def _pick_group(x):
    row = lax.broadcasted_iota(jnp.int32, (NSA_H, HEAD_DIM), 0)
    return jnp.where(row < NSA_G, x[:, :HEAD_DIM], x[:, HEAD_DIM:])


def _nsa_decode_cmp_kernel(q_ref, slope_ref, kcvc_ref, win_ref, ocmp_ref, owin_ref, idx_ref, *, past, n_win):
    q8 = q_ref[0]
    kcvc = kcvc_ref[0]
    nb = kcvc.shape[0]
    kvw = NSA_KVH * HEAD_DIM
    lane = lax.broadcasted_iota(jnp.int32, (NSA_H, nb), 1)
    row = lax.broadcasted_iota(jnp.int32, (NSA_H, nb), 0)
    slope = slope_ref[...][:, :1]
    cdist = (past - (lane * CMP_BLOCK + CMP_BLOCK - 1)).astype(F32)
    s = lax.dot_general(q8, kcvc[:, :kvw].astype(BF16), NT_DIMS, preferred_element_type=F32) * ATT_SCALE
    s = s - slope * cdist
    p = jnp.exp(s - jnp.max(s, axis=-1, keepdims=True))
    p = p / jnp.sum(p, axis=-1, keepdims=True)
    ocmp_ref[0] = _pick_group(jnp.dot(p.astype(BF16), kcvc[:, kvw:].astype(BF16), preferred_element_type=F32))

    g0 = p[0:1] + p[1:2] + p[2:3] + p[3:4]
    g1 = p[4:5] + p[5:6] + p[6:7] + p[7:8]
    x = jnp.where(row < NSA_G, g0, g1)
    x = jnp.where(lane == 0, FORCE_SCORE, x)
    out_lane = lax.broadcasted_iota(jnp.int32, (NSA_H, LANES), 1)
    lane_f = lane.astype(F32)
    picked = jnp.zeros((NSA_H, LANES), F32)
    for t in range(TOP_N - 1):
        mx = jnp.max(x, axis=-1, keepdims=True)
        idx = jnp.min(jnp.where(x == mx, lane_f, float(nb)), axis=-1, keepdims=True)
        picked = jnp.where(out_lane == t, idx, picked)
        x = jnp.where(lane_f == idx, -2.0, x)
    idx_ref[0] = picked.astype(jnp.int32)

    win = win_ref[0]
    nw = win.shape[0]
    wl = lax.broadcasted_iota(jnp.int32, (NSA_H, nw), 1)
    wdist = (n_win - 1 - wl)
    valid = wdist >= 0
    s = lax.dot_general(q8, win[:, :kvw].astype(BF16), NT_DIMS, preferred_element_type=F32) * ATT_SCALE
    s = jnp.where(valid, s - slope * wdist.astype(F32), NEG_INF)
    p = jnp.exp(s - jnp.max(s, axis=-1, keepdims=True))
    p = p / jnp.sum(p, axis=-1, keepdims=True)
    owin_ref[0] = _pick_group(jnp.dot(p.astype(BF16), win[:, kvw:].astype(BF16), preferred_element_type=F32))


def nsa_decode_cmp(q8, slope8, kcvc, win_all, past, n_win):
    db = q8.shape[0]
    nb = kcvc.shape[1]
    nw = win_all.shape[1]
    w = kcvc.shape[2]
    head_out = jax.ShapeDtypeStruct((db, NSA_H, HEAD_DIM), F32)
    head_spec = pl.BlockSpec((1, NSA_H, HEAD_DIM), lambda b: (b, 0, 0))
    return pl.pallas_call(
        functools.partial(_nsa_decode_cmp_kernel, past=past, n_win=n_win),
        grid=(db,),
        in_specs=[pl.BlockSpec((1, NSA_H, NSA_KVH * HEAD_DIM), lambda b: (b, 0, 0)),
                  pl.BlockSpec((NSA_H, LANES), lambda b: (0, 0)),
                  pl.BlockSpec((1, nb, w), lambda b: (b, 0, 0)),
                  pl.BlockSpec((1, nw, w), lambda b: (b, 0, 0))],
        out_specs=[head_spec, head_spec, pl.BlockSpec((1, NSA_H, LANES), lambda b: (b, 0, 0))],
        out_shape=[head_out, head_out, jax.ShapeDtypeStruct((db, NSA_H, LANES), jnp.int32)],
        compiler_params=_cparams(("parallel",)),
        name="nsa_decode_cmp",
    )(q8, slope8, kcvc, win_all)


def _nsa_decode_sel_kernel(pt_ref, ids_ref, q_ref, slope_ref, new_ref, pa_ref, pb_ref, ocmp_ref, owin_ref,
                           gate_ref, o_ref, m_ref, l_ref, acc_ref, *, past, n_sel):
    del pt_ref
    b = pl.program_id(0)
    t = pl.program_id(1)
    q8 = q_ref[0]
    row = lax.broadcasted_iota(jnp.int32, (NSA_H, LANES), 0)
    lane = lax.broadcasted_iota(jnp.int32, (NSA_H, LANES), 1)
    slope = slope_ref[...][:, :1]
    kvw = NSA_KVH * HEAD_DIM

    @pl.when(t == 0)
    def _():
        new = new_ref[0].astype(BF16).astype(F32)
        m_ref[...] = jnp.sum(q8.astype(F32) * new[:, :kvw], axis=-1, keepdims=True) * ATT_SCALE + jnp.zeros(
            (NSA_H, LANES), F32)
        l_ref[...] = jnp.ones((NSA_H, LANES), F32)
        acc_ref[...] = jnp.where(row < NSA_G, new[:, kvw:kvw + HEAD_DIM], new[:, kvw + HEAD_DIM:])

    na = ids_ref[b, t]
    nbk = ids_ref[b, n_sel + t]
    blk = jnp.where(row < NSA_G, na, nbk)
    per_page = PAGE_SIZE // CMP_BLOCK
    valid = (lane >> CMP_SHIFT) == (blk & (per_page - 1))
    dist = past - ((blk >> (per_page.bit_length() - 1)) * PAGE_SIZE + lane)
    sa = lax.dot_general(q8[:, :HEAD_DIM], pa_ref[0, 0, :, 0, 0, :].astype(BF16), NT_DIMS,
                         preferred_element_type=F32)
    sb = lax.dot_general(q8[:, HEAD_DIM:], pb_ref[0, 0, :, 0, 1, :].astype(BF16), NT_DIMS,
                         preferred_element_type=F32)
    s = jnp.where(row < NSA_G, sa, sb) * ATT_SCALE - slope * dist.astype(F32)
    s = jnp.where(valid, s, NEG_INF)
    m_prev = m_ref[...]
    m_new = jnp.maximum(m_prev, jnp.max(s, axis=-1, keepdims=True))
    alpha = jnp.exp(m_prev - m_new)
    p = jnp.exp(s - m_new)
    l_ref[...] = alpha * l_ref[...] + jnp.sum(p, axis=-1, keepdims=True)
    m_ref[...] = m_new
    pb16 = p.astype(BF16)
    oa = jnp.dot(pb16, pa_ref[0, 0, :, 1, 0, :].astype(BF16), preferred_element_type=F32)
    ob = jnp.dot(pb16, pb_ref[0, 0, :, 1, 1, :].astype(BF16), preferred_element_type=F32)
    acc_ref[...] = alpha * acc_ref[...] + jnp.where(row < NSA_G, oa, ob)

    @pl.when(t == n_sel - 1)
    def _():
        gates = jax.nn.sigmoid(gate_ref[0])
        o = (gates[:, 0:1] * ocmp_ref[0] + gates[:, 1:2] * (acc_ref[...] / l_ref[...])
             + gates[:, 2:3] * owin_ref[0])
        o_ref[0] = o.astype(o_ref.dtype)


def nsa_decode_sel(cache, layer, page_table, ids, q8, slope8, new_row, o_cmp, o_win, gates, past):
    db = q8.shape[0]
    n_sel = ids.shape[1] // NSA_KVH
    width = 2 * NSA_KVH * HEAD_DIM
    per_page = PAGE_SIZE // CMP_BLOCK
    head_spec = pl.BlockSpec((1, NSA_H, HEAD_DIM), lambda b, t, pt, ids: (b, 0, 0))

    def page_spec(g):
        return pl.BlockSpec((1, 1) + cache.shape[2:],
                            lambda b, t, pt, ids: (layer, pt[b, ids[b, g * n_sel + t] // per_page], 0, 0, 0, 0))

    return pl.pallas_call(
        functools.partial(_nsa_decode_sel_kernel, past=past, n_sel=n_sel),
        grid_spec=pltpu.PrefetchScalarGridSpec(
            num_scalar_prefetch=2,
            grid=(db, n_sel),
            in_specs=[pl.BlockSpec((1, NSA_H, NSA_KVH * HEAD_DIM), lambda b, t, pt, ids: (b, 0, 0)),
                      pl.BlockSpec((NSA_H, LANES), lambda b, t, pt, ids: (0, 0)),
                      pl.BlockSpec((1, 1, width), lambda b, t, pt, ids: (b, 0, 0)),
                      page_spec(0), page_spec(1), head_spec, head_spec, head_spec],
            out_specs=head_spec,
            scratch_shapes=[pltpu.VMEM((NSA_H, LANES), F32), pltpu.VMEM((NSA_H, LANES), F32),
                            pltpu.VMEM((NSA_H, HEAD_DIM), F32)]),
        out_shape=jax.ShapeDtypeStruct((db, NSA_H, HEAD_DIM), BF16),
        compiler_params=_cparams(("parallel", "arbitrary")),
        name="nsa_decode_sel",
    )(page_table, ids, q8, slope8, new_row, cache, cache, o_cmp, o_win, gates)


def _diff_decode_kernel(pt_ref, q_ref, slope_ref, new_ref, lam_ref, dn_ref, *refs, pp, past, lam_init):
    del pt_ref
    page_refs = refs[:pp]
    o_ref, m_ref, l_ref, acc_ref = refs[pp:]
    j = pl.program_id(1)
    rows = 2 * DIFF_H
    n_keys = PAGE_SIZE * DIFF_H
    h_shift = DIFF_H.bit_length() - 1
    q8 = q_ref[0]
    slope2 = slope_ref[...][:, :1] * LOG2E
    row = lax.broadcasted_iota(jnp.int32, (rows, n_keys), 0)
    col = lax.broadcasted_iota(jnp.int32, (rows, n_keys), 1)
    own_head = (col & (DIFF_H - 1)) == (row & (DIFF_H - 1))
    key_in_page = (col >> h_shift).astype(F32)

    @pl.when(j == 0)
    def _():
        k_new = jnp.concatenate([new_ref[0, 0]] * 2, axis=0).astype(BF16).astype(F32)
        m_ref[0] = jnp.sum(q8.astype(F32) * k_new, axis=-1, keepdims=True) * (ATT_SCALE * LOG2E) + jnp.zeros(
            (rows, LANES), F32)
        l_ref[0] = jnp.ones((rows, LANES), F32)
        acc_ref[0] = jnp.concatenate([new_ref[0, 1]] * 2, axis=0).astype(BF16).astype(F32)

    scores, values = [], []
    for i in range(pp):
        page = page_refs[i]
        k0 = (j * pp + i) * PAGE_SIZE
        kx = page[0, 0, :, 0].reshape(n_keys, DIFF_VD).astype(BF16)
        values.append(page[0, 0, :, 1].reshape(n_keys, DIFF_VD).astype(BF16))
        s = lax.dot_general(q8, kx, NT_DIMS, preferred_element_type=F32) * (ATT_SCALE * LOG2E)
        s = s - slope2 * ((past - k0).astype(F32) - key_in_page)
        scores.append(jnp.where(own_head, s, NEG_INF))
    _flash_step(scores, values, m_ref, l_ref, acc_ref, 0)

    @pl.when(j == pl.num_programs(1) - 1)
    def _():
        lam = _diff_lambda(lam_ref[...], lam_init)
        o = _flash_result(l_ref, acc_ref, 0)
        oh = o[:DIFF_H] - lam * o[DIFF_H:]
        y = oh * lax.rsqrt(jnp.mean(oh * oh, axis=-1, keepdims=True) + NORM_EPS)
        o_ref[0] = (y * dn_ref[...] * (1.0 - lam_init)).astype(o_ref.dtype)


def diff_decode(cache, layer, page_table, q8, slope8, new_row, lam_vec, dnorm, lam_init, past, pp=8):
    db, n_pages = page_table.shape
    rows = 2 * DIFF_H

    def page_spec(i):
        return pl.BlockSpec((1, 1) + cache.shape[2:], lambda b, j, pt: (layer, pt[b, j * pp + i], 0, 0, 0, 0))

    return pl.pallas_call(
        functools.partial(_diff_decode_kernel, pp=pp, past=past, lam_init=lam_init),
        grid_spec=pltpu.PrefetchScalarGridSpec(
            num_scalar_prefetch=1,
            grid=(db, n_pages // pp),
            in_specs=[pl.BlockSpec((1, rows, DIFF_VD), lambda b, j, pt: (b, 0, 0)),
                      pl.BlockSpec((rows, LANES), lambda b, j, pt: (0, 0)),
                      pl.BlockSpec((1, 2, DIFF_H, DIFF_VD), lambda b, j, pt: (b, 0, 0, 0)),
                      pl.BlockSpec((4, HEAD_DIM), lambda b, j, pt: (0, 0)),
                      pl.BlockSpec((1, DIFF_VD), lambda b, j, pt: (0, 0))] + [page_spec(i) for i in range(pp)],
            out_specs=pl.BlockSpec((1, DIFF_H, DIFF_VD), lambda b, j, pt: (b, 0, 0)),
            scratch_shapes=[pltpu.VMEM((1, rows, LANES), F32), pltpu.VMEM((1, rows, LANES), F32),
                            pltpu.VMEM((1, rows, DIFF_VD), F32)]),
        out_shape=jax.ShapeDtypeStruct((db, DIFF_H, DIFF_VD), BF16),
        compiler_params=_cparams(("parallel", "arbitrary")),
        name="diff_decode",
    )(page_table, q8, slope8, new_row, lam_vec, dnorm.reshape(1, DIFF_VD), *([cache] * pp))


def _ret_decode_kernel(lg_ref, q_ref, k_ref, v_ref, g_ref, rn_ref, st_ref, o_ref, nst_ref):
    ii = lax.broadcasted_iota(jnp.int32, (RET_DK, RET_DK), 0)
    jj = lax.broadcasted_iota(jnp.int32, (RET_DK, RET_DK), 1)
    ones = jnp.ones((1, 1), F32)
    for h in range(RET_H):
        gamma = jnp.exp(ones * lg_ref[h])
        q = q_ref[0, h:h + 1, :]
        k = k_ref[0, h:h + 1, :] * (RET_DK ** -0.5)
        v = v_ref[0, h:h + 1, :]
        state = st_ref[0, h]
        qb = q.astype(BF16).astype(F32)
        kb = k.astype(BF16).astype(F32)
        s = jnp.sum(qb * kb, axis=-1, keepdims=True)
        q_dec = jnp.broadcast_to(q * gamma, (SUBLANES, RET_DK)).astype(BF16)
        o = s * v + jnp.dot(q_dec, state.astype(BF16), preferred_element_type=F32)[0:1]
        k_col = jnp.sum(jnp.where(ii == jj, k, 0.0), axis=-1, keepdims=True)
        nst_ref[0, h] = state * gamma + k_col * v
        y = o * lax.rsqrt(jnp.mean(o * o, axis=-1, keepdims=True) + NORM_EPS) * rn_ref[h]
        gate = g_ref[0, h:h + 1, :]
        o_ref[0, h:h + 1, :] = (gate * jax.nn.sigmoid(gate) * y).astype(o_ref.dtype)


def ret_decode(q, k, v, g, rnorm, state, log_g):
    db = q.shape[0]
    return pl.pallas_call(
        _ret_decode_kernel,
        grid=(db,),
        in_specs=[pl.BlockSpec(memory_space=pltpu.SMEM),
                  pl.BlockSpec((1, RET_H, RET_DK), lambda b: (b, 0, 0)),
                  pl.BlockSpec((1, RET_H, RET_DK), lambda b: (b, 0, 0)),
                  pl.BlockSpec((1, RET_H, RET_DV), lambda b: (b, 0, 0)),
                  pl.BlockSpec((1, RET_H, RET_DV), lambda b: (b, 0, 0)),
                  pl.BlockSpec((RET_H, 1, RET_DV), lambda b: (0, 0, 0)),
                  pl.BlockSpec((1, RET_H, RET_DK, RET_DV), lambda b: (b, 0, 0, 0))],
        out_specs=[pl.BlockSpec((1, RET_H, RET_DV), lambda b: (b, 0, 0)),
                   pl.BlockSpec((1, RET_H, RET_DK, RET_DV), lambda b: (b, 0, 0, 0))],
        out_shape=[jax.ShapeDtypeStruct((db, RET_H, RET_DV), BF16),
                   jax.ShapeDtypeStruct(state.shape, F32)],
        compiler_params=_cparams(("parallel",)),
        name="ret_decode",
    )(log_g, q, k, v, g, rnorm.reshape(RET_H, 1, RET_DV), state)


def _alibi_slopes(n):
    return jnp.asarray([2.0 ** (-8.0 * (i + 1) / n) for i in range(n)], dtype=F32)


def _even_layer(e, layer, hp, hs, batch, seq, caches, page_table, even_w_in, cmp_pos, lam_vec, dnorm, stacks):
    cache_cmp, cache_sel, cache_win, cache_diff = caches
    n_even = even_w_in.shape[0]
    d = even_w_in.shape[1]
    lam_init = 0.8 - 0.6 * math.exp(-0.3 * layer)
    db = page_table.shape[0]
    past = page_table.shape[1] * PAGE_SIZE
    nsa_slopes = _alibi_slopes(NSA_H)
    diff_slopes = _alibi_slopes(DIFF_H)
    kvw = 2 * NSA_KV_W
    per_group = NSA_G * 3
    w_gate = even_w_in[e, :, GATE_OFF:GATE_OFF + NSA_GATE_W].reshape(d, NSA_KVH, per_group)
    w_gate = jnp.pad(w_gate, ((0, 0), (0, 0), (0, LANES - per_group))).reshape(d, NSA_KVH * LANES).astype(BF16)
    w_tail = even_w_in[e, :, GATE_OFF + NSA_GATE_W:].astype(BF16)

    qn_bf, qn_s = dense(hp, even_w_in, NSA_Q_W, lead=e, out_dtypes=(BF16,), xs=hs)
    (cmp_st, cmp_bf), (new_cmp, _) = dense(hp, even_w_in, kvw, col0=NSA_Q_W, lead=e, out_dtypes=(F32, BF16),
                                           stack=(n_even, e, stacks[0]), xs=hs)
    (sel_st, sel_bf), (new_sel, _) = dense(hp, even_w_in, kvw, col0=NSA_Q_W + kvw, lead=e,
                                           out_dtypes=(F32, BF16), stack=(n_even, e, stacks[1]), xs=hs)
    (win_f, win_bf), (new_win, _) = dense(hp, even_w_in, kvw, col0=NSA_Q_W + 2 * kvw, lead=e,
                                          out_dtypes=(F32, BF16), xs=hs)
    gate_pre, gate_s = dense(hp, w_gate, NSA_KVH * LANES, xs=hs)
    qd_bf, qd_s = dense(hp, w_tail, DIFF_Q_W, out_dtypes=(BF16,), xs=hs)
    (kvd_st, kvd_bf), (new_diff, _) = dense(hp, w_tail, DIFF_Q_W + DIFF_V_W, col0=DIFF_Q_W,
                                            out_dtypes=(F32, BF16), stack=(n_even, e, stacks[2]), xs=hs)
    qn_s, new_cmp, new_sel, new_win, gate_s, qd_s, new_diff = (
        t[:db] for t in (qn_s, new_cmp, new_sel, new_win, gate_s, qd_s, new_diff))

    pos_cg = jnp.transpose(cmp_pos, (0, 2, 1)).reshape(2 * NSA_KVH, CMP_BLOCK)
    kcvc = compress_prompt(cmp_bf, jnp.tile(pos_cg, (1, seq // CMP_BLOCK)), batch, seq)
    o_nsa = nsa_prompt(qn_bf, gate_pre, kcvc, sel_bf, win_bf, nsa_slopes, batch, seq)
    o_diff = diff_prompt(qd_bf, kvd_bf, lam_vec, dnorm, diff_slopes, lam_init, batch, seq)
    mix_p = jnp.concatenate([o_nsa, o_diff], axis=1)
    keep_p = min(WINDOW, seq)
    win_p = win_f.reshape(batch, seq, kvw)[:, seq - keep_p:].reshape(batch, keep_p, 2, NSA_KVH, HEAD_DIM)

    gates8 = gate_s.reshape(db, NSA_KVH, LANES)[:, :, :per_group].reshape(db, NSA_H, 3)
    gates8 = jnp.pad(gates8, ((0, 0), (0, 0), (0, LANES - 3)))
    group_of_head = jnp.asarray(np.arange(NSA_H) // NSA_G)
    onehot_g = jax.nn.one_hot(group_of_head, NSA_KVH, dtype=BF16)
    qn = qn_s.reshape(db, NSA_H, HEAD_DIM)
    q8 = (qn[:, :, None, :] * onehot_g[None, :, :, None]).reshape(db, NSA_H, NSA_KVH * HEAD_DIM)
    slope8 = jnp.broadcast_to(nsa_slopes[:, None], (NSA_H, LANES))
    pos_rg = jnp.tile(jnp.transpose(cmp_pos, (0, 2, 1)), (1, SUBLANES // NSA_KVH, PAGE_SIZE // CMP_BLOCK))
    pos_rows = jnp.repeat(pos_rg, NSA_KVH, axis=2)
    kcvc_s = compress_sample(cache_cmp, e, page_table, pos_rows)
    w_buf = cache_win.shape[2]
    win_all = jnp.concatenate([cache_win[e].reshape(db, w_buf, -1), new_win[:, None, :]], axis=1)
    n_win = w_buf + 1
    win_pad = jnp.pad(win_all, ((0, 0), (0, -n_win % LANES), (0, 0)))
    o_cmp, o_win, idx = nsa_decode_cmp(q8, slope8, kcvc_s, win_pad, past, n_win)
    ids = jnp.concatenate([idx[:, 0, :TOP_N - 1], idx[:, NSA_G, :TOP_N - 1]], axis=1)
    o_nsa_s = nsa_decode_sel(cache_sel, e, page_table, ids, q8, slope8, new_sel[:, None, :], o_cmp, o_win, gates8,
                             past)

    qd = qd_s.reshape(db, DIFF_H, 2, HEAD_DIM)
    eye_c = jnp.eye(2, dtype=BF16)
    q8d = jnp.einsum('bhcd,cj->bchjd', qd, eye_c).reshape(db, 2 * DIFF_H, DIFF_VD)
    slope8d = jnp.broadcast_to(jnp.tile(diff_slopes, 2)[:, None], (2 * DIFF_H, LANES))
    o_diff_s = diff_decode(cache_diff, e, page_table, q8d, slope8d, new_diff.reshape(db, 2, DIFF_H, DIFF_VD),
                           lam_vec, dnorm, lam_init, past)
    mix_s = jnp.concatenate([o_nsa_s.reshape(db, NSA_Q_W), o_diff_s.reshape(db, DIFF_V_W)], axis=1)
    mix_s = jnp.pad(mix_s, ((0, hs.shape[0] - db), (0, 0)))
    kv_s = (db, 1, 2, NSA_KVH, HEAD_DIM)
    keep = min(WINDOW, n_win)
    small = (win_p, new_cmp.reshape(kv_s), new_sel.reshape(kv_s),
             win_all[:, n_win - keep:].reshape(db, keep, 2, NSA_KVH, HEAD_DIM),
             new_diff.reshape(db, 1, 2, DIFF_H, DIFF_VD))
    return mix_p, mix_s, (cmp_st, sel_st, kvd_st), small


def _ret_layer(o, hp, hs, batch, seq, state, ret_w_in, rnorm):
    db = state.shape[0]
    log_g = jnp.log1p(-jnp.exp2(-5.0 - jnp.arange(RET_H, dtype=F32)))
    n_in = ret_w_in.shape[2]
    proj, ps = dense(hp, ret_w_in, n_in, lead=o, xs=hs)
    ps = ps[:db]
    gated_p, st_p = ret_prompt(proj, rnorm, log_g, batch, seq)
    hk = RET_H * RET_DK
    hv = RET_H * RET_DV
    q = ps[:, :hk].reshape(db, RET_H, RET_DK)
    k = ps[:, hk:2 * hk].reshape(db, RET_H, RET_DK)
    v = ps[:, 2 * hk:2 * hk + hv].reshape(db, RET_H, RET_DV)
    g = ps[:, 2 * hk + hv:].reshape(db, RET_H, RET_DV)
    gated_s, st_s = ret_decode(q, k, v, g, rnorm, state, log_g)
    return gated_p, jnp.pad(gated_s.reshape(db, hv), ((0, hs.shape[0] - db), (0, 0))), st_p, st_s


def kernel(x_prompt, x_sample, cache_nsa_cmp, cache_nsa_sel, cache_nsa_win, cache_diff, state_ret, page_table,
           norm_mix, norm_ffn, norm_final, even_w_in, even_w_out, nsa_cmp_pos, diff_lambda, diff_norm,
           ret_w_in, ret_norm, ret_w_out, ffn_w13, ffn_w2):
    batch, seq, d = x_prompt.shape
    db = x_sample.shape[0]
    d_ff = ffn_w2.shape[1]
    xp = x_prompt.reshape(batch * seq, d)
    xs = jnp.pad(x_sample.reshape(db, d), ((0, -db % SAMPLE_ROWS), (0, 0)))
    caches = (cache_nsa_cmp, cache_nsa_sel, cache_nsa_win, cache_diff)
    stacks = (None, None, None)
    small = [[] for _ in range(5)]
    ret_p, ret_s = [], []
    for layer in range(DEPTH):
        hp = rmsnorm(xp, norm_mix[layer], BF16)
        hs = rmsnorm(xs, norm_mix[layer], BF16)
        if layer % 2 == 0:
            e = layer // 2
            mix_p, mix_s, stacks, small_e = _even_layer(
                e, layer, hp, hs, batch, seq, caches, page_table, even_w_in, nsa_cmp_pos[e], diff_lambda[e],
                diff_norm[e], stacks)
            for lst, item in zip(small, small_e):
                lst.append(item)
            w_out, lead = even_w_out, e
        else:
            o = layer // 2
            mix_p, mix_s, st_p, st_s = _ret_layer(o, hp, hs, batch, seq, state_ret[o], ret_w_in, ret_norm[o])
            ret_p.append(st_p)
            ret_s.append(st_s)
            w_out, lead = ret_w_out, o
        xp, xs = dense(mix_p, w_out, d, lead=lead, res=xp, xs=mix_s, res_s=xs)
        up_p, up_s = dense(rmsnorm(xp, norm_ffn[layer], BF16), ffn_w13, d_ff, lead=layer, col3=d_ff,
                           out_dtypes=(BF16,), xs=rmsnorm(xs, norm_ffn[layer], BF16))
        xp, xs = dense(up_p, ffn_w2, d, lead=layer, res=xp, xs=up_s, res_s=xs)
    y_prompt = rmsnorm(xp, norm_final, F32).reshape(batch, seq, d)
    y_sample = rmsnorm(xs, norm_final, F32)[:db].reshape(db, 1, d)
    n_even = even_w_in.shape[0]
    cmp_st, sel_st, kvd_st = stacks
    win_p, cmp_s, sel_s, win_s, diff_s = [jnp.stack(t) for t in small]
    return (y_prompt, y_sample,
            cmp_st.reshape(n_even, batch, seq, 2, NSA_KVH, HEAD_DIM),
            sel_st.reshape(n_even, batch, seq, 2, NSA_KVH, HEAD_DIM), win_p,
            kvd_st.reshape(n_even, batch, seq, 2, DIFF_H, DIFF_VD), jnp.stack(ret_p),
            cmp_s, sel_s, win_s, diff_s, jnp.stack(ret_s))
```

```python
import functools
import math

import jax
import jax.numpy as jnp
import numpy as np
from jax import lax
from jax.experimental import pallas as pl
from jax.experimental.pallas import tpu as pltpu

F32 = jnp.float32
BF16 = jnp.bfloat16

D_MODEL = 2048
DEPTH = 4
PAGE_SIZE = 128
HEAD_DIM = 128
NSA_H = 8
NSA_KVH = 2
NSA_G = NSA_H // NSA_KVH
CMP_BLOCK = 64
CMP_SHIFT = CMP_BLOCK.bit_length() - 1
TOP_N = 16
WINDOW = 512
FORCE_SCORE = 1.0e4
DIFF_H = 4
DIFF_VD = 2 * HEAD_DIM
RET_H = 8
RET_DK = D_MODEL // RET_H
RET_DV = 2 * D_MODEL // RET_H
RET_CHUNK = 128
D_FF = ((8 * D_MODEL + 3 * 256 - 1) // (3 * 256)) * 256
NORM_EPS = 1e-6
NEG_INF = -1e30
ATT_SCALE = HEAD_DIM ** -0.5
LOG2E = math.log2(math.e)

NSA_Q_W = NSA_H * HEAD_DIM
NSA_KV_W = NSA_KVH * HEAD_DIM
NSA_GATE_W = NSA_H * 3
DIFF_Q_W = DIFF_H * 2 * HEAD_DIM
DIFF_V_W = DIFF_H * DIFF_VD
MAIN_W = NSA_Q_W + 6 * NSA_KV_W + 2 * DIFF_Q_W + DIFF_V_W
GATE_OFF = NSA_Q_W + 6 * NSA_KV_W

NSA_TQ, NSA_TK = 256, 256
DIFF_TQ, DIFF_TK = 512, 512
RET_BLOCK = 512

LANES = 128
SUBLANES = 8
SAMPLE_ROWS = 2 * SUBLANES
VMEM_LIMIT = 56 * 1024 * 1024


def _cparams(sem):
    return pltpu.CompilerParams(dimension_semantics=sem, vmem_limit_bytes=VMEM_LIMIT)


def _rmsnorm_kernel(x_ref, g_ref, o_ref):
    x = x_ref[...]
    y = x * lax.rsqrt(jnp.mean(x * x, axis=-1, keepdims=True) + NORM_EPS)
    o_ref[...] = (y * g_ref[...]).astype(o_ref.dtype)


def rmsnorm(x, g, out_dtype):
    m, d = x.shape
    tm = min(512, m)
    return pl.pallas_call(
        _rmsnorm_kernel,
        grid=(m // tm,),
        in_specs=[pl.BlockSpec((tm, d), lambda i: (i, 0)), pl.BlockSpec((1, d), lambda i: (0, 0))],
        out_specs=pl.BlockSpec((tm, d), lambda i: (i, 0)),
        out_shape=jax.ShapeDtypeStruct((m, d), out_dtype),
        compiler_params=_cparams(("parallel",)),
        name="rmsnorm",
    )(x, g.reshape(1, d))


def _dense_kernel(*refs, n_x, n_w, has_gain, row_norm, has_res, has_prev, has_s, n_out, cast_w, n_i):
    refs = list(refs)
    x_refs = [refs.pop(0) for _ in range(n_x)]
    xs_refs = [refs.pop(0) for _ in range(n_x)] if has_s else []
    w_refs = [refs.pop(0) for _ in range(n_w)]
    gain_ref = refs.pop(0) if has_gain else None
    res_ref = refs.pop(0) if has_res else None
    res_s_ref = refs.pop(0) if (has_res and has_s) else None
    if has_prev:
        refs.pop(0)
    out_refs = [refs.pop(0) for _ in range(n_out)]
    out_s_refs = [refs.pop(0) for _ in range(n_out)] if has_s else []
    wb_refs = [refs.pop(0) for _ in range(n_w)] if cast_w else []
    r_ref, rs_ref = (refs.pop(0), refs.pop(0)) if row_norm else (None, None)
    j = pl.program_id(0)
    i = pl.program_id(1)
    if cast_w:
        @pl.when(i == 0)
        def _():
            for w_ref, wb_ref in zip(w_refs, wb_refs):
                w = w_ref[...]
                wb_ref[...] = (w * gain_ref[...] if has_gain else w).astype(BF16)
        w_refs = wb_refs

    def rms_factor(pieces):
        sq = sum(jnp.sum(p.astype(F32) ** 2, axis=-1, keepdims=True) for p in pieces)
        width = sum(p.shape[-1] for p in pieces)
        return lax.rsqrt(sq / width + NORM_EPS) + jnp.zeros((pieces[0].shape[0], LANES), F32)

    if row_norm:
        @pl.when(j == 0)
        def _():
            r_ref[i] = rms_factor([x_ref[...] for x_ref in x_refs])
            if has_s:
                @pl.when(i == n_i - 1)
                def _():
                    rs_ref[...] = rms_factor([xs_ref[...] for xs_ref in xs_refs])

    def product(pieces, scale):
        def one(w_ref):
            k0, acc = 0, None
            for p in pieces:
                part = jnp.dot(p, w_ref[k0:k0 + p.shape[-1], :], preferred_element_type=F32)
                acc = part if acc is None else acc + part
                k0 += p.shape[-1]
            return acc if scale is None else acc * _lane_tile(scale, acc.shape[-1])
        acc = one(w_refs[0])
        if n_w == 2:
            acc = acc * jax.nn.sigmoid(acc) * one(w_refs[1])
        return acc

    def emit(acc, r, dst_refs):
        if r is not None:
            acc = r[...] + acc
        for o_ref in dst_refs:
            o_ref[...] = acc.astype(o_ref.dtype)

    def plain():
        emit(product([x_ref[...] for x_ref in x_refs], r_ref[i] if row_norm else None), res_ref, out_refs)

    def with_sample_rows():
        tm = x_refs[0].shape[0]
        pieces = [jnp.concatenate([x_ref[...], xs_ref[...]], axis=0) for x_ref, xs_ref in zip(x_refs, xs_refs)]
        scale = jnp.concatenate([r_ref[i], rs_ref[...]], axis=0) if row_norm else None
        acc = product(pieces, scale)
        emit(acc[:tm], res_ref, out_refs)
        emit(acc[tm:], res_s_ref, out_s_refs)

    if has_s:
        pl.when(i < n_i - 1)(plain)
        pl.when(i == n_i - 1)(with_sample_rows)
    else:
        plain()


def _dense_tiles(m, k, n, n_w):
    tn = min(1024 if (k * n_w <= 2048 and n % 1024 == 0) else 512, n)
    tm = min(1024 if k * n_w <= 4096 else 512, m)
    return tm, tn


def dense(x, w, n, col0=0, lead=None, col3=None, res=None, out_dtypes=(F32,), stack=None, xs=None, res_s=None,
          gain=None, row_norm=False):
    xparts = tuple(x) if isinstance(x, (tuple, list)) else (x,)
    sparts = (tuple(xs) if isinstance(xs, (tuple, list)) else (xs,)) if xs is not None else ()
    m = xparts[0].shape[0]
    k = sum(p.shape[1] for p in xparts)
    n_w = 1 if col3 is None else 2
    tm, tn = _dense_tiles(m, k, n, n_w)
    assert m % tm == 0 and n % tn == 0 and col0 % tn == 0 and (col3 is None or col3 % tn == 0)
    cast_w = w.dtype != BF16
    assert gain is None or (cast_w and row_norm)
    has_s = xs is not None
    n_i = m // tm

    def w_spec(c0):
        if w.ndim == 3:
            return pl.BlockSpec((None, k, tn), lambda j, i: (lead, 0, c0 // tn + j))
        return pl.BlockSpec((k, tn), lambda j, i: (0, c0 // tn + j))

    in_specs = [pl.BlockSpec((tm, p.shape[1]), lambda j, i: (i, 0)) for p in xparts]
    args = list(xparts)
    if has_s:
        ms = sparts[0].shape[0]
        in_specs += [pl.BlockSpec((ms, p.shape[1]), lambda j, i: (0, 0)) for p in sparts]
        args += list(sparts)
    in_specs.append(w_spec(col0))
    args.append(w)
    if n_w == 2:
        in_specs.append(w_spec(col3))
        args.append(w)
    if gain is not None:
        in_specs.append(pl.BlockSpec((k, 1), lambda j, i: (0, 0)))
        args.append(gain.reshape(k, 1))
    if res is not None:
        in_specs.append(pl.BlockSpec((tm, tn), lambda j, i: (i, j)))
        args.append(res)
        if has_s:
            in_specs.append(pl.BlockSpec((ms, tn), lambda j, i: (0, j)))
            args.append(res_s)
    out_specs = [pl.BlockSpec((tm, tn), lambda j, i: (i, j)) for _ in out_dtypes]
    out_shape = [jax.ShapeDtypeStruct((m, n), dt) for dt in out_dtypes]
    aliases = {}
    has_prev = False
    if stack is not None:
        depth, slot, prev = stack
        out_specs[0] = pl.BlockSpec((None, tm, tn), lambda j, i: (slot, i, j))
        out_shape[0] = jax.ShapeDtypeStruct((depth, m, n), out_dtypes[0])
        if prev is not None:
            has_prev = True
            aliases = {len(args): 0}
            in_specs.append(pl.BlockSpec(memory_space=pl.ANY))
            args.append(prev)
    if has_s:
        out_specs += [pl.BlockSpec((ms, tn), lambda j, i: (0, j)) for _ in out_dtypes]
        out_shape += [jax.ShapeDtypeStruct((ms, n), dt) for dt in out_dtypes]
    scratch = [pltpu.VMEM((k, tn), BF16)] * (n_w if cast_w else 0)
    if row_norm:
        scratch += [pltpu.VMEM((n_i, tm, LANES), F32), pltpu.VMEM((ms if has_s else SUBLANES, LANES), F32)]
    outs = pl.pallas_call(
        functools.partial(_dense_kernel, n_x=len(xparts), n_w=n_w, has_gain=gain is not None, row_norm=row_norm,
                          has_res=res is not None, has_prev=has_prev, has_s=has_s, n_out=len(out_dtypes),
                          cast_w=cast_w, n_i=n_i),
        grid=(n // tn, n_i),
        in_specs=in_specs,
        out_specs=out_specs,
        out_shape=out_shape,
        scratch_shapes=scratch,
        input_output_aliases=aliases,
        compiler_params=_cparams(("arbitrary", "arbitrary")),
        name="dense",
    )(*args)
    n_out = len(out_dtypes)
    unwrap = lambda t: t[0] if n_out == 1 else tuple(t)
    if has_s:
        return unwrap(outs[:n_out]), unwrap(outs[n_out:])
    return unwrap(outs)


def _softmax_rows64(logits):
    e = jnp.exp(logits - jnp.max(logits, axis=-1, keepdims=True))
    reps = logits.shape[-1] // CMP_BLOCK
    return e * (reps / jnp.sum(e, axis=-1, keepdims=True))


def _compress_prompt_kernel(x_ref, pos_ref, o_ref, *, seq):
    w = _softmax_rows64(pos_ref[...])
    blk = lax.broadcasted_iota(jnp.int32, (LANES, seq), 0)
    key_blk = lax.broadcasted_iota(jnp.int32, (LANES, seq), 1) >> CMP_SHIFT
    onblk = blk == key_blk
    for c in range(4):
        wb = jnp.where(onblk, w[c:c + 1, :], 0.0).astype(BF16)
        o_ref[0, :, c * LANES:(c + 1) * LANES] = jnp.dot(
            wb, x_ref[:, c * LANES:(c + 1) * LANES], preferred_element_type=F32).astype(BF16)


def compress_prompt(cmp_bf, pos_tiled, batch, seq):
    return pl.pallas_call(
        functools.partial(_compress_prompt_kernel, seq=seq),
        grid=(batch,),
        in_specs=[pl.BlockSpec((seq, 4 * LANES), lambda b: (b, 0)),
                  pl.BlockSpec((4, seq), lambda b: (0, 0))],
        out_specs=pl.BlockSpec((1, LANES, 4 * LANES), lambda b: (b, 0, 0)),
        out_shape=jax.ShapeDtypeStruct((batch, LANES, 4 * LANES), BF16),
        compiler_params=_cparams(("parallel",)),
        name="nsa_compress_prompt",
    )(cmp_bf, pos_tiled)


def _lane_tile(x, width):
    return x if width == LANES else jnp.concatenate([x] * (width // LANES), axis=-1)


def _flash_step(s, v, m_ref, l_ref, acc_ref, idx):
    ss, vs = (s, v) if isinstance(s, (list, tuple)) else ([s], [v])
    tiles = [[t[:, j * LANES:(j + 1) * LANES] for j in range(t.shape[1] // LANES)] for t in ss]
    blocks = [blk for tile in tiles for blk in tile]
    mx = blocks[0]
    for blk in blocks[1:]:
        mx = jnp.maximum(mx, blk)
    m_prev = m_ref[idx]
    m_new = jnp.maximum(m_prev, jnp.max(mx, axis=-1, keepdims=True))
    alpha = jnp.exp2(m_prev - m_new)
    ps = [[jnp.exp2(blk - m_new) for blk in tile] for tile in tiles]
    row_sum = None
    for tile in ps:
        for p in tile:
            row_sum = p if row_sum is None else row_sum + p
    l_ref[idx] = alpha * l_ref[idx] + jnp.sum(row_sum, axis=-1, keepdims=True)
    m_ref[idx] = m_new
    acc = _lane_tile(alpha, vs[0].shape[-1]) * acc_ref[idx]
    for tile, vt in zip(ps, vs):
        acc = acc + jnp.dot(jnp.concatenate([p.astype(BF16) for p in tile], axis=-1), vt,
                            preferred_element_type=F32)
    acc_ref[idx] = acc


def _flash_init(m_ref, l_ref, acc_ref):
    m_ref[...] = jnp.full(m_ref.shape, NEG_INF, F32)
    l_ref[...] = jnp.zeros(l_ref.shape, F32)
    acc_ref[...] = jnp.zeros(acc_ref.shape, F32)


def _flash_result(l_ref, acc_ref, idx):
    return acc_ref[idx] / _lane_tile(l_ref[idx], acc_ref.shape[-1])


def _softmax_pv(s, v):
    p = jnp.exp2(s - jnp.max(s, axis=-1, keepdims=True))
    l = jnp.sum(p, axis=-1, keepdims=True)
    return jnp.dot(p.astype(BF16), v, preferred_element_type=F32) / l


def _nsa_prompt_kernel(slopes_ref, q_ref, gate_ref, kc_ref, vc_ref, ks_ref, vs_ref, kw_ref, vw_ref,
                       o_ref, m_ref, l_ref, acc_ref, flag_ref, *, tq, tk, n_blocks):
    g = pl.program_id(1)
    qi = pl.program_id(2)
    q0 = qi * tq
    row = lax.broadcasted_iota(jnp.int32, (tq, LANES), 0)
    col = lax.broadcasted_iota(jnp.int32, (tq, LANES), 1)
    qpos = q0 + row
    nt = (((1,), (1,)), ((), ()))

    kc = kc_ref[0]
    vc = vc_ref[0]
    cdist = qpos - (col * CMP_BLOCK + (CMP_BLOCK - 1))
    cmask = cdist >= 0
    cdist_f = cdist.astype(F32)
    imp = jnp.zeros((tq, LANES), F32)
    o_cmp = []
    for h in range(NSA_G):
        slope = slopes_ref[g * NSA_G + h]
        qh = q_ref[:, h * HEAD_DIM:(h + 1) * HEAD_DIM]
        s = lax.dot_general(qh, kc, nt, preferred_element_type=F32) * ATT_SCALE - slope * cdist_f
        s = jnp.where(cmask, s, NEG_INF)
        p = jnp.exp(s - jnp.max(s, axis=-1, keepdims=True))
        p = jnp.where(cmask, p / jnp.sum(p, axis=-1, keepdims=True), 0.0)
        imp = imp + p
        o_cmp.append(jnp.dot(p.astype(BF16), vc, preferred_element_type=F32))

    imp = jnp.where(cmask, imp, -1.0)
    forced = (col == (qpos >> CMP_SHIFT)) | (col == 0)
    imp = jnp.where(forced, FORCE_SCORE, imp)
    nblk = -(-n_blocks // SUBLANES) * SUBLANES
    imp_t = imp.T[:nblk]
    sub = lax.broadcasted_iota(jnp.int32, (SUBLANES, tq), 0)
    sel_rows = []
    for vi in range(nblk // SUBLANES):
        x = imp_t[vi * SUBLANES:(vi + 1) * SUBLANES]
        rank = jnp.zeros((SUBLANES, tq), F32)
        for j in range(nblk):
            r = imp_t[j:j + 1, :]
            jv, jr = divmod(j, SUBLANES)
            ge = jnp.where(r >= x, 1.0, 0.0)
            gt = jnp.where(r > x, 1.0, 0.0)
            if jv < vi:
                rank = rank + ge
            elif jv > vi:
                rank = rank + gt
            else:
                rank = rank + jnp.where(sub > jr, ge, gt)
        sel_rows.append(jnp.where(rank < TOP_N, 1.0, 0.0))
    sel_rows.append(jnp.zeros((LANES - nblk, tq), F32))
    sel_f = jnp.concatenate(sel_rows, axis=0).T
    sel = sel_f.astype(BF16)
    picked_any = jnp.max(sel_f, axis=0, keepdims=True)
    tile_of_block = lax.broadcasted_iota(jnp.int32, (1, LANES), 1) >> ((tk // CMP_BLOCK).bit_length() - 1)
    for kt in range(n_blocks * CMP_BLOCK // tk):
        hit = jnp.max(jnp.where(tile_of_block == kt, picked_any, 0.0))
        flag_ref[kt] = (hit > 0.5).astype(jnp.int32)

    c1 = ATT_SCALE * LOG2E
    slope2 = [slopes_ref[g * NSA_G + h] * LOG2E for h in range(NSA_G)]

    _flash_init(m_ref, l_ref, acc_ref)
    e_row = lax.broadcasted_iota(jnp.int32, (LANES, tk), 0)
    e_col = lax.broadcasted_iota(jnp.int32, (LANES, tk), 1)
    rel = lax.broadcasted_iota(jnp.int32, (1, tk), 1)
    krow = lax.broadcasted_iota(jnp.int32, (tq, tk), 0)
    kcol = lax.broadcasted_iota(jnp.int32, (tq, tk), 1)

    def sel_tile(kt, diagonal):
        k0 = pl.multiple_of(kt * tk, tk)
        k = ks_ref[pl.ds(k0, tk), :]
        v = vs_ref[pl.ds(k0, tk), :]
        expand = jnp.where(((k0 + e_col) >> CMP_SHIFT) == e_row, 1.0, 0.0).astype(BF16)
        picked = jnp.dot(sel, expand, preferred_element_type=F32)
        if diagonal:
            picked = jnp.where(k0 + kcol <= q0 + krow, picked, 0.0)
        valid = picked > 0.5
        bias = (k0 - q0 + rel).astype(F32)
        for h in range(NSA_G):
            qh = q_ref[:, h * HEAD_DIM:(h + 1) * HEAD_DIM]
            s = lax.dot_general(qh, k, nt, preferred_element_type=F32) * c1 + slope2[h] * bias
            _flash_step(jnp.where(valid, s, NEG_INF), v, m_ref, l_ref, acc_ref, h)

    kd = q0 // tk
    sel_tile(kd, True)

    def sel_body(i, carry):
        kt = kd - 1 - i
        pl.when(flag_ref[kt] > 0)(lambda: sel_tile(kt, False))
        return carry

    lax.fori_loop(0, kd, sel_body, 0)
    o_sel = [_flash_result(l_ref, acc_ref, h) for h in range(NSA_G)]

    n_band = WINDOW + tq
    w0 = pl.multiple_of(jnp.maximum(q0 - WINDOW, 0), LANES)
    kw = kw_ref[pl.ds(w0, n_band), :]
    vw = vw_ref[pl.ds(w0, n_band), :]
    wdist = (q0 + lax.broadcasted_iota(jnp.int32, (tq, n_band), 0)) - (
        w0 + lax.broadcasted_iota(jnp.int32, (tq, n_band), 1))
    wvalid = (wdist >= 0) & (wdist <= WINDOW)
    wbias = (w0 - q0 + lax.broadcasted_iota(jnp.int32, (1, n_band), 1)).astype(F32)
    o_win = []
    for h in range(NSA_G):
        qh = q_ref[:, h * HEAD_DIM:(h + 1) * HEAD_DIM]
        s = lax.dot_general(qh, kw, nt, preferred_element_type=F32) * c1 + slope2[h] * wbias
        o_win.append(_softmax_pv(jnp.where(wvalid, s, NEG_INF), vw))

    gates = jax.nn.sigmoid(gate_ref[...])
    for h in range(NSA_G):
        o = (gates[:, 3 * h:3 * h + 1] * o_cmp[h] + gates[:, 3 * h + 1:3 * h + 2] * o_sel[h]
             + gates[:, 3 * h + 2:3 * h + 3] * o_win[h])
        o_ref[:, h * HEAD_DIM:(h + 1) * HEAD_DIM] = o.astype(o_ref.dtype)


def nsa_prompt(qn_bf, gate_pre, kcvc, sel_bf, win_bf, slopes, batch, seq, tq=NSA_TQ, tk=NSA_TK):
    assert tk % tq == 0 and seq % tk == 0 and seq >= WINDOW + tq and seq // CMP_BLOCK <= LANES
    nq = seq // tq
    gw = NSA_G * HEAD_DIM

    def kv_spec(kv):
        return pl.BlockSpec((seq, HEAD_DIM), lambda b, g, i: (b, NSA_KVH * kv + g))

    return pl.pallas_call(
        functools.partial(_nsa_prompt_kernel, tq=tq, tk=tk, n_blocks=seq // CMP_BLOCK),
        grid=(batch, NSA_KVH, nq),
        in_specs=[pl.BlockSpec(memory_space=pltpu.SMEM),
                  pl.BlockSpec((tq, gw), lambda b, g, i: (b * nq + i, g)),
                  pl.BlockSpec((tq, LANES), lambda b, g, i: (b * nq + i, g)),
                  pl.BlockSpec((1, LANES, HEAD_DIM), lambda b, g, i: (b, 0, g)),
                  pl.BlockSpec((1, LANES, HEAD_DIM), lambda b, g, i: (b, 0, 2 + g)),
                  kv_spec(0), kv_spec(1), kv_spec(0), kv_spec(1)],
        out_specs=pl.BlockSpec((tq, gw), lambda b, g, i: (b * nq + i, g)),
        out_shape=jax.ShapeDtypeStruct((batch * seq, NSA_Q_W), BF16),
        scratch_shapes=[pltpu.VMEM((NSA_G, tq, LANES), F32), pltpu.VMEM((NSA_G, tq, LANES), F32),
                        pltpu.VMEM((NSA_G, tq, HEAD_DIM), F32), pltpu.SMEM((seq // tk,), jnp.int32)],
        compiler_params=_cparams(("parallel", "parallel", "arbitrary")),
        name="nsa_prompt",
    )(slopes, qn_bf, gate_pre, kcvc, kcvc, sel_bf, sel_bf, win_bf, win_bf)


def _diff_lambda(lv, lam_init):
    a = jnp.sum(lv[0:1] * lv[1:2], axis=-1, keepdims=True)
    b = jnp.sum(lv[2:3] * lv[3:4], axis=-1, keepdims=True)
    return jnp.exp(a) - jnp.exp(b) + lam_init


def _diff_prompt_kernel(slopes_ref, q_ref, k_ref, v_ref, lam_ref, dn_ref, o_ref, m_ref, l_ref, acc_ref,
                        *, tq, tk, lam_init):
    h = pl.program_id(1)
    qi = pl.program_id(2)
    q0 = qi * tq
    krow = lax.broadcasted_iota(jnp.int32, (tq, tk), 0)
    kcol = lax.broadcasted_iota(jnp.int32, (tq, tk), 1)
    rel = lax.broadcasted_iota(jnp.int32, (1, tk), 1)
    c1 = ATT_SCALE * LOG2E
    slope2 = slopes_ref[h] * LOG2E
    nt = (((1,), (1,)), ((), ()))
    _flash_init(m_ref, l_ref, acc_ref)

    def tile(kt, diagonal):
        k0 = pl.multiple_of(kt * tk, tk)
        k = k_ref[pl.ds(k0, tk), :]
        v = v_ref[pl.ds(k0, tk), :]
        bias = slope2 * (k0 - q0 + rel).astype(F32)
        for c in range(2):
            s = lax.dot_general(q_ref[:, c * HEAD_DIM:(c + 1) * HEAD_DIM], k[:, c * HEAD_DIM:(c + 1) * HEAD_DIM],
                                nt, preferred_element_type=F32) * c1 + bias
            if diagonal:
                s = jnp.where(k0 + kcol <= q0 + krow, s, NEG_INF)
            _flash_step(s, v, m_ref, l_ref, acc_ref, c)

    kd = q0 // tk
    tile(kd, True)

    def body(i, carry):
        tile(kd - 1 - i, False)
        return carry

    lax.fori_loop(0, kd, body, 0)
    lam = _diff_lambda(lam_ref[...], lam_init)
    o = _flash_result(l_ref, acc_ref, 0) - lam * _flash_result(l_ref, acc_ref, 1)
    y = o * lax.rsqrt(jnp.mean(o * o, axis=-1, keepdims=True) + NORM_EPS)
    o_ref[...] = (y * dn_ref[...] * (1.0 - lam_init)).astype(o_ref.dtype)


def diff_prompt(qd_bf, kvd_bf, lam_vec, dnorm, slopes, lam_init, batch, seq, tq=DIFF_TQ, tk=DIFF_TK):
    assert tk % tq == 0 and seq % tk == 0
    nq = seq // tq
    w = 2 * HEAD_DIM
    qb, kb, vb = 0, 0, DIFF_H
    return pl.pallas_call(
        functools.partial(_diff_prompt_kernel, tq=tq, tk=tk, lam_init=lam_init),
        grid=(batch, DIFF_H, nq),
        in_specs=[pl.BlockSpec(memory_space=pltpu.SMEM),
                  pl.BlockSpec((tq, w), lambda b, h, i: (b * nq + i, qb + h)),
                  pl.BlockSpec((seq, w), lambda b, h, i: (b, kb + h)),
                  pl.BlockSpec((seq, w), lambda b, h, i: (b, vb + h)),
                  pl.BlockSpec((4, HEAD_DIM), lambda b, h, i: (0, 0)),
                  pl.BlockSpec((1, DIFF_VD), lambda b, h, i: (0, 0))],
        out_specs=pl.BlockSpec((tq, w), lambda b, h, i: (b * nq + i, h)),
        out_shape=jax.ShapeDtypeStruct((batch * seq, DIFF_V_W), BF16),
        scratch_shapes=[pltpu.VMEM((2, tq, LANES), F32), pltpu.VMEM((2, tq, LANES), F32),
                        pltpu.VMEM((2, tq, DIFF_VD), F32)],
        compiler_params=_cparams(("parallel", "parallel", "arbitrary")),
        name="diff_prompt",
    )(slopes, qd_bf, kvd_bf, kvd_bf, lam_vec, dnorm.reshape(1, DIFF_VD))


def _ret_prompt_kernel(lg_ref, q_ref, k_ref, v_ref, g_ref, rn_ref, o_ref, st_ref, *, chunk):
    h = pl.program_id(1)
    c = pl.program_id(2)
    lg = lg_ref[h]

    @pl.when(c == 0)
    def _():
        st_ref[...] = jnp.zeros(st_ref.shape, F32)

    ii = lax.broadcasted_iota(jnp.int32, (chunk, chunk), 0)
    jj = lax.broadcasted_iota(jnp.int32, (chunk, chunk), 1)
    d = (ii - jj).astype(F32)
    decay = jnp.where(d >= 0, jnp.exp(lg * jnp.maximum(d, 0.0)), 0.0)
    ik = lax.broadcasted_iota(jnp.int32, (chunk, RET_DK), 0).astype(F32)
    q_dec = jnp.exp(lg * (ik + 1.0))
    k_dec = jnp.exp(lg * (chunk - 1.0 - ik))
    ones = jnp.ones((1, 1), F32)
    g_chunk = jnp.exp(ones * (lg * chunk))

    q = q_ref[...]
    k = k_ref[...] * (RET_DK ** -0.5)
    v = v_ref[...].astype(BF16)
    state = st_ref[0, 0]
    s = lax.dot_general(q.astype(BF16), k.astype(BF16), (((1,), (1,)), ((), ())),
                        preferred_element_type=F32) * decay
    o = jnp.dot(s.astype(BF16), v, preferred_element_type=F32)
    o = o + jnp.dot((q * q_dec).astype(BF16), state.astype(BF16), preferred_element_type=F32)
    kv = lax.dot_general((k * k_dec).astype(BF16), v, (((0,), (0,)), ((), ())), preferred_element_type=F32)
    st_ref[0, 0] = state * g_chunk + kv

    y = o * lax.rsqrt(jnp.mean(o * o, axis=-1, keepdims=True) + NORM_EPS) * rn_ref[0]
    gate = g_ref[...]
    o_ref[...] = (gate * jax.nn.sigmoid(gate) * y).astype(o_ref.dtype)


def ret_prompt(proj, rnorm, log_g, batch, seq, chunk=RET_BLOCK):
    nc = seq // chunk
    kb = RET_H
    vb = 2 * RET_H * RET_DK // RET_DV
    gb = vb + RET_H
    return pl.pallas_call(
        functools.partial(_ret_prompt_kernel, chunk=chunk),
        grid=(batch, RET_H, nc),
        in_specs=[pl.BlockSpec(memory_space=pltpu.SMEM),
                  pl.BlockSpec((chunk, RET_DK), lambda b, h, c: (b * nc + c, h)),
                  pl.BlockSpec((chunk, RET_DK), lambda b, h, c: (b * nc + c, kb + h)),
                  pl.BlockSpec((chunk, RET_DV), lambda b, h, c: (b * nc + c, vb + h)),
                  pl.BlockSpec((chunk, RET_DV), lambda b, h, c: (b * nc + c, gb + h)),
                  pl.BlockSpec((1, 1, RET_DV), lambda b, h, c: (h, 0, 0))],
        out_specs=[pl.BlockSpec((chunk, RET_DV), lambda b, h, c: (b * nc + c, h)),
                   pl.BlockSpec((1, 1, RET_DK, RET_DV), lambda b, h, c: (b, h, 0, 0))],
        out_shape=[jax.ShapeDtypeStruct((batch * seq, RET_H * RET_DV), BF16),
                   jax.ShapeDtypeStruct((batch, RET_H, RET_DK, RET_DV), F32)],
        compiler_params=_cparams(("parallel", "parallel", "arbitrary")),
        name="ret_prompt",
    )(log_g, proj, proj, proj, proj, rnorm.reshape(RET_H, 1, RET_DV))


NT_DIMS = (((1,), (1,)), ((), ()))


def _compress_sample_kernel(pt_ref, pos_ref, *refs, pp):
    del pt_ref
    o_ref = refs[pp]
    per_page = PAGE_SIZE // CMP_BLOCK
    n_rows = PAGE_SIZE * NSA_KVH
    row = lax.broadcasted_iota(jnp.int32, (SUBLANES, n_rows), 0)
    col = lax.broadcasted_iota(jnp.int32, (SUBLANES, n_rows), 1)
    g_shift = NSA_KVH.bit_length() - 1
    member = ((col & (NSA_KVH - 1)) == (row & (NSA_KVH - 1))) & ((col >> (g_shift + CMP_SHIFT)) == (row >> g_shift))
    weights = []
    for kv in range(2):
        logits = jnp.where(member, pos_ref[kv], NEG_INF)
        e = jnp.exp(logits - jnp.max(logits, axis=-1, keepdims=True))
        weights.append((e / jnp.sum(e, axis=-1, keepdims=True)).astype(BF16))
    for i in range(pp):
        for kv in range(2):
            x = refs[i][0, 0, :, kv].reshape(n_rows, HEAD_DIM).astype(BF16)
            res = jnp.dot(weights[kv], x, preferred_element_type=F32)
            for half in range(per_page):
                for g in range(NSA_KVH):
                    c0 = (kv * NSA_KVH + g) * HEAD_DIM
                    o_ref[0, i * per_page + half:i * per_page + half + 1, c0:c0 + HEAD_DIM] = (
                        res[half * NSA_KVH + g:half * NSA_KVH + g + 1])


def compress_sample(cache, layer, page_table, pos_rows, pp=16):
    db, n_pages = page_table.shape
    per_page = PAGE_SIZE // CMP_BLOCK
    width = 2 * NSA_KVH * HEAD_DIM
    assert per_page * NSA_KVH <= SUBLANES

    def page_spec(i):
        return pl.BlockSpec((1, 1) + cache.shape[2:], lambda b, j, pt: (layer, pt[b, j * pp + i], 0, 0, 0, 0))

    return pl.pallas_call(
        functools.partial(_compress_sample_kernel, pp=pp),
        grid_spec=pltpu.PrefetchScalarGridSpec(
            num_scalar_prefetch=1,
            grid=(db, n_pages // pp),
            in_specs=[pl.BlockSpec(pos_rows.shape, lambda b, j, pt: (0, 0, 0))] + [page_spec(i) for i in range(pp)],
            out_specs=pl.BlockSpec((1, pp * per_page, width), lambda b, j, pt: (b, j, 0))),
        out_shape=jax.ShapeDtypeStruct((db, n_pages * per_page, width), F32),
        compiler_params=_cparams(("parallel", "arbitrary")),
        name="nsa_compress_sample",
    )(page_table, pos_rows, *([cache] * pp))


def _pick_group(x):
    row = lax.broadcasted_iota(jnp.int32, (NSA_H, HEAD_DIM), 0)
    return jnp.where(row < NSA_G, x[:, :HEAD_DIM], x[:, HEAD_DIM:])


def _nsa_decode_cmp_kernel(q_ref, slope_ref, kcvc_ref, win_ref, ocmp_ref, owin_ref, idx_ref, *, past, n_win):
    q8 = q_ref[0]
    kcvc = kcvc_ref[0]
    nb = kcvc.shape[0]
    kvw = NSA_KVH * HEAD_DIM
    lane = lax.broadcasted_iota(jnp.int32, (NSA_H, nb), 1)
    row = lax.broadcasted_iota(jnp.int32, (NSA_H, nb), 0)
    slope = slope_ref[...][:, :1]
    cdist = (past - (lane * CMP_BLOCK + CMP_BLOCK - 1)).astype(F32)
    s = lax.dot_general(q8, kcvc[:, :kvw].astype(BF16), NT_DIMS, preferred_element_type=F32) * ATT_SCALE
    s = s - slope * cdist
    p = jnp.exp(s - jnp.max(s, axis=-1, keepdims=True))
    p = p / jnp.sum(p, axis=-1, keepdims=True)
    ocmp_ref[0] = _pick_group(jnp.dot(p.astype(BF16), kcvc[:, kvw:].astype(BF16), preferred_element_type=F32))

    g0 = p[0:1] + p[1:2] + p[2:3] + p[3:4]
    g1 = p[4:5] + p[5:6] + p[6:7] + p[7:8]
    x = jnp.where(row < NSA_G, g0, g1)
    x = jnp.where(lane == 0, FORCE_SCORE, x)
    out_lane = lax.broadcasted_iota(jnp.int32, (NSA_H, LANES), 1)
    lane_f = lane.astype(F32)
    picked = jnp.zeros((NSA_H, LANES), F32)
    for t in range(TOP_N - 1):
        mx = jnp.max(x, axis=-1, keepdims=True)
        idx = jnp.min(jnp.where(x == mx, lane_f, float(nb)), axis=-1, keepdims=True)
        picked = jnp.where(out_lane == t, idx, picked)
        x = jnp.where(lane_f == idx, -2.0, x)
    idx_ref[0] = picked.astype(jnp.int32)

    win = win_ref[0]
    nw = win.shape[0]
    wl = lax.broadcasted_iota(jnp.int32, (NSA_H, nw), 1)
    wdist = (n_win - 1 - wl)
    valid = wdist >= 0
    s = lax.dot_general(q8, win[:, :kvw].astype(BF16), NT_DIMS, preferred_element_type=F32) * ATT_SCALE
    s = jnp.where(valid, s - slope * wdist.astype(F32), NEG_INF)
    p = jnp.exp(s - jnp.max(s, axis=-1, keepdims=True))
    p = p / jnp.sum(p, axis=-1, keepdims=True)
    owin_ref[0] = _pick_group(jnp.dot(p.astype(BF16), win[:, kvw:].astype(BF16), preferred_element_type=F32))


def nsa_decode_cmp(q8, slope8, kcvc, win_all, past, n_win):
    db = q8.shape[0]
    nb = kcvc.shape[1]
    nw = win_all.shape[1]
    w = kcvc.shape[2]
    head_out = jax.ShapeDtypeStruct((db, NSA_H, HEAD_DIM), F32)
    head_spec = pl.BlockSpec((1, NSA_H, HEAD_DIM), lambda b: (b, 0, 0))
    return pl.pallas_call(
        functools.partial(_nsa_decode_cmp_kernel, past=past, n_win=n_win),
        grid=(db,),
        in_specs=[pl.BlockSpec((1, NSA_H, NSA_KVH * HEAD_DIM), lambda b: (b, 0, 0)),
                  pl.BlockSpec((NSA_H, LANES), lambda b: (0, 0)),
                  pl.BlockSpec((1, nb, w), lambda b: (b, 0, 0)),
                  pl.BlockSpec((1, nw, w), lambda b: (b, 0, 0))],
        out_specs=[head_spec, head_spec, pl.BlockSpec((1, NSA_H, LANES), lambda b: (b, 0, 0))],
        out_shape=[head_out, head_out, jax.ShapeDtypeStruct((db, NSA_H, LANES), jnp.int32)],
        compiler_params=_cparams(("parallel",)),
        name="nsa_decode_cmp",
    )(q8, slope8, kcvc, win_all)


def _nsa_decode_sel_kernel(pt_ref, ids_ref, q_ref, slope_ref, new_ref, pa_ref, pb_ref, ocmp_ref, owin_ref,
                           gate_ref, o_ref, m_ref, l_ref, acc_ref, *, past, n_sel):
    del pt_ref
    b = pl.program_id(0)
    t = pl.program_id(1)
    q8 = q_ref[0]
    row = lax.broadcasted_iota(jnp.int32, (NSA_H, LANES), 0)
    lane = lax.broadcasted_iota(jnp.int32, (NSA_H, LANES), 1)
    slope = slope_ref[...][:, :1]
    kvw = NSA_KVH * HEAD_DIM

    @pl.when(t == 0)
    def _():
        new = new_ref[0].astype(BF16).astype(F32)
        m_ref[...] = jnp.sum(q8.astype(F32) * new[:, :kvw], axis=-1, keepdims=True) * ATT_SCALE + jnp.zeros(
            (NSA_H, LANES), F32)
        l_ref[...] = jnp.ones((NSA_H, LANES), F32)
        acc_ref[...] = jnp.where(row < NSA_G, new[:, kvw:kvw + HEAD_DIM], new[:, kvw + HEAD_DIM:])

    na = ids_ref[b, t]
    nbk = ids_ref[b, n_sel + t]
    blk = jnp.where(row < NSA_G, na, nbk)
    per_page = PAGE_SIZE // CMP_BLOCK
    valid = (lane >> CMP_SHIFT) == (blk & (per_page - 1))
    dist = past - ((blk >> (per_page.bit_length() - 1)) * PAGE_SIZE + lane)
    sa = lax.dot_general(q8[:, :HEAD_DIM], pa_ref[0, 0, :, 0, 0, :].astype(BF16), NT_DIMS,
                         preferred_element_type=F32)
    sb = lax.dot_general(q8[:, HEAD_DIM:], pb_ref[0, 0, :, 0, 1, :].astype(BF16), NT_DIMS,
                         preferred_element_type=F32)
    s = jnp.where(row < NSA_G, sa, sb) * ATT_SCALE - slope * dist.astype(F32)
    s = jnp.where(valid, s, NEG_INF)
    m_prev = m_ref[...]
    m_new = jnp.maximum(m_prev, jnp.max(s, axis=-1, keepdims=True))
    alpha = jnp.exp(m_prev - m_new)
    p = jnp.exp(s - m_new)
    l_ref[...] = alpha * l_ref[...] + jnp.sum(p, axis=-1, keepdims=True)
    m_ref[...] = m_new
    pb16 = p.astype(BF16)
    oa = jnp.dot(pb16, pa_ref[0, 0, :, 1, 0, :].astype(BF16), preferred_element_type=F32)
    ob = jnp.dot(pb16, pb_ref[0, 0, :, 1, 1, :].astype(BF16), preferred_element_type=F32)
    acc_ref[...] = alpha * acc_ref[...] + jnp.where(row < NSA_G, oa, ob)

    @pl.when(t == n_sel - 1)
    def _():
        gates = jax.nn.sigmoid(gate_ref[0])
        o = (gates[:, 0:1] * ocmp_ref[0] + gates[:, 1:2] * (acc_ref[...] / l_ref[...])
             + gates[:, 2:3] * owin_ref[0])
        o_ref[0] = o.astype(o_ref.dtype)


def nsa_decode_sel(cache, layer, page_table, ids, q8, slope8, new_row, o_cmp, o_win, gates, past):
    db = q8.shape[0]
    n_sel = ids.shape[1] // NSA_KVH
    width = 2 * NSA_KVH * HEAD_DIM
    per_page = PAGE_SIZE // CMP_BLOCK
    head_spec = pl.BlockSpec((1, NSA_H, HEAD_DIM), lambda b, t, pt, ids: (b, 0, 0))

    def page_spec(g):
        return pl.BlockSpec((1, 1) + cache.shape[2:],
                            lambda b, t, pt, ids: (layer, pt[b, ids[b, g * n_sel + t] // per_page], 0, 0, 0, 0))

    return pl.pallas_call(
        functools.partial(_nsa_decode_sel_kernel, past=past, n_sel=n_sel),
        grid_spec=pltpu.PrefetchScalarGridSpec(
            num_scalar_prefetch=2,
            grid=(db, n_sel),
            in_specs=[pl.BlockSpec((1, NSA_H, NSA_KVH * HEAD_DIM), lambda b, t, pt, ids: (b, 0, 0)),
                      pl.BlockSpec((NSA_H, LANES), lambda b, t, pt, ids: (0, 0)),
                      pl.BlockSpec((1, 1, width), lambda b, t, pt, ids: (b, 0, 0)),
                      page_spec(0), page_spec(1), head_spec, head_spec, head_spec],
            out_specs=head_spec,
            scratch_shapes=[pltpu.VMEM((NSA_H, LANES), F32), pltpu.VMEM((NSA_H, LANES), F32),
                            pltpu.VMEM((NSA_H, HEAD_DIM), F32)]),
        out_shape=jax.ShapeDtypeStruct((db, NSA_H, HEAD_DIM), BF16),
        compiler_params=_cparams(("parallel", "arbitrary")),
        name="nsa_decode_sel",
    )(page_table, ids, q8, slope8, new_row, cache, cache, o_cmp, o_win, gates)


def _diff_decode_kernel(pt_ref, q_ref, slope_ref, new_ref, lam_ref, dn_ref, *refs, pp, past, lam_init):
    del pt_ref
    page_refs = refs[:pp]
    o_ref, m_ref, l_ref, acc_ref = refs[pp:]
    j = pl.program_id(1)
    rows = 2 * DIFF_H
    n_keys = PAGE_SIZE * DIFF_H
    h_shift = DIFF_H.bit_length() - 1
    q8 = q_ref[0]
    slope2 = slope_ref[...][:, :1] * LOG2E
    row = lax.broadcasted_iota(jnp.int32, (rows, n_keys), 0)
    col = lax.broadcasted_iota(jnp.int32, (rows, n_keys), 1)
    own_head = (col & (DIFF_H - 1)) == (row & (DIFF_H - 1))
    key_in_page = (col >> h_shift).astype(F32)

    @pl.when(j == 0)
    def _():
        k_new = jnp.concatenate([new_ref[0, 0]] * 2, axis=0).astype(BF16).astype(F32)
        m_ref[0] = jnp.sum(q8.astype(F32) * k_new, axis=-1, keepdims=True) * (ATT_SCALE * LOG2E) + jnp.zeros(
            (rows, LANES), F32)
        l_ref[0] = jnp.ones((rows, LANES), F32)
        acc_ref[0] = jnp.concatenate([new_ref[0, 1]] * 2, axis=0).astype(BF16).astype(F32)

    scores, values = [], []
    for i in range(pp):
        page = page_refs[i]
        k0 = (j * pp + i) * PAGE_SIZE
        kx = page[0, 0, :, 0].reshape(n_keys, DIFF_VD).astype(BF16)
        values.append(page[0, 0, :, 1].reshape(n_keys, DIFF_VD).astype(BF16))
        s = lax.dot_general(q8, kx, NT_DIMS, preferred_element_type=F32) * (ATT_SCALE * LOG2E)
        s = s - slope2 * ((past - k0).astype(F32) - key_in_page)
        scores.append(jnp.where(own_head, s, NEG_INF))
    _flash_step(scores, values, m_ref, l_ref, acc_ref, 0)

    @pl.when(j == pl.num_programs(1) - 1)
    def _():
        lam = _diff_lambda(lam_ref[...], lam_init)
        o = _flash_result(l_ref, acc_ref, 0)
        oh = o[:DIFF_H] - lam * o[DIFF_H:]
        y = oh * lax.rsqrt(jnp.mean(oh * oh, axis=-1, keepdims=True) + NORM_EPS)
        o_ref[0] = (y * dn_ref[...] * (1.0 - lam_init)).astype(o_ref.dtype)


def diff_decode(cache, layer, page_table, q8, slope8, new_row, lam_vec, dnorm, lam_init, past, pp=8):
    db, n_pages = page_table.shape
    rows = 2 * DIFF_H

    def page_spec(i):
        return pl.BlockSpec((1, 1) + cache.shape[2:], lambda b, j, pt: (layer, pt[b, j * pp + i], 0, 0, 0, 0))

    return pl.pallas_call(
        functools.partial(_diff_decode_kernel, pp=pp, past=past, lam_init=lam_init),
        grid_spec=pltpu.PrefetchScalarGridSpec(
            num_scalar_prefetch=1,
            grid=(db, n_pages // pp),
            in_specs=[pl.BlockSpec((1, rows, DIFF_VD), lambda b, j, pt: (b, 0, 0)),
                      pl.BlockSpec((rows, LANES), lambda b, j, pt: (0, 0)),
                      pl.BlockSpec((1, 2, DIFF_H, DIFF_VD), lambda b, j, pt: (b, 0, 0, 0)),
                      pl.BlockSpec((4, HEAD_DIM), lambda b, j, pt: (0, 0)),
                      pl.BlockSpec((1, DIFF_VD), lambda b, j, pt: (0, 0))] + [page_spec(i) for i in range(pp)],
            out_specs=pl.BlockSpec((1, DIFF_H, DIFF_VD), lambda b, j, pt: (b, 0, 0)),
            scratch_shapes=[pltpu.VMEM((1, rows, LANES), F32), pltpu.VMEM((1, rows, LANES), F32),
                            pltpu.VMEM((1, rows, DIFF_VD), F32)]),
        out_shape=jax.ShapeDtypeStruct((db, DIFF_H, DIFF_VD), BF16),
        compiler_params=_cparams(("parallel", "arbitrary")),
        name="diff_decode",
    )(page_table, q8, slope8, new_row, lam_vec, dnorm.reshape(1, DIFF_VD), *([cache] * pp))


def _ret_decode_kernel(lg_ref, q_ref, k_ref, v_ref, g_ref, rn_ref, st_ref, o_ref, nst_ref):
    ii = lax.broadcasted_iota(jnp.int32, (RET_DK, RET_DK), 0)
    jj = lax.broadcasted_iota(jnp.int32, (RET_DK, RET_DK), 1)
    ones = jnp.ones((1, 1), F32)
    for h in range(RET_H):
        gamma = jnp.exp(ones * lg_ref[h])
        q = q_ref[0, h:h + 1, :]
        k = k_ref[0, h:h + 1, :] * (RET_DK ** -0.5)
        v = v_ref[0, h:h + 1, :]
        state = st_ref[0, h]
        qb = q.astype(BF16).astype(F32)
        kb = k.astype(BF16).astype(F32)
        s = jnp.sum(qb * kb, axis=-1, keepdims=True)
        q_dec = jnp.broadcast_to(q * gamma, (SUBLANES, RET_DK)).astype(BF16)
        o = s * v + jnp.dot(q_dec, state.astype(BF16), preferred_element_type=F32)[0:1]
        k_col = jnp.sum(jnp.where(ii == jj, k, 0.0), axis=-1, keepdims=True)
        nst_ref[0, h] = state * gamma + k_col * v
        y = o * lax.rsqrt(jnp.mean(o * o, axis=-1, keepdims=True) + NORM_EPS) * rn_ref[h]
        gate = g_ref[0, h:h + 1, :]
        o_ref[0, h:h + 1, :] = (gate * jax.nn.sigmoid(gate) * y).astype(o_ref.dtype)


def ret_decode(q, k, v, g, rnorm, state, log_g):
    db = q.shape[0]
    return pl.pallas_call(
        _ret_decode_kernel,
        grid=(db,),
        in_specs=[pl.BlockSpec(memory_space=pltpu.SMEM),
                  pl.BlockSpec((1, RET_H, RET_DK), lambda b: (b, 0, 0)),
                  pl.BlockSpec((1, RET_H, RET_DK), lambda b: (b, 0, 0)),
                  pl.BlockSpec((1, RET_H, RET_DV), lambda b: (b, 0, 0)),
                  pl.BlockSpec((1, RET_H, RET_DV), lambda b: (b, 0, 0)),
                  pl.BlockSpec((RET_H, 1, RET_DV), lambda b: (0, 0, 0)),
                  pl.BlockSpec((1, RET_H, RET_DK, RET_DV), lambda b: (b, 0, 0, 0))],
        out_specs=[pl.BlockSpec((1, RET_H, RET_DV), lambda b: (b, 0, 0)),
                   pl.BlockSpec((1, RET_H, RET_DK, RET_DV), lambda b: (b, 0, 0, 0))],
        out_shape=[jax.ShapeDtypeStruct((db, RET_H, RET_DV), BF16),
                   jax.ShapeDtypeStruct(state.shape, F32)],
        compiler_params=_cparams(("parallel",)),
        name="ret_decode",
    )(log_g, q, k, v, g, rnorm.reshape(RET_H, 1, RET_DV), state)


def _alibi_slopes(n):
    return jnp.asarray([2.0 ** (-8.0 * (i + 1) / n) for i in range(n)], dtype=F32)


def _even_layer(e, layer, hp, hs, gain, batch, seq, caches, page_table, even_w_in, cmp_pos, lam_vec, dnorm, stacks):
    cache_cmp, cache_sel, cache_win, cache_diff = caches
    n_even = even_w_in.shape[0]
    d = even_w_in.shape[1]
    lam_init = 0.8 - 0.6 * math.exp(-0.3 * layer)
    db = page_table.shape[0]
    past = page_table.shape[1] * PAGE_SIZE
    nsa_slopes = _alibi_slopes(NSA_H)
    diff_slopes = _alibi_slopes(DIFF_H)
    kvw = 2 * NSA_KV_W
    per_group = NSA_G * 3
    w_gate = even_w_in[e, :, GATE_OFF:GATE_OFF + NSA_GATE_W].reshape(d, NSA_KVH, per_group)
    w_gate = jnp.pad(w_gate, ((0, 0), (0, 0), (0, LANES - per_group))).reshape(d, NSA_KVH * LANES)
    w_tail = even_w_in[e, :, GATE_OFF + NSA_GATE_W:]
    rn = gain is not None
    if rn:
        w_gate, w_tail = w_gate * gain[:, None], w_tail * gain[:, None]
    w_gate, w_tail = w_gate.astype(BF16), w_tail.astype(BF16)
    nrm = dict(gain=gain, row_norm=rn)
    nrm_folded = dict(row_norm=rn)

    qn_bf, qn_s = dense(hp, even_w_in, NSA_Q_W, lead=e, out_dtypes=(BF16,), xs=hs, **nrm)
    (cmp_st, cmp_bf), (new_cmp, _) = dense(hp, even_w_in, kvw, col0=NSA_Q_W, lead=e, out_dtypes=(F32, BF16),
                                           stack=(n_even, e, stacks[0]), xs=hs, **nrm)
    (sel_st, sel_bf), (new_sel, _) = dense(hp, even_w_in, kvw, col0=NSA_Q_W + kvw, lead=e,
                                           out_dtypes=(F32, BF16), stack=(n_even, e, stacks[1]), xs=hs, **nrm)
    (win_f, win_bf), (new_win, _) = dense(hp, even_w_in, kvw, col0=NSA_Q_W + 2 * kvw, lead=e,
                                          out_dtypes=(F32, BF16), xs=hs, **nrm)
    gate_pre, gate_s = dense(hp, w_gate, NSA_KVH * LANES, xs=hs, **nrm_folded)
    qd_bf, qd_s = dense(hp, w_tail, DIFF_Q_W, out_dtypes=(BF16,), xs=hs, **nrm_folded)
    (kvd_st, kvd_bf), (new_diff, _) = dense(hp, w_tail, DIFF_Q_W + DIFF_V_W, col0=DIFF_Q_W,
                                            out_dtypes=(F32, BF16), stack=(n_even, e, stacks[2]), xs=hs,
                                            **nrm_folded)
    qn_s, new_cmp, new_sel, new_win, gate_s, qd_s, new_diff = (
        t[:db] for t in (qn_s, new_cmp, new_sel, new_win, gate_s, qd_s, new_diff))

    pos_cg = jnp.transpose(cmp_pos, (0, 2, 1)).reshape(2 * NSA_KVH, CMP_BLOCK)
    kcvc = compress_prompt(cmp_bf, jnp.tile(pos_cg, (1, seq // CMP_BLOCK)), batch, seq)
    o_nsa = nsa_prompt(qn_bf, gate_pre, kcvc, sel_bf, win_bf, nsa_slopes, batch, seq)
    o_diff = diff_prompt(qd_bf, kvd_bf, lam_vec, dnorm, diff_slopes, lam_init, batch, seq)
    mix_p = (o_nsa, o_diff)
    keep_p = min(WINDOW, seq)
    win_p = win_f.reshape(batch, seq, kvw)[:, seq - keep_p:].reshape(batch, keep_p, 2, NSA_KVH, HEAD_DIM)

    gates8 = gate_s.reshape(db, NSA_KVH, LANES)[:, :, :per_group].reshape(db, NSA_H, 3)
    gates8 = jnp.pad(gates8, ((0, 0), (0, 0), (0, LANES - 3)))
    group_of_head = jnp.asarray(np.arange(NSA_H) // NSA_G)
    onehot_g = jax.nn.one_hot(group_of_head, NSA_KVH, dtype=BF16)
    qn = qn_s.reshape(db, NSA_H, HEAD_DIM)
    q8 = (qn[:, :, None, :] * onehot_g[None, :, :, None]).reshape(db, NSA_H, NSA_KVH * HEAD_DIM)
    slope8 = jnp.broadcast_to(nsa_slopes[:, None], (NSA_H, LANES))
    pos_rg = jnp.tile(jnp.transpose(cmp_pos, (0, 2, 1)), (1, SUBLANES // NSA_KVH, PAGE_SIZE // CMP_BLOCK))
    pos_rows = jnp.repeat(pos_rg, NSA_KVH, axis=2)
    kcvc_s = compress_sample(cache_cmp, e, page_table, pos_rows)
    w_buf = cache_win.shape[2]
    win_all = jnp.concatenate([cache_win[e].reshape(db, w_buf, -1), new_win[:, None, :]], axis=1)
    n_win = w_buf + 1
    win_pad = jnp.pad(win_all, ((0, 0), (0, -n_win % LANES), (0, 0)))
    o_cmp, o_win, idx = nsa_decode_cmp(q8, slope8, kcvc_s, win_pad, past, n_win)
    ids = jnp.concatenate([idx[:, 0, :TOP_N - 1], idx[:, NSA_G, :TOP_N - 1]], axis=1)
    o_nsa_s = nsa_decode_sel(cache_sel, e, page_table, ids, q8, slope8, new_sel[:, None, :], o_cmp, o_win, gates8,
                             past)

    qd = qd_s.reshape(db, DIFF_H, 2, HEAD_DIM)
    eye_c = jnp.eye(2, dtype=BF16)
    q8d = jnp.einsum('bhcd,cj->bchjd', qd, eye_c).reshape(db, 2 * DIFF_H, DIFF_VD)
    slope8d = jnp.broadcast_to(jnp.tile(diff_slopes, 2)[:, None], (2 * DIFF_H, LANES))
    o_diff_s = diff_decode(cache_diff, e, page_table, q8d, slope8d, new_diff.reshape(db, 2, DIFF_H, DIFF_VD),
                           lam_vec, dnorm, lam_init, past)
    pad_rows = ((0, hs.shape[0] - db), (0, 0))
    mix_s = (jnp.pad(o_nsa_s.reshape(db, NSA_Q_W), pad_rows), jnp.pad(o_diff_s.reshape(db, DIFF_V_W), pad_rows))
    kv_s = (db, 1, 2, NSA_KVH, HEAD_DIM)
    keep = min(WINDOW, n_win)
    small = (win_p, new_cmp.reshape(kv_s), new_sel.reshape(kv_s),
             win_all[:, n_win - keep:].reshape(db, keep, 2, NSA_KVH, HEAD_DIM),
             new_diff.reshape(db, 1, 2, DIFF_H, DIFF_VD))
    return mix_p, mix_s, (cmp_st, sel_st, kvd_st), small


def _ret_layer(o, hp, hs, gain, batch, seq, state, ret_w_in, rnorm):
    db = state.shape[0]
    log_g = jnp.log1p(-jnp.exp2(-5.0 - jnp.arange(RET_H, dtype=F32)))
    n_in = ret_w_in.shape[2]
    proj, ps = dense(hp, ret_w_in, n_in, lead=o, xs=hs, gain=gain, row_norm=gain is not None)
    ps = ps[:db]
    gated_p, st_p = ret_prompt(proj, rnorm, log_g, batch, seq)
    hk = RET_H * RET_DK
    hv = RET_H * RET_DV
    q = ps[:, :hk].reshape(db, RET_H, RET_DK)
    k = ps[:, hk:2 * hk].reshape(db, RET_H, RET_DK)
    v = ps[:, 2 * hk:2 * hk + hv].reshape(db, RET_H, RET_DV)
    g = ps[:, 2 * hk + hv:].reshape(db, RET_H, RET_DV)
    gated_s, st_s = ret_decode(q, k, v, g, rnorm, state, log_g)
    return gated_p, jnp.pad(gated_s.reshape(db, hv), ((0, hs.shape[0] - db), (0, 0))), st_p, st_s


def kernel(x_prompt, x_sample, cache_nsa_cmp, cache_nsa_sel, cache_nsa_win, cache_diff, state_ret, page_table,
           norm_mix, norm_ffn, norm_final, even_w_in, even_w_out, nsa_cmp_pos, diff_lambda, diff_norm,
           ret_w_in, ret_norm, ret_w_out, ffn_w13, ffn_w2):
    batch, seq, d = x_prompt.shape
    db = x_sample.shape[0]
    d_ff = ffn_w2.shape[1]
    xp = x_prompt.reshape(batch * seq, d)
    xs = jnp.pad(x_sample.reshape(db, d), ((0, -db % SAMPLE_ROWS), (0, 0)))
    caches = (cache_nsa_cmp, cache_nsa_sel, cache_nsa_win, cache_diff)
    stacks = (None, None, None)
    small = [[] for _ in range(5)]
    ret_p, ret_s = [], []
    both = (F32, BF16)
    for layer in range(DEPTH):
        if layer == 0:
            hp, hs, gain = rmsnorm(xp, norm_mix[0], BF16), rmsnorm(xs, norm_mix[0], BF16), None
        else:
            hp, hs, gain = xp_bf, xs_bf, norm_mix[layer]
        if layer % 2 == 0:
            e = layer // 2
            mix_p, mix_s, stacks, small_e = _even_layer(
                e, layer, hp, hs, gain, batch, seq, caches, page_table, even_w_in, nsa_cmp_pos[e], diff_lambda[e],
                diff_norm[e], stacks)
            for lst, item in zip(small, small_e):
                lst.append(item)
            w_out, lead = even_w_out, e
        else:
            o = layer // 2
            mix_p, mix_s, st_p, st_s = _ret_layer(o, hp, hs, gain, batch, seq, state_ret[o], ret_w_in, ret_norm[o])
            ret_p.append(st_p)
            ret_s.append(st_s)
            w_out, lead = ret_w_out, o
        (xp, xp_bf), (xs, xs_bf) = dense(mix_p, w_out, d, lead=lead, res=xp, xs=mix_s, res_s=xs, out_dtypes=both)
        up_p, up_s = dense(xp_bf, ffn_w13, d_ff, lead=layer, col3=d_ff, out_dtypes=(BF16,), xs=xs_bf,
                           gain=norm_ffn[layer], row_norm=True)
        if layer + 1 < DEPTH:
            (xp, xp_bf), (xs, xs_bf) = dense(up_p, ffn_w2, d, lead=layer, res=xp, xs=up_s, res_s=xs, out_dtypes=both)
        else:
            xp, xs = dense(up_p, ffn_w2, d, lead=layer, res=xp, xs=up_s, res_s=xs)
    y_prompt = rmsnorm(xp, norm_final, F32).reshape(batch, seq, d)
    y_sample = rmsnorm(xs, norm_final, F32)[:db].reshape(db, 1, d)
    n_even = even_w_in.shape[0]
    cmp_st, sel_st, kvd_st = stacks
    win_p, cmp_s, sel_s, win_s, diff_s = [jnp.stack(t) for t in small]
    return (y_prompt, y_sample,
            cmp_st.reshape(n_even, batch, seq, 2, NSA_KVH, HEAD_DIM),
            sel_st.reshape(n_even, batch, seq, 2, NSA_KVH, HEAD_DIM), win_p,
            kvd_st.reshape(n_even, batch, seq, 2, DIFF_H, DIFF_VD), jnp.stack(ret_p),
            cmp_s, sel_s, win_s, diff_s, jnp.stack(ret_s))
```

```python
import functools
import math

import jax
import jax.numpy as jnp
import numpy as np
from jax import lax
from jax.experimental import pallas as pl
from jax.experimental.pallas import tpu as pltpu

F32 = jnp.float32
BF16 = jnp.bfloat16

D_MODEL = 2048
DEPTH = 4
PAGE_SIZE = 128
HEAD_DIM = 128
NSA_H = 8
NSA_KVH = 2
NSA_G = NSA_H // NSA_KVH
CMP_BLOCK = 64
CMP_SHIFT = CMP_BLOCK.bit_length() - 1
TOP_N = 16
WINDOW = 512
FORCE_SCORE = 1.0e4
DIFF_H = 4
DIFF_VD = 2 * HEAD_DIM
RET_H = 8
RET_DK = D_MODEL // RET_H
RET_DV = 2 * D_MODEL // RET_H
RET_CHUNK = 128
D_FF = ((8 * D_MODEL + 3 * 256 - 1) // (3 * 256)) * 256
NORM_EPS = 1e-6
NEG_INF = -1e30
ATT_SCALE = HEAD_DIM ** -0.5
LOG2E = math.log2(math.e)

NSA_Q_W = NSA_H * HEAD_DIM
NSA_KV_W = NSA_KVH * HEAD_DIM
NSA_GATE_W = NSA_H * 3
DIFF_Q_W = DIFF_H * 2 * HEAD_DIM
DIFF_V_W = DIFF_H * DIFF_VD
MAIN_W = NSA_Q_W + 6 * NSA_KV_W + 2 * DIFF_Q_W + DIFF_V_W
GATE_OFF = NSA_Q_W + 6 * NSA_KV_W

NSA_TQ, NSA_TK = 256, 256
DIFF_TQ, DIFF_TK = 512, 512
RET_BLOCK = 512

LANES = 128
SUBLANES = 8
SAMPLE_ROWS = 2 * SUBLANES
VMEM_LIMIT = 56 * 1024 * 1024


def _cparams(sem):
    return pltpu.CompilerParams(dimension_semantics=sem, vmem_limit_bytes=VMEM_LIMIT)


def _rmsnorm_kernel(x_ref, g_ref, o_ref):
    x = x_ref[...]
    y = x * lax.rsqrt(jnp.mean(x * x, axis=-1, keepdims=True) + NORM_EPS)
    o_ref[...] = (y * g_ref[...]).astype(o_ref.dtype)


def rmsnorm(x, g, out_dtype):
    m, d = x.shape
    tm = min(512, m)
    return pl.pallas_call(
        _rmsnorm_kernel,
        grid=(m // tm,),
        in_specs=[pl.BlockSpec((tm, d), lambda i: (i, 0)), pl.BlockSpec((1, d), lambda i: (0, 0))],
        out_specs=pl.BlockSpec((tm, d), lambda i: (i, 0)),
        out_shape=jax.ShapeDtypeStruct((m, d), out_dtype),
        compiler_params=_cparams(("parallel",)),
        name="rmsnorm",
    )(x, g.reshape(1, d))


def _dense_kernel(*refs, n_x, n_w, has_gain, row_norm, has_res, has_prev, has_s, n_out, cast_w, n_i, init_slot):
    refs = list(refs)
    x_refs = [refs.pop(0) for _ in range(n_x)]
    xs_refs = [refs.pop(0) for _ in range(n_x)] if has_s else []
    w_refs = [refs.pop(0) for _ in range(n_w)]
    gain_ref = refs.pop(0) if has_gain else None
    res_ref = refs.pop(0) if has_res else None
    res_s_ref = refs.pop(0) if (has_res and has_s) else None
    if has_prev:
        refs.pop(0)
    out_refs = [refs.pop(0) for _ in range(n_out)]
    out_s_refs = [refs.pop(0) for _ in range(n_out)] if has_s else []
    wb_refs = [refs.pop(0) for _ in range(n_w)] if cast_w else []
    r_ref, rs_ref = (refs.pop(0), refs.pop(0)) if row_norm else (None, None)
    j = pl.program_id(0)
    i = pl.program_id(1)
    if cast_w:
        @pl.when(i == 0)
        def _():
            for w_ref, wb_ref in zip(w_refs, wb_refs):
                w = w_ref[...]
                wb_ref[...] = (w * gain_ref[...] if has_gain else w).astype(BF16)
        w_refs = wb_refs

    def rms_factor(pieces):
        sq = sum(jnp.sum(p.astype(F32) ** 2, axis=-1, keepdims=True) for p in pieces)
        width = sum(p.shape[-1] for p in pieces)
        return lax.rsqrt(sq / width + NORM_EPS) + jnp.zeros((pieces[0].shape[0], LANES), F32)

    if row_norm:
        @pl.when(j == 0)
        def _():
            r_ref[i] = rms_factor([x_ref[...] for x_ref in x_refs])
            if has_s:
                @pl.when(i == n_i - 1)
                def _():
                    rs_ref[...] = rms_factor([xs_ref[...] for xs_ref in xs_refs])

    def product(pieces, scale):
        def one(w_ref):
            k0, acc = 0, None
            for p in pieces:
                part = jnp.dot(p, w_ref[k0:k0 + p.shape[-1], :], preferred_element_type=F32)
                acc = part if acc is None else acc + part
                k0 += p.shape[-1]
            return acc if scale is None else acc * _lane_tile(scale, acc.shape[-1])
        acc = one(w_refs[0])
        if n_w == 2:
            acc = acc * jax.nn.sigmoid(acc) * one(w_refs[1])
        return acc

    def emit(acc, r, dst_refs):
        if r is not None:
            acc = r[...] + acc
        for o_ref in dst_refs:
            if o_ref.ndim == 3:
                for s in range(o_ref.shape[0]):
                    o_ref[s] = acc.astype(o_ref.dtype) if s == init_slot else jnp.zeros(acc.shape, o_ref.dtype)
            else:
                o_ref[...] = acc.astype(o_ref.dtype)

    def plain():
        emit(product([x_ref[...] for x_ref in x_refs], r_ref[i] if row_norm else None), res_ref, out_refs)

    def with_sample_rows():
        tm = x_refs[0].shape[0]
        pieces = [jnp.concatenate([x_ref[...], xs_ref[...]], axis=0) for x_ref, xs_ref in zip(x_refs, xs_refs)]
        scale = jnp.concatenate([r_ref[i], rs_ref[...]], axis=0) if row_norm else None
        acc = product(pieces, scale)
        emit(acc[:tm], res_ref, out_refs)
        emit(acc[tm:], res_s_ref, out_s_refs)

    if has_s:
        pl.when(i < n_i - 1)(plain)
        pl.when(i == n_i - 1)(with_sample_rows)
    else:
        plain()


def _dense_tiles(m, k, n, n_w):
    tn = min(1024 if (k * n_w <= 2048 and n % 1024 == 0) else 512, n)
    tm = min(1024 if k * n_w <= 4096 else 512, m)
    return tm, tn


def dense(x, w, n, col0=0, lead=None, col3=None, res=None, out_dtypes=(F32,), stack=None, xs=None, res_s=None,
          gain=None, row_norm=False):
    xparts = tuple(x) if isinstance(x, (tuple, list)) else (x,)
    sparts = (tuple(xs) if isinstance(xs, (tuple, list)) else (xs,)) if xs is not None else ()
    m = xparts[0].shape[0]
    k = sum(p.shape[1] for p in xparts)
    n_w = 1 if col3 is None else 2
    tm, tn = _dense_tiles(m, k, n, n_w)
    assert m % tm == 0 and n % tn == 0 and col0 % tn == 0 and (col3 is None or col3 % tn == 0)
    cast_w = w.dtype != BF16
    assert gain is None or (cast_w and row_norm)
    has_s = xs is not None
    n_i = m // tm

    def w_spec(c0):
        if w.ndim == 3:
            return pl.BlockSpec((None, k, tn), lambda j, i: (lead, 0, c0 // tn + j))
        return pl.BlockSpec((k, tn), lambda j, i: (0, c0 // tn + j))

    in_specs = [pl.BlockSpec((tm, p.shape[1]), lambda j, i: (i, 0)) for p in xparts]
    args = list(xparts)
    if has_s:
        ms = sparts[0].shape[0]
        in_specs += [pl.BlockSpec((ms, p.shape[1]), lambda j, i: (0, 0)) for p in sparts]
        args += list(sparts)
    in_specs.append(w_spec(col0))
    args.append(w)
    if n_w == 2:
        in_specs.append(w_spec(col3))
        args.append(w)
    if gain is not None:
        in_specs.append(pl.BlockSpec((k, 1), lambda j, i: (0, 0)))
        args.append(gain.reshape(k, 1))
    if res is not None:
        in_specs.append(pl.BlockSpec((tm, tn), lambda j, i: (i, j)))
        args.append(res)
        if has_s:
            in_specs.append(pl.BlockSpec((ms, tn), lambda j, i: (0, j)))
            args.append(res_s)
    out_specs = [pl.BlockSpec((tm, tn), lambda j, i: (i, j)) for _ in out_dtypes]
    out_shape = [jax.ShapeDtypeStruct((m, n), dt) for dt in out_dtypes]
    aliases = {}
    has_prev = False
    init_slot = None
    if stack is not None:
        depth, slot, prev = stack
        out_specs[0] = pl.BlockSpec((None, tm, tn), lambda j, i: (slot, i, j))
        out_shape[0] = jax.ShapeDtypeStruct((depth, m, n), out_dtypes[0])
        if prev is None:
            init_slot = slot
            out_specs[0] = pl.BlockSpec((depth, tm, tn), lambda j, i: (0, i, j))
        if prev is not None:
            has_prev = True
            aliases = {len(args): 0}
            in_specs.append(pl.BlockSpec(memory_space=pl.ANY))
            args.append(prev)
    if has_s:
        out_specs += [pl.BlockSpec((ms, tn), lambda j, i: (0, j)) for _ in out_dtypes]
        out_shape += [jax.ShapeDtypeStruct((ms, n), dt) for dt in out_dtypes]
    scratch = [pltpu.VMEM((k, tn), BF16)] * (n_w if cast_w else 0)
    if row_norm:
        scratch += [pltpu.VMEM((n_i, tm, LANES), F32), pltpu.VMEM((ms if has_s else SUBLANES, LANES), F32)]
    outs = pl.pallas_call(
        functools.partial(_dense_kernel, n_x=len(xparts), n_w=n_w, has_gain=gain is not None, row_norm=row_norm,
                          has_res=res is not None, has_prev=has_prev, has_s=has_s, n_out=len(out_dtypes),
                          cast_w=cast_w, n_i=n_i, init_slot=init_slot),
        grid=(n // tn, n_i),
        in_specs=in_specs,
        out_specs=out_specs,
        out_shape=out_shape,
        scratch_shapes=scratch,
        input_output_aliases=aliases,
        compiler_params=_cparams(("arbitrary", "arbitrary")),
        name="dense",
    )(*args)
    n_out = len(out_dtypes)
    unwrap = lambda t: t[0] if n_out == 1 else tuple(t)
    if has_s:
        return unwrap(outs[:n_out]), unwrap(outs[n_out:])
    return unwrap(outs)


def _softmax_rows64(logits):
    e = jnp.exp(logits - jnp.max(logits, axis=-1, keepdims=True))
    reps = logits.shape[-1] // CMP_BLOCK
    return e * (reps / jnp.sum(e, axis=-1, keepdims=True))


def _compress_prompt_kernel(x_ref, pos_ref, o_ref, *, seq):
    w = _softmax_rows64(pos_ref[...])
    blk = lax.broadcasted_iota(jnp.int32, (LANES, seq), 0)
    key_blk = lax.broadcasted_iota(jnp.int32, (LANES, seq), 1) >> CMP_SHIFT
    onblk = blk == key_blk
    for c in range(4):
        wb = jnp.where(onblk, w[c:c + 1, :], 0.0).astype(BF16)
        o_ref[0, :, c * LANES:(c + 1) * LANES] = jnp.dot(
            wb, x_ref[:, c * LANES:(c + 1) * LANES], preferred_element_type=F32).astype(BF16)


def compress_prompt(cmp_bf, pos_tiled, batch, seq):
    return pl.pallas_call(
        functools.partial(_compress_prompt_kernel, seq=seq),
        grid=(batch,),
        in_specs=[pl.BlockSpec((seq, 4 * LANES), lambda b: (b, 0)),
                  pl.BlockSpec((4, seq), lambda b: (0, 0))],
        out_specs=pl.BlockSpec((1, LANES, 4 * LANES), lambda b: (b, 0, 0)),
        out_shape=jax.ShapeDtypeStruct((batch, LANES, 4 * LANES), BF16),
        compiler_params=_cparams(("parallel",)),
        name="nsa_compress_prompt",
    )(cmp_bf, pos_tiled)


def _lane_tile(x, width):
    return x if width == LANES else jnp.concatenate([x] * (width // LANES), axis=-1)


def _flash_step(s, v, m_ref, l_ref, acc_ref, idx):
    ss, vs = (s, v) if isinstance(s, (list, tuple)) else ([s], [v])
    tiles = [[t[:, j * LANES:(j + 1) * LANES] for j in range(t.shape[1] // LANES)] for t in ss]
    blocks = [blk for tile in tiles for blk in tile]
    mx = blocks[0]
    for blk in blocks[1:]:
        mx = jnp.maximum(mx, blk)
    m_prev = m_ref[idx]
    m_new = jnp.maximum(m_prev, jnp.max(mx, axis=-1, keepdims=True))
    alpha = jnp.exp2(m_prev - m_new)
    ps = [[jnp.exp2(blk - m_new) for blk in tile] for tile in tiles]
    row_sum = None
    for tile in ps:
        for p in tile:
            row_sum = p if row_sum is None else row_sum + p
    l_ref[idx] = alpha * l_ref[idx] + jnp.sum(row_sum, axis=-1, keepdims=True)
    m_ref[idx] = m_new
    acc = _lane_tile(alpha, vs[0].shape[-1]) * acc_ref[idx]
    for tile, vt in zip(ps, vs):
        acc = acc + jnp.dot(jnp.concatenate([p.astype(BF16) for p in tile], axis=-1), vt,
                            preferred_element_type=F32)
    acc_ref[idx] = acc


def _flash_init(m_ref, l_ref, acc_ref):
    m_ref[...] = jnp.full(m_ref.shape, NEG_INF, F32)
    l_ref[...] = jnp.zeros(l_ref.shape, F32)
    acc_ref[...] = jnp.zeros(acc_ref.shape, F32)


def _flash_result(l_ref, acc_ref, idx):
    return acc_ref[idx] / _lane_tile(l_ref[idx], acc_ref.shape[-1])


def _softmax_pv(s, v):
    p = jnp.exp2(s - jnp.max(s, axis=-1, keepdims=True))
    l = jnp.sum(p, axis=-1, keepdims=True)
    return jnp.dot(p.astype(BF16), v, preferred_element_type=F32) / l


def _nsa_prompt_kernel(slopes_ref, q_ref, gate_ref, kc_ref, vc_ref, ks_ref, vs_ref, kw_ref, vw_ref,
                       o_ref, m_ref, l_ref, acc_ref, flag_ref, *, tq, tk, n_blocks):
    g = pl.program_id(1)
    qi = pl.program_id(2)
    q0 = qi * tq
    row = lax.broadcasted_iota(jnp.int32, (tq, LANES), 0)
    col = lax.broadcasted_iota(jnp.int32, (tq, LANES), 1)
    qpos = q0 + row
    nt = (((1,), (1,)), ((), ()))

    kc = kc_ref[0]
    vc = vc_ref[0]
    cdist = qpos - (col * CMP_BLOCK + (CMP_BLOCK - 1))
    cmask = cdist >= 0
    cdist_f = cdist.astype(F32)
    imp = jnp.zeros((tq, LANES), F32)
    o_cmp = []
    for h in range(NSA_G):
        slope = slopes_ref[g * NSA_G + h]
        qh = q_ref[:, h * HEAD_DIM:(h + 1) * HEAD_DIM]
        s = lax.dot_general(qh, kc, nt, preferred_element_type=F32) * ATT_SCALE - slope * cdist_f
        s = jnp.where(cmask, s, NEG_INF)
        p = jnp.exp(s - jnp.max(s, axis=-1, keepdims=True))
        p = jnp.where(cmask, p / jnp.sum(p, axis=-1, keepdims=True), 0.0)
        imp = imp + p
        o_cmp.append(jnp.dot(p.astype(BF16), vc, preferred_element_type=F32))

    imp = jnp.where(cmask, imp, -1.0)
    forced = (col == (qpos >> CMP_SHIFT)) | (col == 0)
    imp = jnp.where(forced, FORCE_SCORE, imp)
    nblk = -(-n_blocks // SUBLANES) * SUBLANES
    imp_t = imp.T[:nblk]
    sub = lax.broadcasted_iota(jnp.int32, (SUBLANES, tq), 0)
    sel_rows = []
    for vi in range(nblk // SUBLANES):
        x = imp_t[vi * SUBLANES:(vi + 1) * SUBLANES]
        rank = jnp.zeros((SUBLANES, tq), F32)
        for j in range(nblk):
            r = imp_t[j:j + 1, :]
            jv, jr = divmod(j, SUBLANES)
            ge = jnp.where(r >= x, 1.0, 0.0)
            gt = jnp.where(r > x, 1.0, 0.0)
            if jv < vi:
                rank = rank + ge
            elif jv > vi:
                rank = rank + gt
            else:
                rank = rank + jnp.where(sub > jr, ge, gt)
        sel_rows.append(jnp.where(rank < TOP_N, 1.0, 0.0))
    sel_rows.append(jnp.zeros((LANES - nblk, tq), F32))
    sel_f = jnp.concatenate(sel_rows, axis=0).T
    sel = sel_f.astype(BF16)
    picked_any = jnp.max(sel_f, axis=0, keepdims=True)
    tile_of_block = lax.broadcasted_iota(jnp.int32, (1, LANES), 1) >> ((tk // CMP_BLOCK).bit_length() - 1)
    for kt in range(n_blocks * CMP_BLOCK // tk):
        hit = jnp.max(jnp.where(tile_of_block == kt, picked_any, 0.0))
        flag_ref[kt] = (hit > 0.5).astype(jnp.int32)

    c1 = ATT_SCALE * LOG2E
    slope2 = [slopes_ref[g * NSA_G + h] * LOG2E for h in range(NSA_G)]

    _flash_init(m_ref, l_ref, acc_ref)
    e_row = lax.broadcasted_iota(jnp.int32, (LANES, tk), 0)
    e_col = lax.broadcasted_iota(jnp.int32, (LANES, tk), 1)
    rel = lax.broadcasted_iota(jnp.int32, (1, tk), 1)
    krow = lax.broadcasted_iota(jnp.int32, (tq, tk), 0)
    kcol = lax.broadcasted_iota(jnp.int32, (tq, tk), 1)

    def sel_tile(kt, diagonal):
        k0 = pl.multiple_of(kt * tk, tk)
        k = ks_ref[pl.ds(k0, tk), :]
        v = vs_ref[pl.ds(k0, tk), :]
        expand = jnp.where(((k0 + e_col) >> CMP_SHIFT) == e_row, 1.0, 0.0).astype(BF16)
        picked = jnp.dot(sel, expand, preferred_element_type=F32)
        if diagonal:
            picked = jnp.where(k0 + kcol <= q0 + krow, picked, 0.0)
        valid = picked > 0.5
        bias = (k0 - q0 + rel).astype(F32)
        for h in range(NSA_G):
            qh = q_ref[:, h * HEAD_DIM:(h + 1) * HEAD_DIM]
            s = lax.dot_general(qh, k, nt, preferred_element_type=F32) * c1 + slope2[h] * bias
            _flash_step(jnp.where(valid, s, NEG_INF), v, m_ref, l_ref, acc_ref, h)

    kd = q0 // tk
    sel_tile(kd, True)

    def sel_body(i, carry):
        kt = kd - 1 - i
        pl.when(flag_ref[kt] > 0)(lambda: sel_tile(kt, False))
        return carry

    lax.fori_loop(0, kd, sel_body, 0)
    o_sel = [_flash_result(l_ref, acc_ref, h) for h in range(NSA_G)]

    n_band = WINDOW + tq
    w0 = pl.multiple_of(jnp.maximum(q0 - WINDOW, 0), LANES)
    kw = kw_ref[pl.ds(w0, n_band), :]
    vw = vw_ref[pl.ds(w0, n_band), :]
    wdist = (q0 + lax.broadcasted_iota(jnp.int32, (tq, n_band), 0)) - (
        w0 + lax.broadcasted_iota(jnp.int32, (tq, n_band), 1))
    wvalid = (wdist >= 0) & (wdist <= WINDOW)
    wbias = (w0 - q0 + lax.broadcasted_iota(jnp.int32, (1, n_band), 1)).astype(F32)
    o_win = []
    for h in range(NSA_G):
        qh = q_ref[:, h * HEAD_DIM:(h + 1) * HEAD_DIM]
        s = lax.dot_general(qh, kw, nt, preferred_element_type=F32) * c1 + slope2[h] * wbias
        o_win.append(_softmax_pv(jnp.where(wvalid, s, NEG_INF), vw))

    gates = jax.nn.sigmoid(gate_ref[...])
    for h in range(NSA_G):
        o = (gates[:, 3 * h:3 * h + 1] * o_cmp[h] + gates[:, 3 * h + 1:3 * h + 2] * o_sel[h]
             + gates[:, 3 * h + 2:3 * h + 3] * o_win[h])
        o_ref[:, h * HEAD_DIM:(h + 1) * HEAD_DIM] = o.astype(o_ref.dtype)


def nsa_prompt(qn_bf, gate_pre, kcvc, sel_bf, win_bf, slopes, batch, seq, tq=NSA_TQ, tk=NSA_TK):
    assert tk % tq == 0 and seq % tk == 0 and seq >= WINDOW + tq and seq // CMP_BLOCK <= LANES
    nq = seq // tq
    gw = NSA_G * HEAD_DIM

    def kv_spec(kv):
        return pl.BlockSpec((seq, HEAD_DIM), lambda b, g, i: (b, NSA_KVH * kv + g))

    return pl.pallas_call(
        functools.partial(_nsa_prompt_kernel, tq=tq, tk=tk, n_blocks=seq // CMP_BLOCK),
        grid=(batch, NSA_KVH, nq),
        in_specs=[pl.BlockSpec(memory_space=pltpu.SMEM),
                  pl.BlockSpec((tq, gw), lambda b, g, i: (b * nq + i, g)),
                  pl.BlockSpec((tq, LANES), lambda b, g, i: (b * nq + i, g)),
                  pl.BlockSpec((1, LANES, HEAD_DIM), lambda b, g, i: (b, 0, g)),
                  pl.BlockSpec((1, LANES, HEAD_DIM), lambda b, g, i: (b, 0, 2 + g)),
                  kv_spec(0), kv_spec(1), kv_spec(0), kv_spec(1)],
        out_specs=pl.BlockSpec((tq, gw), lambda b, g, i: (b * nq + i, g)),
        out_shape=jax.ShapeDtypeStruct((batch * seq, NSA_Q_W), BF16),
        scratch_shapes=[pltpu.VMEM((NSA_G, tq, LANES), F32), pltpu.VMEM((NSA_G, tq, LANES), F32),
                        pltpu.VMEM((NSA_G, tq, HEAD_DIM), F32), pltpu.SMEM((seq // tk,), jnp.int32)],
        compiler_params=_cparams(("parallel", "parallel", "arbitrary")),
        name="nsa_prompt",
    )(slopes, qn_bf, gate_pre, kcvc, kcvc, sel_bf, sel_bf, win_bf, win_bf)


def _diff_lambda(lv, lam_init):
    a = jnp.sum(lv[0:1] * lv[1:2], axis=-1, keepdims=True)
    b = jnp.sum(lv[2:3] * lv[3:4], axis=-1, keepdims=True)
    return jnp.exp(a) - jnp.exp(b) + lam_init


def _diff_prompt_kernel(slopes_ref, q_ref, k_ref, v_ref, lam_ref, dn_ref, o_ref, m_ref, l_ref, acc_ref,
                        *, tq, tk, lam_init):
    h = pl.program_id(1)
    qi = pl.program_id(2)
    q0 = qi * tq
    krow = lax.broadcasted_iota(jnp.int32, (tq, tk), 0)
    kcol = lax.broadcasted_iota(jnp.int32, (tq, tk), 1)
    rel = lax.broadcasted_iota(jnp.int32, (1, tk), 1)
    c1 = ATT_SCALE * LOG2E
    slope2 = slopes_ref[h] * LOG2E
    nt = (((1,), (1,)), ((), ()))
    _flash_init(m_ref, l_ref, acc_ref)

    def tile(kt, diagonal):
        k0 = pl.multiple_of(kt * tk, tk)
        k = k_ref[pl.ds(k0, tk), :]
        v = v_ref[pl.ds(k0, tk), :]
        bias = slope2 * (k0 - q0 + rel).astype(F32)
        for c in range(2):
            s = lax.dot_general(q_ref[:, c * HEAD_DIM:(c + 1) * HEAD_DIM], k[:, c * HEAD_DIM:(c + 1) * HEAD_DIM],
                                nt, preferred_element_type=F32) * c1 + bias
            if diagonal:
                s = jnp.where(k0 + kcol <= q0 + krow, s, NEG_INF)
            _flash_step(s, v, m_ref, l_ref, acc_ref, c)

    kd = q0 // tk
    tile(kd, True)

    def body(i, carry):
        tile(kd - 1 - i, False)
        return carry

    lax.fori_loop(0, kd, body, 0)
    lam = _diff_lambda(lam_ref[...], lam_init)
    o = _flash_result(l_ref, acc_ref, 0) - lam * _flash_result(l_ref, acc_ref, 1)
    y = o * lax.rsqrt(jnp.mean(o * o, axis=-1, keepdims=True) + NORM_EPS)
    o_ref[...] = (y * dn_ref[...] * (1.0 - lam_init)).astype(o_ref.dtype)


def diff_prompt(qd_bf, kvd_bf, lam_vec, dnorm, slopes, lam_init, batch, seq, tq=DIFF_TQ, tk=DIFF_TK):
    assert tk % tq == 0 and seq % tk == 0
    nq = seq // tq
    w = 2 * HEAD_DIM
    qb, kb, vb = 0, 0, DIFF_H
    return pl.pallas_call(
        functools.partial(_diff_prompt_kernel, tq=tq, tk=tk, lam_init=lam_init),
        grid=(batch, DIFF_H, nq),
        in_specs=[pl.BlockSpec(memory_space=pltpu.SMEM),
                  pl.BlockSpec((tq, w), lambda b, h, i: (b * nq + i, qb + h)),
                  pl.BlockSpec((seq, w), lambda b, h, i: (b, kb + h)),
                  pl.BlockSpec((seq, w), lambda b, h, i: (b, vb + h)),
                  pl.BlockSpec((4, HEAD_DIM), lambda b, h, i: (0, 0)),
                  pl.BlockSpec((1, DIFF_VD), lambda b, h, i: (0, 0))],
        out_specs=pl.BlockSpec((tq, w), lambda b, h, i: (b * nq + i, h)),
        out_shape=jax.ShapeDtypeStruct((batch * seq, DIFF_V_W), BF16),
        scratch_shapes=[pltpu.VMEM((2, tq, LANES), F32), pltpu.VMEM((2, tq, LANES), F32),
                        pltpu.VMEM((2, tq, DIFF_VD), F32)],
        compiler_params=_cparams(("parallel", "parallel", "arbitrary")),
        name="diff_prompt",
    )(slopes, qd_bf, kvd_bf, kvd_bf, lam_vec, dnorm.reshape(1, DIFF_VD))


def _ret_prompt_kernel(lg_ref, q_ref, k_ref, v_ref, g_ref, rn_ref, o_ref, st_ref, *, chunk):
    h = pl.program_id(1)
    c = pl.program_id(2)
    lg = lg_ref[h]

    @pl.when(c == 0)
    def _():
        st_ref[...] = jnp.zeros(st_ref.shape, F32)

    ii = lax.broadcasted_iota(jnp.int32, (chunk, chunk), 0)
    jj = lax.broadcasted_iota(jnp.int32, (chunk, chunk), 1)
    d = (ii - jj).astype(F32)
    decay = jnp.where(d >= 0, jnp.exp(lg * jnp.maximum(d, 0.0)), 0.0)
    ik = lax.broadcasted_iota(jnp.int32, (chunk, RET_DK), 0).astype(F32)
    q_dec = jnp.exp(lg * (ik + 1.0))
    k_dec = jnp.exp(lg * (chunk - 1.0 - ik))
    ones = jnp.ones((1, 1), F32)
    g_chunk = jnp.exp(ones * (lg * chunk))

    q = q_ref[...]
    k = k_ref[...] * (RET_DK ** -0.5)
    v = v_ref[...].astype(BF16)
    state = st_ref[0, 0]
    s = lax.dot_general(q.astype(BF16), k.astype(BF16), (((1,), (1,)), ((), ())),
                        preferred_element_type=F32) * decay
    o = jnp.dot(s.astype(BF16), v, preferred_element_type=F32)
    o = o + jnp.dot((q * q_dec).astype(BF16), state.astype(BF16), preferred_element_type=F32)
    kv = lax.dot_general((k * k_dec).astype(BF16), v, (((0,), (0,)), ((), ())), preferred_element_type=F32)
    st_ref[0, 0] = state * g_chunk + kv

    y = o * lax.rsqrt(jnp.mean(o * o, axis=-1, keepdims=True) + NORM_EPS) * rn_ref[0]
    gate = g_ref[...]
    o_ref[...] = (gate * jax.nn.sigmoid(gate) * y).astype(o_ref.dtype)


def ret_prompt(proj, rnorm, log_g, batch, seq, chunk=RET_BLOCK):
    nc = seq // chunk
    kb = RET_H
    vb = 2 * RET_H * RET_DK // RET_DV
    gb = vb + RET_H
    return pl.pallas_call(
        functools.partial(_ret_prompt_kernel, chunk=chunk),
        grid=(batch, RET_H, nc),
        in_specs=[pl.BlockSpec(memory_space=pltpu.SMEM),
                  pl.BlockSpec((chunk, RET_DK), lambda b, h, c: (b * nc + c, h)),
                  pl.BlockSpec((chunk, RET_DK), lambda b, h, c: (b * nc + c, kb + h)),
                  pl.BlockSpec((chunk, RET_DV), lambda b, h, c: (b * nc + c, vb + h)),
                  pl.BlockSpec((chunk, RET_DV), lambda b, h, c: (b * nc + c, gb + h)),
                  pl.BlockSpec((1, 1, RET_DV), lambda b, h, c: (h, 0, 0))],
        out_specs=[pl.BlockSpec((chunk, RET_DV), lambda b, h, c: (b * nc + c, h)),
                   pl.BlockSpec((1, 1, RET_DK, RET_DV), lambda b, h, c: (b, h, 0, 0))],
        out_shape=[jax.ShapeDtypeStruct((batch * seq, RET_H * RET_DV), BF16),
                   jax.ShapeDtypeStruct((batch, RET_H, RET_DK, RET_DV), F32)],
        compiler_params=_cparams(("parallel", "parallel", "arbitrary")),
        name="ret_prompt",
    )(log_g, proj, proj, proj, proj, rnorm.reshape(RET_H, 1, RET_DV))


NT_DIMS = (((1,), (1,)), ((), ()))


def _compress_sample_kernel(pt_ref, pos_ref, *refs, pp):
    del pt_ref
    o_ref = refs[pp]
    per_page = PAGE_SIZE // CMP_BLOCK
    n_rows = PAGE_SIZE * NSA_KVH
    row = lax.broadcasted_iota(jnp.int32, (SUBLANES, n_rows), 0)
    col = lax.broadcasted_iota(jnp.int32, (SUBLANES, n_rows), 1)
    g_shift = NSA_KVH.bit_length() - 1
    member = ((col & (NSA_KVH - 1)) == (row & (NSA_KVH - 1))) & ((col >> (g_shift + CMP_SHIFT)) == (row >> g_shift))
    weights = []
    for kv in range(2):
        logits = jnp.where(member, pos_ref[kv], NEG_INF)
        e = jnp.exp(logits - jnp.max(logits, axis=-1, keepdims=True))
        weights.append((e / jnp.sum(e, axis=-1, keepdims=True)).astype(BF16))
    for i in range(pp):
        for kv in range(2):
            x = refs[i][0, 0, :, kv].reshape(n_rows, HEAD_DIM).astype(BF16)
            res = jnp.dot(weights[kv], x, preferred_element_type=F32)
            for half in range(per_page):
                for g in range(NSA_KVH):
                    c0 = (kv * NSA_KVH + g) * HEAD_DIM
                    o_ref[0, i * per_page + half:i * per_page + half + 1, c0:c0 + HEAD_DIM] = (
                        res[half * NSA_KVH + g:half * NSA_KVH + g + 1])


def compress_sample(cache, layer, page_table, pos_rows, pp=16):
    db, n_pages = page_table.shape
    per_page = PAGE_SIZE // CMP_BLOCK
    width = 2 * NSA_KVH * HEAD_DIM
    assert per_page * NSA_KVH <= SUBLANES

    def page_spec(i):
        return pl.BlockSpec((1, 1) + cache.shape[2:], lambda b, j, pt: (layer, pt[b, j * pp + i], 0, 0, 0, 0))

    return pl.pallas_call(
        functools.partial(_compress_sample_kernel, pp=pp),
        grid_spec=pltpu.PrefetchScalarGridSpec(
            num_scalar_prefetch=1,
            grid=(db, n_pages // pp),
            in_specs=[pl.BlockSpec(pos_rows.shape, lambda b, j, pt: (0, 0, 0))] + [page_spec(i) for i in range(pp)],
            out_specs=pl.BlockSpec((1, pp * per_page, width), lambda b, j, pt: (b, j, 0))),
        out_shape=jax.ShapeDtypeStruct((db, n_pages * per_page, width), F32),
        compiler_params=_cparams(("parallel", "arbitrary")),
        name="nsa_compress_sample",
    )(page_table, pos_rows, *([cache] * pp))


def _pick_group(x):
    row = lax.broadcasted_iota(jnp.int32, (NSA_H, HEAD_DIM), 0)
    return jnp.where(row < NSA_G, x[:, :HEAD_DIM], x[:, HEAD_DIM:])


def _nsa_decode_cmp_kernel(q_ref, slope_ref, kcvc_ref, win_ref, ocmp_ref, owin_ref, idx_ref, *, past, n_win):
    q8 = q_ref[0]
    kcvc = kcvc_ref[0]
    nb = kcvc.shape[0]
    kvw = NSA_KVH * HEAD_DIM
    lane = lax.broadcasted_iota(jnp.int32, (NSA_H, nb), 1)
    row = lax.broadcasted_iota(jnp.int32, (NSA_H, nb), 0)
    slope = slope_ref[...][:, :1]
    cdist = (past - (lane * CMP_BLOCK + CMP_BLOCK - 1)).astype(F32)
    s = lax.dot_general(q8, kcvc[:, :kvw].astype(BF16), NT_DIMS, preferred_element_type=F32) * ATT_SCALE
    s = s - slope * cdist
    p = jnp.exp(s - jnp.max(s, axis=-1, keepdims=True))
    p = p / jnp.sum(p, axis=-1, keepdims=True)
    ocmp_ref[0] = _pick_group(jnp.dot(p.astype(BF16), kcvc[:, kvw:].astype(BF16), preferred_element_type=F32))

    g0 = p[0:1] + p[1:2] + p[2:3] + p[3:4]
    g1 = p[4:5] + p[5:6] + p[6:7] + p[7:8]
    x = jnp.where(row < NSA_G, g0, g1)
    x = jnp.where(lane == 0, FORCE_SCORE, x)
    out_lane = lax.broadcasted_iota(jnp.int32, (NSA_H, LANES), 1)
    lane_f = lane.astype(F32)
    picked = jnp.zeros((NSA_H, LANES), F32)
    for t in range(TOP_N - 1):
        mx = jnp.max(x, axis=-1, keepdims=True)
        idx = jnp.min(jnp.where(x == mx, lane_f, float(nb)), axis=-1, keepdims=True)
        picked = jnp.where(out_lane == t, idx, picked)
        x = jnp.where(lane_f == idx, -2.0, x)
    idx_ref[0] = picked.astype(jnp.int32)

    win = win_ref[0]
    nw = win.shape[0]
    wl = lax.broadcasted_iota(jnp.int32, (NSA_H, nw), 1)
    wdist = (n_win - 1 - wl)
    valid = wdist >= 0
    s = lax.dot_general(q8, win[:, :kvw].astype(BF16), NT_DIMS, preferred_element_type=F32) * ATT_SCALE
    s = jnp.where(valid, s - slope * wdist.astype(F32), NEG_INF)
    p = jnp.exp(s - jnp.max(s, axis=-1, keepdims=True))
    p = p / jnp.sum(p, axis=-1, keepdims=True)
    owin_ref[0] = _pick_group(jnp.dot(p.astype(BF16), win[:, kvw:].astype(BF16), preferred_element_type=F32))


def nsa_decode_cmp(q8, slope8, kcvc, win_all, past, n_win):
    db = q8.shape[0]
    nb = kcvc.shape[1]
    nw = win_all.shape[1]
    w = kcvc.shape[2]
    head_out = jax.ShapeDtypeStruct((db, NSA_H, HEAD_DIM), F32)
    head_spec = pl.BlockSpec((1, NSA_H, HEAD_DIM), lambda b: (b, 0, 0))
    return pl.pallas_call(
        functools.partial(_nsa_decode_cmp_kernel, past=past, n_win=n_win),
        grid=(db,),
        in_specs=[pl.BlockSpec((1, NSA_H, NSA_KVH * HEAD_DIM), lambda b: (b, 0, 0)),
                  pl.BlockSpec((NSA_H, LANES), lambda b: (0, 0)),
                  pl.BlockSpec((1, nb, w), lambda b: (b, 0, 0)),
                  pl.BlockSpec((1, nw, w), lambda b: (b, 0, 0))],
        out_specs=[head_spec, head_spec, pl.BlockSpec((1, NSA_H, LANES), lambda b: (b, 0, 0))],
        out_shape=[head_out, head_out, jax.ShapeDtypeStruct((db, NSA_H, LANES), jnp.int32)],
        compiler_params=_cparams(("parallel",)),
        name="nsa_decode_cmp",
    )(q8, slope8, kcvc, win_all)


def _nsa_decode_sel_kernel(pt_ref, ids_ref, q_ref, slope_ref, new_ref, pa_ref, pb_ref, ocmp_ref, owin_ref,
                           gate_ref, o_ref, m_ref, l_ref, acc_ref, *, past, n_sel):
    del pt_ref
    b = pl.program_id(0)
    t = pl.program_id(1)
    q8 = q_ref[0]
    row = lax.broadcasted_iota(jnp.int32, (NSA_H, LANES), 0)
    lane = lax.broadcasted_iota(jnp.int32, (NSA_H, LANES), 1)
    slope = slope_ref[...][:, :1]
    kvw = NSA_KVH * HEAD_DIM

    @pl.when(t == 0)
    def _():
        new = new_ref[0].astype(BF16).astype(F32)
        m_ref[...] = jnp.sum(q8.astype(F32) * new[:, :kvw], axis=-1, keepdims=True) * ATT_SCALE + jnp.zeros(
            (NSA_H, LANES), F32)
        l_ref[...] = jnp.ones((NSA_H, LANES), F32)
        acc_ref[...] = jnp.where(row < NSA_G, new[:, kvw:kvw + HEAD_DIM], new[:, kvw + HEAD_DIM:])

    na = ids_ref[b, t]
    nbk = ids_ref[b, n_sel + t]
    blk = jnp.where(row < NSA_G, na, nbk)
    per_page = PAGE_SIZE // CMP_BLOCK
    valid = (lane >> CMP_SHIFT) == (blk & (per_page - 1))
    dist = past - ((blk >> (per_page.bit_length() - 1)) * PAGE_SIZE + lane)
    sa = lax.dot_general(q8[:, :HEAD_DIM], pa_ref[0, 0, :, 0, 0, :].astype(BF16), NT_DIMS,
                         preferred_element_type=F32)
    sb = lax.dot_general(q8[:, HEAD_DIM:], pb_ref[0, 0, :, 0, 1, :].astype(BF16), NT_DIMS,
                         preferred_element_type=F32)
    s = jnp.where(row < NSA_G, sa, sb) * ATT_SCALE - slope * dist.astype(F32)
    s = jnp.where(valid, s, NEG_INF)
    m_prev = m_ref[...]
    m_new = jnp.maximum(m_prev, jnp.max(s, axis=-1, keepdims=True))
    alpha = jnp.exp(m_prev - m_new)
    p = jnp.exp(s - m_new)
    l_ref[...] = alpha * l_ref[...] + jnp.sum(p, axis=-1, keepdims=True)
    m_ref[...] = m_new
    pb16 = p.astype(BF16)
    oa = jnp.dot(pb16, pa_ref[0, 0, :, 1, 0, :].astype(BF16), preferred_element_type=F32)
    ob = jnp.dot(pb16, pb_ref[0, 0, :, 1, 1, :].astype(BF16), preferred_element_type=F32)
    acc_ref[...] = alpha * acc_ref[...] + jnp.where(row < NSA_G, oa, ob)

    @pl.when(t == n_sel - 1)
    def _():
        gates = jax.nn.sigmoid(gate_ref[0])
        o = (gates[:, 0:1] * ocmp_ref[0] + gates[:, 1:2] * (acc_ref[...] / l_ref[...])
             + gates[:, 2:3] * owin_ref[0])
        o_ref[0] = o.astype(o_ref.dtype)


def nsa_decode_sel(cache, layer, page_table, ids, q8, slope8, new_row, o_cmp, o_win, gates, past):
    db = q8.shape[0]
    n_sel = ids.shape[1] // NSA_KVH
    width = 2 * NSA_KVH * HEAD_DIM
    per_page = PAGE_SIZE // CMP_BLOCK
    head_spec = pl.BlockSpec((1, NSA_H, HEAD_DIM), lambda b, t, pt, ids: (b, 0, 0))

    def page_spec(g):
        return pl.BlockSpec((1, 1) + cache.shape[2:],
                            lambda b, t, pt, ids: (layer, pt[b, ids[b, g * n_sel + t] // per_page], 0, 0, 0, 0))

    return pl.pallas_call(
        functools.partial(_nsa_decode_sel_kernel, past=past, n_sel=n_sel),
        grid_spec=pltpu.PrefetchScalarGridSpec(
            num_scalar_prefetch=2,
            grid=(db, n_sel),
            in_specs=[pl.BlockSpec((1, NSA_H, NSA_KVH * HEAD_DIM), lambda b, t, pt, ids: (b, 0, 0)),
                      pl.BlockSpec((NSA_H, LANES), lambda b, t, pt, ids: (0, 0)),
                      pl.BlockSpec((1, 1, width), lambda b, t, pt, ids: (b, 0, 0)),
                      page_spec(0), page_spec(1), head_spec, head_spec, head_spec],
            out_specs=head_spec,
            scratch_shapes=[pltpu.VMEM((NSA_H, LANES), F32), pltpu.VMEM((NSA_H, LANES), F32),
                            pltpu.VMEM((NSA_H, HEAD_DIM), F32)]),
        out_shape=jax.ShapeDtypeStruct((db, NSA_H, HEAD_DIM), BF16),
        compiler_params=_cparams(("parallel", "arbitrary")),
        name="nsa_decode_sel",
    )(page_table, ids, q8, slope8, new_row, cache, cache, o_cmp, o_win, gates)


def _diff_decode_kernel(pt_ref, q_ref, slope_ref, new_ref, lam_ref, dn_ref, *refs, pp, past, lam_init):
    del pt_ref
    page_refs = refs[:pp]
    o_ref, m_ref, l_ref, acc_ref = refs[pp:]
    j = pl.program_id(1)
    rows = 2 * DIFF_H
    n_keys = PAGE_SIZE * DIFF_H
    h_shift = DIFF_H.bit_length() - 1
    q8 = q_ref[0]
    slope2 = slope_ref[...][:, :1] * LOG2E
    row = lax.broadcasted_iota(jnp.int32, (rows, n_keys), 0)
    col = lax.broadcasted_iota(jnp.int32, (rows, n_keys), 1)
    own_head = (col & (DIFF_H - 1)) == (row & (DIFF_H - 1))
    key_in_page = (col >> h_shift).astype(F32)

    @pl.when(j == 0)
    def _():
        k_new = jnp.concatenate([new_ref[0, 0]] * 2, axis=0).astype(BF16).astype(F32)
        m_ref[0] = jnp.sum(q8.astype(F32) * k_new, axis=-1, keepdims=True) * (ATT_SCALE * LOG2E) + jnp.zeros(
            (rows, LANES), F32)
        l_ref[0] = jnp.ones((rows, LANES), F32)
        acc_ref[0] = jnp.concatenate([new_ref[0, 1]] * 2, axis=0).astype(BF16).astype(F32)

    scores, values = [], []
    for i in range(pp):
        page = page_refs[i]
        k0 = (j * pp + i) * PAGE_SIZE
        kx = page[0, 0, :, 0].reshape(n_keys, DIFF_VD).astype(BF16)
        values.append(page[0, 0, :, 1].reshape(n_keys, DIFF_VD).astype(BF16))
        s = lax.dot_general(q8, kx, NT_DIMS, preferred_element_type=F32) * (ATT_SCALE * LOG2E)
        s = s - slope2 * ((past - k0).astype(F32) - key_in_page)
        scores.append(jnp.where(own_head, s, NEG_INF))
    _flash_step(scores, values, m_ref, l_ref, acc_ref, 0)

    @pl.when(j == pl.num_programs(1) - 1)
    def _():
        lam = _diff_lambda(lam_ref[...], lam_init)
        o = _flash_result(l_ref, acc_ref, 0)
        oh = o[:DIFF_H] - lam * o[DIFF_H:]
        y = oh * lax.rsqrt(jnp.mean(oh * oh, axis=-1, keepdims=True) + NORM_EPS)
        o_ref[0] = (y * dn_ref[...] * (1.0 - lam_init)).astype(o_ref.dtype)


def diff_decode(cache, layer, page_table, q8, slope8, new_row, lam_vec, dnorm, lam_init, past, pp=8):
    db, n_pages = page_table.shape
    rows = 2 * DIFF_H

    def page_spec(i):
        return pl.BlockSpec((1, 1) + cache.shape[2:], lambda b, j, pt: (layer, pt[b, j * pp + i], 0, 0, 0, 0))

    return pl.pallas_call(
        functools.partial(_diff_decode_kernel, pp=pp, past=past, lam_init=lam_init),
        grid_spec=pltpu.PrefetchScalarGridSpec(
            num_scalar_prefetch=1,
            grid=(db, n_pages // pp),
            in_specs=[pl.BlockSpec((1, rows, DIFF_VD), lambda b, j, pt: (b, 0, 0)),
                      pl.BlockSpec((rows, LANES), lambda b, j, pt: (0, 0)),
                      pl.BlockSpec((1, 2, DIFF_H, DIFF_VD), lambda b, j, pt: (b, 0, 0, 0)),
                      pl.BlockSpec((4, HEAD_DIM), lambda b, j, pt: (0, 0)),
                      pl.BlockSpec((1, DIFF_VD), lambda b, j, pt: (0, 0))] + [page_spec(i) for i in range(pp)],
            out_specs=pl.BlockSpec((1, DIFF_H, DIFF_VD), lambda b, j, pt: (b, 0, 0)),
            scratch_shapes=[pltpu.VMEM((1, rows, LANES), F32), pltpu.VMEM((1, rows, LANES), F32),
                            pltpu.VMEM((1, rows, DIFF_VD), F32)]),
        out_shape=jax.ShapeDtypeStruct((db, DIFF_H, DIFF_VD), BF16),
        compiler_params=_cparams(("parallel", "arbitrary")),
        name="diff_decode",
    )(page_table, q8, slope8, new_row, lam_vec, dnorm.reshape(1, DIFF_VD), *([cache] * pp))


def _ret_decode_kernel(lg_ref, q_ref, k_ref, v_ref, g_ref, rn_ref, st_ref, o_ref, nst_ref):
    ii = lax.broadcasted_iota(jnp.int32, (RET_DK, RET_DK), 0)
    jj = lax.broadcasted_iota(jnp.int32, (RET_DK, RET_DK), 1)
    ones = jnp.ones((1, 1), F32)
    for h in range(RET_H):
        gamma = jnp.exp(ones * lg_ref[h])
        q = q_ref[0, h:h + 1, :]
        k = k_ref[0, h:h + 1, :] * (RET_DK ** -0.5)
        v = v_ref[0, h:h + 1, :]
        state = st_ref[0, h]
        qb = q.astype(BF16).astype(F32)
        kb = k.astype(BF16).astype(F32)
        s = jnp.sum(qb * kb, axis=-1, keepdims=True)
        q_dec = jnp.broadcast_to(q * gamma, (SUBLANES, RET_DK)).astype(BF16)
        o = s * v + jnp.dot(q_dec, state.astype(BF16), preferred_element_type=F32)[0:1]
        k_col = jnp.sum(jnp.where(ii == jj, k, 0.0), axis=-1, keepdims=True)
        nst_ref[0, h] = state * gamma + k_col * v
        y = o * lax.rsqrt(jnp.mean(o * o, axis=-1, keepdims=True) + NORM_EPS) * rn_ref[h]
        gate = g_ref[0, h:h + 1, :]
        o_ref[0, h:h + 1, :] = (gate * jax.nn.sigmoid(gate) * y).astype(o_ref.dtype)


def ret_decode(q, k, v, g, rnorm, state, log_g):
    db = q.shape[0]
    return pl.pallas_call(
        _ret_decode_kernel,
        grid=(db,),
        in_specs=[pl.BlockSpec(memory_space=pltpu.SMEM),
                  pl.BlockSpec((1, RET_H, RET_DK), lambda b: (b, 0, 0)),
                  pl.BlockSpec((1, RET_H, RET_DK), lambda b: (b, 0, 0)),
                  pl.BlockSpec((1, RET_H, RET_DV), lambda b: (b, 0, 0)),
                  pl.BlockSpec((1, RET_H, RET_DV), lambda b: (b, 0, 0)),
                  pl.BlockSpec((RET_H, 1, RET_DV), lambda b: (0, 0, 0)),
                  pl.BlockSpec((1, RET_H, RET_DK, RET_DV), lambda b: (b, 0, 0, 0))],
        out_specs=[pl.BlockSpec((1, RET_H, RET_DV), lambda b: (b, 0, 0)),
                   pl.BlockSpec((1, RET_H, RET_DK, RET_DV), lambda b: (b, 0, 0, 0))],
        out_shape=[jax.ShapeDtypeStruct((db, RET_H, RET_DV), BF16),
                   jax.ShapeDtypeStruct(state.shape, F32)],
        compiler_params=_cparams(("parallel",)),
        name="ret_decode",
    )(log_g, q, k, v, g, rnorm.reshape(RET_H, 1, RET_DV), state)


def _alibi_slopes(n):
    return jnp.asarray([2.0 ** (-8.0 * (i + 1) / n) for i in range(n)], dtype=F32)


def _even_layer(e, layer, hp, hs, gain, batch, seq, caches, page_table, even_w_in, cmp_pos, lam_vec, dnorm, stacks):
    cache_cmp, cache_sel, cache_win, cache_diff = caches
    n_even = even_w_in.shape[0]
    d = even_w_in.shape[1]
    lam_init = 0.8 - 0.6 * math.exp(-0.3 * layer)
    db = page_table.shape[0]
    past = page_table.shape[1] * PAGE_SIZE
    nsa_slopes = _alibi_slopes(NSA_H)
    diff_slopes = _alibi_slopes(DIFF_H)
    kvw = 2 * NSA_KV_W
    per_group = NSA_G * 3
    w_gate = even_w_in[e, :, GATE_OFF:GATE_OFF + NSA_GATE_W].reshape(d, NSA_KVH, per_group)
    w_gate = jnp.pad(w_gate, ((0, 0), (0, 0), (0, LANES - per_group))).reshape(d, NSA_KVH * LANES)
    w_tail = even_w_in[e, :, GATE_OFF + NSA_GATE_W:]
    rn = gain is not None
    if rn:
        w_gate, w_tail = w_gate * gain[:, None], w_tail * gain[:, None]
    w_gate, w_tail = w_gate.astype(BF16), w_tail.astype(BF16)
    nrm = dict(gain=gain, row_norm=rn)
    nrm_folded = dict(row_norm=rn)

    qn_bf, qn_s = dense(hp, even_w_in, NSA_Q_W, lead=e, out_dtypes=(BF16,), xs=hs, **nrm)
    (cmp_st, cmp_bf), (new_cmp, _) = dense(hp, even_w_in, kvw, col0=NSA_Q_W, lead=e, out_dtypes=(F32, BF16),
                                           stack=(n_even, e, stacks[0]), xs=hs, **nrm)
    (sel_st, sel_bf), (new_sel, _) = dense(hp, even_w_in, kvw, col0=NSA_Q_W + kvw, lead=e,
                                           out_dtypes=(F32, BF16), stack=(n_even, e, stacks[1]), xs=hs, **nrm)
    (win_f, win_bf), (new_win, _) = dense(hp, even_w_in, kvw, col0=NSA_Q_W + 2 * kvw, lead=e,
                                          out_dtypes=(F32, BF16), xs=hs, **nrm)
    gate_pre, gate_s = dense(hp, w_gate, NSA_KVH * LANES, xs=hs, **nrm_folded)
    qd_bf, qd_s = dense(hp, w_tail, DIFF_Q_W, out_dtypes=(BF16,), xs=hs, **nrm_folded)
    (kvd_st, kvd_bf), (new_diff, _) = dense(hp, w_tail, DIFF_Q_W + DIFF_V_W, col0=DIFF_Q_W,
                                            out_dtypes=(F32, BF16), stack=(n_even, e, stacks[2]), xs=hs,
                                            **nrm_folded)
    qn_s, new_cmp, new_sel, new_win, gate_s, qd_s, new_diff = (
        t[:db] for t in (qn_s, new_cmp, new_sel, new_win, gate_s, qd_s, new_diff))

    pos_cg = jnp.transpose(cmp_pos, (0, 2, 1)).reshape(2 * NSA_KVH, CMP_BLOCK)
    kcvc = compress_prompt(cmp_bf, jnp.tile(pos_cg, (1, seq // CMP_BLOCK)), batch, seq)
    o_nsa = nsa_prompt(qn_bf, gate_pre, kcvc, sel_bf, win_bf, nsa_slopes, batch, seq)
    o_diff = diff_prompt(qd_bf, kvd_bf, lam_vec, dnorm, diff_slopes, lam_init, batch, seq)
    mix_p = (o_nsa, o_diff)
    keep_p = min(WINDOW, seq)
    win_p = win_f.reshape(batch, seq, kvw)[:, seq - keep_p:].reshape(batch, keep_p, 2, NSA_KVH, HEAD_DIM)

    gates8 = gate_s.reshape(db, NSA_KVH, LANES)[:, :, :per_group].reshape(db, NSA_H, 3)
    gates8 = jnp.pad(gates8, ((0, 0), (0, 0), (0, LANES - 3)))
    group_of_head = jnp.asarray(np.arange(NSA_H) // NSA_G)
    onehot_g = jax.nn.one_hot(group_of_head, NSA_KVH, dtype=BF16)
    qn = qn_s.reshape(db, NSA_H, HEAD_DIM)
    q8 = (qn[:, :, None, :] * onehot_g[None, :, :, None]).reshape(db, NSA_H, NSA_KVH * HEAD_DIM)
    slope8 = jnp.broadcast_to(nsa_slopes[:, None], (NSA_H, LANES))
    pos_rg = jnp.tile(jnp.transpose(cmp_pos, (0, 2, 1)), (1, SUBLANES // NSA_KVH, PAGE_SIZE // CMP_BLOCK))
    pos_rows = jnp.repeat(pos_rg, NSA_KVH, axis=2)
    kcvc_s = compress_sample(cache_cmp, e, page_table, pos_rows)
    w_buf = cache_win.shape[2]
    win_all = jnp.concatenate([cache_win[e].reshape(db, w_buf, -1), new_win[:, None, :]], axis=1)
    n_win = w_buf + 1
    win_pad = jnp.pad(win_all, ((0, 0), (0, -n_win % LANES), (0, 0)))
    o_cmp, o_win, idx = nsa_decode_cmp(q8, slope8, kcvc_s, win_pad, past, n_win)
    ids = jnp.concatenate([idx[:, 0, :TOP_N - 1], idx[:, NSA_G, :TOP_N - 1]], axis=1)
    o_nsa_s = nsa_decode_sel(cache_sel, e, page_table, ids, q8, slope8, new_sel[:, None, :], o_cmp, o_win, gates8,
                             past)

    qd = qd_s.reshape(db, DIFF_H, 2, HEAD_DIM)
    eye_c = jnp.eye(2, dtype=BF16)
    q8d = jnp.einsum('bhcd,cj->bchjd', qd, eye_c).reshape(db, 2 * DIFF_H, DIFF_VD)
    slope8d = jnp.broadcast_to(jnp.tile(diff_slopes, 2)[:, None], (2 * DIFF_H, LANES))
    o_diff_s = diff_decode(cache_diff, e, page_table, q8d, slope8d, new_diff.reshape(db, 2, DIFF_H, DIFF_VD),
                           lam_vec, dnorm, lam_init, past)
    pad_rows = ((0, hs.shape[0] - db), (0, 0))
    mix_s = (jnp.pad(o_nsa_s.reshape(db, NSA_Q_W), pad_rows), jnp.pad(o_diff_s.reshape(db, DIFF_V_W), pad_rows))
    kv_s = (db, 1, 2, NSA_KVH, HEAD_DIM)
    keep = min(WINDOW, n_win)
    small = (win_p, new_cmp.reshape(kv_s), new_sel.reshape(kv_s),
             win_all[:, n_win - keep:].reshape(db, keep, 2, NSA_KVH, HEAD_DIM),
             new_diff.reshape(db, 1, 2, DIFF_H, DIFF_VD))
    return mix_p, mix_s, (cmp_st, sel_st, kvd_st), small


def _ret_layer(o, hp, hs, gain, batch, seq, state, ret_w_in, rnorm):
    db = state.shape[0]
    log_g = jnp.log1p(-jnp.exp2(-5.0 - jnp.arange(RET_H, dtype=F32)))
    n_in = ret_w_in.shape[2]
    proj, ps = dense(hp, ret_w_in, n_in, lead=o, xs=hs, gain=gain, row_norm=gain is not None)
    ps = ps[:db]
    gated_p, st_p = ret_prompt(proj, rnorm, log_g, batch, seq)
    hk = RET_H * RET_DK
    hv = RET_H * RET_DV
    q = ps[:, :hk].reshape(db, RET_H, RET_DK)
    k = ps[:, hk:2 * hk].reshape(db, RET_H, RET_DK)
    v = ps[:, 2 * hk:2 * hk + hv].reshape(db, RET_H, RET_DV)
    g = ps[:, 2 * hk + hv:].reshape(db, RET_H, RET_DV)
    gated_s, st_s = ret_decode(q, k, v, g, rnorm, state, log_g)
    return gated_p, jnp.pad(gated_s.reshape(db, hv), ((0, hs.shape[0] - db), (0, 0))), st_p, st_s


def kernel(x_prompt, x_sample, cache_nsa_cmp, cache_nsa_sel, cache_nsa_win, cache_diff, state_ret, page_table,
           norm_mix, norm_ffn, norm_final, even_w_in, even_w_out, nsa_cmp_pos, diff_lambda, diff_norm,
           ret_w_in, ret_norm, ret_w_out, ffn_w13, ffn_w2):
    batch, seq, d = x_prompt.shape
    db = x_sample.shape[0]
    d_ff = ffn_w2.shape[1]
    xp = x_prompt.reshape(batch * seq, d)
    xs = jnp.pad(x_sample.reshape(db, d), ((0, -db % SAMPLE_ROWS), (0, 0)))
    caches = (cache_nsa_cmp, cache_nsa_sel, cache_nsa_win, cache_diff)
    stacks = (None, None, None)
    small = [[] for _ in range(5)]
    ret_p, ret_s = [], []
    both = (F32, BF16)
    for layer in range(DEPTH):
        if layer == 0:
            hp, hs, gain = rmsnorm(xp, norm_mix[0], BF16), rmsnorm(xs, norm_mix[0], BF16), None
        else:
            hp, hs, gain = xp_bf, xs_bf, norm_mix[layer]
        if layer % 2 == 0:
            e = layer // 2
            mix_p, mix_s, stacks, small_e = _even_layer(
                e, layer, hp, hs, gain, batch, seq, caches, page_table, even_w_in, nsa_cmp_pos[e], diff_lambda[e],
                diff_norm[e], stacks)
            for lst, item in zip(small, small_e):
                lst.append(item)
            w_out, lead = even_w_out, e
        else:
            o = layer // 2
            mix_p, mix_s, st_p, st_s = _ret_layer(o, hp, hs, gain, batch, seq, state_ret[o], ret_w_in, ret_norm[o])
            ret_p.append(st_p)
            ret_s.append(st_s)
            w_out, lead = ret_w_out, o
        (xp, xp_bf), (xs, xs_bf) = dense(mix_p, w_out, d, lead=lead, res=xp, xs=mix_s, res_s=xs, out_dtypes=both)
        up_p, up_s = dense(xp_bf, ffn_w13, d_ff, lead=layer, col3=d_ff, out_dtypes=(BF16,), xs=xs_bf,
                           gain=norm_ffn[layer], row_norm=True)
        if layer + 1 < DEPTH:
            (xp, xp_bf), (xs, xs_bf) = dense(up_p, ffn_w2, d, lead=layer, res=xp, xs=up_s, res_s=xs, out_dtypes=both)
        else:
            xp, xs = dense(up_p, ffn_w2, d, lead=layer, res=xp, xs=up_s, res_s=xs)
    y_prompt = rmsnorm(xp, norm_final, F32).reshape(batch, seq, d)
    y_sample = rmsnorm(xs, norm_final, F32)[:db].reshape(db, 1, d)
    n_even = even_w_in.shape[0]
    cmp_st, sel_st, kvd_st = stacks
    win_p, cmp_s, sel_s, win_s, diff_s = [jnp.stack(t) for t in small]
    return (y_prompt, y_sample,
            cmp_st.reshape(n_even, batch, seq, 2, NSA_KVH, HEAD_DIM),
            sel_st.reshape(n_even, batch, seq, 2, NSA_KVH, HEAD_DIM), win_p,
            kvd_st.reshape(n_even, batch, seq, 2, DIFF_H, DIFF_VD), jnp.stack(ret_p),
            cmp_s, sel_s, win_s, diff_s, jnp.stack(ret_s))
```

```python
import functools
import math

import jax
import jax.numpy as jnp
import numpy as np
from jax import lax
from jax.experimental import pallas as pl
from jax.experimental.pallas import tpu as pltpu

F32 = jnp.float32
BF16 = jnp.bfloat16

D_MODEL = 2048
DEPTH = 4
PAGE_SIZE = 128
HEAD_DIM = 128
NSA_H = 8
NSA_KVH = 2
NSA_G = NSA_H // NSA_KVH
CMP_BLOCK = 64
CMP_SHIFT = CMP_BLOCK.bit_length() - 1
TOP_N = 16
WINDOW = 512
FORCE_SCORE = 1.0e4
DIFF_H = 4
DIFF_VD = 2 * HEAD_DIM
RET_H = 8
RET_DK = D_MODEL // RET_H
RET_DV = 2 * D_MODEL // RET_H
RET_CHUNK = 128
D_FF = ((8 * D_MODEL + 3 * 256 - 1) // (3 * 256)) * 256
NORM_EPS = 1e-6
NEG_INF = -1e30
ATT_SCALE = HEAD_DIM ** -0.5
LOG2E = math.log2(math.e)

NSA_Q_W = NSA_H * HEAD_DIM
NSA_KV_W = NSA_KVH * HEAD_DIM
NSA_GATE_W = NSA_H * 3
DIFF_Q_W = DIFF_H * 2 * HEAD_DIM
DIFF_V_W = DIFF_H * DIFF_VD
MAIN_W = NSA_Q_W + 6 * NSA_KV_W + 2 * DIFF_Q_W + DIFF_V_W
GATE_OFF = NSA_Q_W + 6 * NSA_KV_W

NSA_TQ, NSA_TK = 256, 256
DIFF_TQ, DIFF_TK = 512, 512
RET_BLOCK = 512

LANES = 128
SUBLANES = 8
SAMPLE_ROWS = 2 * SUBLANES
VMEM_LIMIT = 56 * 1024 * 1024


def _cparams(sem):
    return pltpu.CompilerParams(dimension_semantics=sem, vmem_limit_bytes=VMEM_LIMIT)


def _rmsnorm_kernel(x_ref, g_ref, o_ref):
    x = x_ref[...]
    y = x * lax.rsqrt(jnp.mean(x * x, axis=-1, keepdims=True) + NORM_EPS)
    o_ref[...] = (y * g_ref[...]).astype(o_ref.dtype)


def rmsnorm(x, g, out_dtype):
    m, d = x.shape
    tm = min(512, m)
    return pl.pallas_call(
        _rmsnorm_kernel,
        grid=(m // tm,),
        in_specs=[pl.BlockSpec((tm, d), lambda i: (i, 0)), pl.BlockSpec((1, d), lambda i: (0, 0))],
        out_specs=pl.BlockSpec((tm, d), lambda i: (i, 0)),
        out_shape=jax.ShapeDtypeStruct((m, d), out_dtype),
        compiler_params=_cparams(("parallel",)),
        name="rmsnorm",
    )(x, g.reshape(1, d))


def _dense_kernel(*refs, n_x, n_w, has_gain, row_norm, has_res, has_prev, has_s, n_out, cast_w, n_i, init_slot):
    refs = list(refs)
    x_refs = [refs.pop(0) for _ in range(n_x)]
    xs_refs = [refs.pop(0) for _ in range(n_x)] if has_s else []
    w_refs = [refs.pop(0) for _ in range(n_w)]
    gain_ref = refs.pop(0) if has_gain else None
    res_ref = refs.pop(0) if has_res else None
    res_s_ref = refs.pop(0) if (has_res and has_s) else None
    if has_prev:
        refs.pop(0)
    out_refs = [refs.pop(0) for _ in range(n_out)]
    out_s_refs = [refs.pop(0) for _ in range(n_out)] if has_s else []
    wb_refs = [refs.pop(0) for _ in range(n_w)] if cast_w else []
    r_ref, rs_ref = (refs.pop(0), refs.pop(0)) if row_norm else (None, None)
    j = pl.program_id(0)
    i = pl.program_id(1)
    if cast_w:
        @pl.when(i == 0)
        def _():
            for w_ref, wb_ref in zip(w_refs, wb_refs):
                w = w_ref[...]
                wb_ref[...] = (w * gain_ref[...] if has_gain else w).astype(BF16)
        w_refs = wb_refs

    def rms_factor(pieces):
        sq = sum(jnp.sum(p.astype(F32) ** 2, axis=-1, keepdims=True) for p in pieces)
        width = sum(p.shape[-1] for p in pieces)
        return lax.rsqrt(sq / width + NORM_EPS) + jnp.zeros((pieces[0].shape[0], LANES), F32)

    if row_norm:
        @pl.when(j == 0)
        def _():
            r_ref[i] = rms_factor([x_ref[...] for x_ref in x_refs])
            if has_s:
                @pl.when(i == n_i - 1)
                def _():
                    rs_ref[...] = rms_factor([xs_ref[...] for xs_ref in xs_refs])

    def product(pieces, scale):
        def one(w_ref):
            k0, acc = 0, None
            for p in pieces:
                part = jnp.dot(p, w_ref[k0:k0 + p.shape[-1], :], preferred_element_type=F32)
                acc = part if acc is None else acc + part
                k0 += p.shape[-1]
            return acc if scale is None else acc * _lane_tile(scale, acc.shape[-1])
        acc = one(w_refs[0])
        if n_w == 2:
            acc = acc * jax.nn.sigmoid(acc) * one(w_refs[1])
        return acc

    def emit(acc, r, dst_refs):
        if r is not None:
            acc = r[...] + acc
        for o_ref in dst_refs:
            if o_ref.ndim == 3:
                for s in range(o_ref.shape[0]):
                    o_ref[s] = acc.astype(o_ref.dtype) if s == init_slot else jnp.zeros(acc.shape, o_ref.dtype)
            else:
                o_ref[...] = acc.astype(o_ref.dtype)

    def plain():
        emit(product([x_ref[...] for x_ref in x_refs], r_ref[i] if row_norm else None), res_ref, out_refs)

    def with_sample_rows():
        tm = x_refs[0].shape[0]
        pieces = [jnp.concatenate([x_ref[...], xs_ref[...]], axis=0) for x_ref, xs_ref in zip(x_refs, xs_refs)]
        scale = jnp.concatenate([r_ref[i], rs_ref[...]], axis=0) if row_norm else None
        acc = product(pieces, scale)
        emit(acc[:tm], res_ref, out_refs)
        emit(acc[tm:], res_s_ref, out_s_refs)

    if has_s:
        pl.when(i < n_i - 1)(plain)
        pl.when(i == n_i - 1)(with_sample_rows)
    else:
        plain()


def _dense_tiles(m, k, n, n_w):
    tn = min(1024 if (k * n_w <= 2048 and n % 1024 == 0) else 512, n)
    tm = min(1024 if k * n_w <= 4096 else 512, m)
    return tm, tn


def dense(x, w, n, col0=0, lead=None, col3=None, res=None, out_dtypes=(F32,), stack=None, xs=None, res_s=None,
          gain=None, row_norm=False):
    xparts = tuple(x) if isinstance(x, (tuple, list)) else (x,)
    sparts = (tuple(xs) if isinstance(xs, (tuple, list)) else (xs,)) if xs is not None else ()
    m = xparts[0].shape[0]
    k = sum(p.shape[1] for p in xparts)
    n_w = 1 if col3 is None else 2
    tm, tn = _dense_tiles(m, k, n, n_w)
    assert m % tm == 0 and n % tn == 0 and col0 % tn == 0 and (col3 is None or col3 % tn == 0)
    cast_w = w.dtype != BF16
    assert gain is None or (cast_w and row_norm)
    has_s = xs is not None
    n_i = m // tm

    def w_spec(c0):
        if w.ndim == 3:
            return pl.BlockSpec((None, k, tn), lambda j, i: (lead, 0, c0 // tn + j))
        return pl.BlockSpec((k, tn), lambda j, i: (0, c0 // tn + j))

    in_specs = [pl.BlockSpec((tm, p.shape[1]), lambda j, i: (i, 0)) for p in xparts]
    args = list(xparts)
    if has_s:
        ms = sparts[0].shape[0]
        in_specs += [pl.BlockSpec((ms, p.shape[1]), lambda j, i: (0, 0)) for p in sparts]
        args += list(sparts)
    in_specs.append(w_spec(col0))
    args.append(w)
    if n_w == 2:
        in_specs.append(w_spec(col3))
        args.append(w)
    if gain is not None:
        in_specs.append(pl.BlockSpec((k, 1), lambda j, i: (0, 0)))
        args.append(gain.reshape(k, 1))
    if res is not None:
        in_specs.append(pl.BlockSpec((tm, tn), lambda j, i: (i, j)))
        args.append(res)
        if has_s:
            in_specs.append(pl.BlockSpec((ms, tn), lambda j, i: (0, j)))
            args.append(res_s)
    out_specs = [pl.BlockSpec((tm, tn), lambda j, i: (i, j)) for _ in out_dtypes]
    out_shape = [jax.ShapeDtypeStruct((m, n), dt) for dt in out_dtypes]
    aliases = {}
    has_prev = False
    init_slot = None
    if stack is not None:
        depth, slot, prev = stack
        out_specs[0] = pl.BlockSpec((None, tm, tn), lambda j, i: (slot, i, j))
        out_shape[0] = jax.ShapeDtypeStruct((depth, m, n), out_dtypes[0])
        if prev is None:
            init_slot = slot
            out_specs[0] = pl.BlockSpec((depth, tm, tn), lambda j, i: (0, i, j))
        if prev is not None:
            has_prev = True
            aliases = {len(args): 0}
            in_specs.append(pl.BlockSpec(memory_space=pl.ANY))
            args.append(prev)
    if has_s:
        out_specs += [pl.BlockSpec((ms, tn), lambda j, i: (0, j)) for _ in out_dtypes]
        out_shape += [jax.ShapeDtypeStruct((ms, n), dt) for dt in out_dtypes]
    scratch = [pltpu.VMEM((k, tn), BF16)] * (n_w if cast_w else 0)
    if row_norm:
        scratch += [pltpu.VMEM((n_i, tm, LANES), F32), pltpu.VMEM((ms if has_s else SUBLANES, LANES), F32)]
    outs = pl.pallas_call(
        functools.partial(_dense_kernel, n_x=len(xparts), n_w=n_w, has_gain=gain is not None, row_norm=row_norm,
                          has_res=res is not None, has_prev=has_prev, has_s=has_s, n_out=len(out_dtypes),
                          cast_w=cast_w, n_i=n_i, init_slot=init_slot),
        grid=(n // tn, n_i),
        in_specs=in_specs,
        out_specs=out_specs,
        out_shape=out_shape,
        scratch_shapes=scratch,
        input_output_aliases=aliases,
        compiler_params=_cparams(("arbitrary", "arbitrary")),
        name="dense",
    )(*args)
    n_out = len(out_dtypes)
    unwrap = lambda t: t[0] if n_out == 1 else tuple(t)
    if has_s:
        return unwrap(outs[:n_out]), unwrap(outs[n_out:])
    return unwrap(outs)


def _softmax_rows64(logits):
    e = jnp.exp(logits - jnp.max(logits, axis=-1, keepdims=True))
    reps = logits.shape[-1] // CMP_BLOCK
    return e * (reps / jnp.sum(e, axis=-1, keepdims=True))


def _compress_prompt_kernel(x_ref, pos_ref, o_ref, *, seq):
    w = _softmax_rows64(pos_ref[...])
    blk = lax.broadcasted_iota(jnp.int32, (LANES, seq), 0)
    key_blk = lax.broadcasted_iota(jnp.int32, (LANES, seq), 1) >> CMP_SHIFT
    onblk = blk == key_blk
    for c in range(4):
        wb = jnp.where(onblk, w[c:c + 1, :], 0.0).astype(BF16)
        o_ref[0, :, c * LANES:(c + 1) * LANES] = jnp.dot(
            wb, x_ref[:, c * LANES:(c + 1) * LANES], preferred_element_type=F32).astype(BF16)


def compress_prompt(cmp_bf, pos_tiled, batch, seq):
    return pl.pallas_call(
        functools.partial(_compress_prompt_kernel, seq=seq),
        grid=(batch,),
        in_specs=[pl.BlockSpec((seq, 4 * LANES), lambda b: (b, 0)),
                  pl.BlockSpec((4, seq), lambda b: (0, 0))],
        out_specs=pl.BlockSpec((1, LANES, 4 * LANES), lambda b: (b, 0, 0)),
        out_shape=jax.ShapeDtypeStruct((batch, LANES, 4 * LANES), BF16),
        compiler_params=_cparams(("parallel",)),
        name="nsa_compress_prompt",
    )(cmp_bf, pos_tiled)


def _lane_tile(x, width):
    return x if width == LANES else jnp.concatenate([x] * (width // LANES), axis=-1)


def _flash_step(s, v, m_ref, l_ref, acc_ref, idx):
    ss, vs = (s, v) if isinstance(s, (list, tuple)) else ([s], [v])
    tiles = [[t[:, j * LANES:(j + 1) * LANES] for j in range(t.shape[1] // LANES)] for t in ss]
    blocks = [blk for tile in tiles for blk in tile]
    mx = blocks[0]
    for blk in blocks[1:]:
        mx = jnp.maximum(mx, blk)
    m_prev = m_ref[idx]
    m_new = jnp.maximum(m_prev, jnp.max(mx, axis=-1, keepdims=True))
    alpha = jnp.exp2(m_prev - m_new)
    ps = [[jnp.exp2(blk - m_new) for blk in tile] for tile in tiles]
    row_sum = None
    for tile in ps:
        for p in tile:
            row_sum = p if row_sum is None else row_sum + p
    l_ref[idx] = alpha * l_ref[idx] + jnp.sum(row_sum, axis=-1, keepdims=True)
    m_ref[idx] = m_new
    acc = _lane_tile(alpha, vs[0].shape[-1]) * acc_ref[idx]
    for tile, vt in zip(ps, vs):
        acc = acc + jnp.dot(jnp.concatenate([p.astype(BF16) for p in tile], axis=-1), vt,
                            preferred_element_type=F32)
    acc_ref[idx] = acc


def _flash_init(m_ref, l_ref, acc_ref):
    m_ref[...] = jnp.full(m_ref.shape, NEG_INF, F32)
    l_ref[...] = jnp.zeros(l_ref.shape, F32)
    acc_ref[...] = jnp.zeros(acc_ref.shape, F32)


def _flash_result(l_ref, acc_ref, idx):
    return acc_ref[idx] / _lane_tile(l_ref[idx], acc_ref.shape[-1])


def _softmax_pv(s, v):
    p = jnp.exp2(s - jnp.max(s, axis=-1, keepdims=True))
    l = jnp.sum(p, axis=-1, keepdims=True)
    return jnp.dot(p.astype(BF16), v, preferred_element_type=F32) / l


def _nsa_prompt_kernel(slopes_ref, q_ref, gate_ref, kc_ref, vc_ref, ks_ref, vs_ref, kw_ref, vw_ref,
                       o_ref, m_ref, l_ref, acc_ref, flag_ref, *, tq, tk, n_blocks):
    g = pl.program_id(1)
    qi = pl.program_id(2)
    q0 = qi * tq
    row = lax.broadcasted_iota(jnp.int32, (tq, LANES), 0)
    col = lax.broadcasted_iota(jnp.int32, (tq, LANES), 1)
    qpos = q0 + row
    nt = (((1,), (1,)), ((), ()))

    kc = kc_ref[0]
    vc = vc_ref[0]
    cdist = qpos - (col * CMP_BLOCK + (CMP_BLOCK - 1))
    cmask = cdist >= 0
    cdist_f = cdist.astype(F32)
    imp = jnp.zeros((tq, LANES), F32)
    o_cmp = []
    for h in range(NSA_G):
        slope = slopes_ref[g * NSA_G + h]
        qh = q_ref[:, h * HEAD_DIM:(h + 1) * HEAD_DIM]
        s = lax.dot_general(qh, kc, nt, preferred_element_type=F32) * ATT_SCALE - slope * cdist_f
        s = jnp.where(cmask, s, NEG_INF)
        p = jnp.exp(s - jnp.max(s, axis=-1, keepdims=True))
        p = jnp.where(cmask, p / jnp.sum(p, axis=-1, keepdims=True), 0.0)
        imp = imp + p
        o_cmp.append(jnp.dot(p.astype(BF16), vc, preferred_element_type=F32))

    imp = jnp.where(cmask, imp, -1.0)
    forced = (col == (qpos >> CMP_SHIFT)) | (col == 0)
    imp = jnp.where(forced, FORCE_SCORE, imp)
    nblk = -(-n_blocks // SUBLANES) * SUBLANES
    imp_t = imp.T[:nblk]
    sub = lax.broadcasted_iota(jnp.int32, (SUBLANES, tq), 0)
    sel_rows = []
    for vi in range(nblk // SUBLANES):
        x = imp_t[vi * SUBLANES:(vi + 1) * SUBLANES]
        rank = jnp.zeros((SUBLANES, tq), F32)
        for j in range(nblk):
            r = imp_t[j:j + 1, :]
            jv, jr = divmod(j, SUBLANES)
            ge = jnp.where(r >= x, 1.0, 0.0)
            gt = jnp.where(r > x, 1.0, 0.0)
            if jv < vi:
                rank = rank + ge
            elif jv > vi:
                rank = rank + gt
            else:
                rank = rank + jnp.where(sub > jr, ge, gt)
        sel_rows.append(jnp.where(rank < TOP_N, 1.0, 0.0))
    sel_rows.append(jnp.zeros((LANES - nblk, tq), F32))
    sel_f = jnp.concatenate(sel_rows, axis=0).T
    sel = sel_f.astype(BF16)
    picked_any = jnp.max(sel_f, axis=0, keepdims=True)
    tile_of_block = lax.broadcasted_iota(jnp.int32, (1, LANES), 1) >> ((tk // CMP_BLOCK).bit_length() - 1)
    for kt in range(n_blocks * CMP_BLOCK // tk):
        hit = jnp.max(jnp.where(tile_of_block == kt, picked_any, 0.0))
        flag_ref[kt] = (hit > 0.5).astype(jnp.int32)

    c1 = ATT_SCALE * LOG2E
    slope2 = [slopes_ref[g * NSA_G + h] * LOG2E for h in range(NSA_G)]

    _flash_init(m_ref, l_ref, acc_ref)
    e_row = lax.broadcasted_iota(jnp.int32, (LANES, tk), 0)
    e_col = lax.broadcasted_iota(jnp.int32, (LANES, tk), 1)
    rel = lax.broadcasted_iota(jnp.int32, (1, tk), 1)
    krow = lax.broadcasted_iota(jnp.int32, (tq, tk), 0)
    kcol = lax.broadcasted_iota(jnp.int32, (tq, tk), 1)

    def sel_tile(kt, diagonal):
        k0 = pl.multiple_of(kt * tk, tk)
        k = ks_ref[pl.ds(k0, tk), :]
        v = vs_ref[pl.ds(k0, tk), :]
        expand = jnp.where(((k0 + e_col) >> CMP_SHIFT) == e_row, 1.0, 0.0).astype(BF16)
        picked = jnp.dot(sel, expand, preferred_element_type=F32)
        if diagonal:
            picked = jnp.where(k0 + kcol <= q0 + krow, picked, 0.0)
        valid = picked > 0.5
        bias = (k0 - q0 + rel).astype(F32)
        for h in range(NSA_G):
            qh = q_ref[:, h * HEAD_DIM:(h + 1) * HEAD_DIM]
            s = lax.dot_general(qh, k, nt, preferred_element_type=F32) * c1 + slope2[h] * bias
            _flash_step(jnp.where(valid, s, NEG_INF), v, m_ref, l_ref, acc_ref, h)

    kd = q0 // tk
    sel_tile(kd, True)

    def sel_body(i, carry):
        kt = kd - 1 - i
        pl.when(flag_ref[kt] > 0)(lambda: sel_tile(kt, False))
        return carry

    lax.fori_loop(0, kd, sel_body, 0)
    o_sel = [_flash_result(l_ref, acc_ref, h) for h in range(NSA_G)]

    n_band = WINDOW + tq
    w0 = pl.multiple_of(jnp.maximum(q0 - WINDOW, 0), LANES)
    kw = kw_ref[pl.ds(w0, n_band), :]
    vw = vw_ref[pl.ds(w0, n_band), :]
    wdist = (q0 + lax.broadcasted_iota(jnp.int32, (tq, n_band), 0)) - (
        w0 + lax.broadcasted_iota(jnp.int32, (tq, n_band), 1))
    wvalid = (wdist >= 0) & (wdist <= WINDOW)
    wbias = (w0 - q0 + lax.broadcasted_iota(jnp.int32, (1, n_band), 1)).astype(F32)
    o_win = []
    for h in range(NSA_G):
        qh = q_ref[:, h * HEAD_DIM:(h + 1) * HEAD_DIM]
        s = lax.dot_general(qh, kw, nt, preferred_element_type=F32) * c1 + slope2[h] * wbias
        o_win.append(_softmax_pv(jnp.where(wvalid, s, NEG_INF), vw))

    gates = jax.nn.sigmoid(gate_ref[...])
    for h in range(NSA_G):
        o = (gates[:, 3 * h:3 * h + 1] * o_cmp[h] + gates[:, 3 * h + 1:3 * h + 2] * o_sel[h]
             + gates[:, 3 * h + 2:3 * h + 3] * o_win[h])
        o_ref[:, h * HEAD_DIM:(h + 1) * HEAD_DIM] = o.astype(o_ref.dtype)


def nsa_prompt(qn_bf, gate_pre, kcvc, sel_bf, win_bf, slopes, batch, seq, tq=NSA_TQ, tk=NSA_TK):
    assert tk % tq == 0 and seq % tk == 0 and seq >= WINDOW + tq and seq // CMP_BLOCK <= LANES
    nq = seq // tq
    gw = NSA_G * HEAD_DIM

    def kv_spec(kv):
        return pl.BlockSpec((seq, HEAD_DIM), lambda b, g, i: (b, NSA_KVH * kv + g))

    return pl.pallas_call(
        functools.partial(_nsa_prompt_kernel, tq=tq, tk=tk, n_blocks=seq // CMP_BLOCK),
        grid=(batch, NSA_KVH, nq),
        in_specs=[pl.BlockSpec(memory_space=pltpu.SMEM),
                  pl.BlockSpec((tq, gw), lambda b, g, i: (b * nq + i, g)),
                  pl.BlockSpec((tq, LANES), lambda b, g, i: (b * nq + i, g)),
                  pl.BlockSpec((1, LANES, HEAD_DIM), lambda b, g, i: (b, 0, g)),
                  pl.BlockSpec((1, LANES, HEAD_DIM), lambda b, g, i: (b, 0, 2 + g)),
                  kv_spec(0), kv_spec(1), kv_spec(0), kv_spec(1)],
        out_specs=pl.BlockSpec((tq, gw), lambda b, g, i: (b * nq + i, g)),
        out_shape=jax.ShapeDtypeStruct((batch * seq, NSA_Q_W), BF16),
        scratch_shapes=[pltpu.VMEM((NSA_G, tq, LANES), F32), pltpu.VMEM((NSA_G, tq, LANES), F32),
                        pltpu.VMEM((NSA_G, tq, HEAD_DIM), F32), pltpu.SMEM((seq // tk,), jnp.int32)],
        compiler_params=_cparams(("parallel", "parallel", "arbitrary")),
        name="nsa_prompt",
    )(slopes, qn_bf, gate_pre, kcvc, kcvc, sel_bf, sel_bf, win_bf, win_bf)


def _diff_lambda(lv, lam_init):
    a = jnp.sum(lv[0:1] * lv[1:2], axis=-1, keepdims=True)
    b = jnp.sum(lv[2:3] * lv[3:4], axis=-1, keepdims=True)
    return jnp.exp(a) - jnp.exp(b) + lam_init


def _diff_prompt_kernel(slopes_ref, q_ref, k_ref, v_ref, lam_ref, dn_ref, o_ref, m_ref, l_ref, acc_ref,
                        *, tq, tk, lam_init):
    h = pl.program_id(1)
    qi = pl.program_id(2)
    q0 = qi * tq
    krow = lax.broadcasted_iota(jnp.int32, (tq, tk), 0)
    kcol = lax.broadcasted_iota(jnp.int32, (tq, tk), 1)
    rel = lax.broadcasted_iota(jnp.int32, (1, tk), 1)
    c1 = ATT_SCALE * LOG2E
    slope2 = slopes_ref[h] * LOG2E
    nt = (((1,), (1,)), ((), ()))
    _flash_init(m_ref, l_ref, acc_ref)

    def tile(kt, diagonal):
        k0 = pl.multiple_of(kt * tk, tk)
        k = k_ref[pl.ds(k0, tk), :]
        v = v_ref[pl.ds(k0, tk), :]
        bias = slope2 * (k0 - q0 + rel).astype(F32)
        for c in range(2):
            s = lax.dot_general(q_ref[:, c * HEAD_DIM:(c + 1) * HEAD_DIM], k[:, c * HEAD_DIM:(c + 1) * HEAD_DIM],
                                nt, preferred_element_type=F32) * c1 + bias
            if diagonal:
                s = jnp.where(k0 + kcol <= q0 + krow, s, NEG_INF)
            _flash_step(s, v, m_ref, l_ref, acc_ref, c)

    kd = q0 // tk
    tile(kd, True)

    def body(i, carry):
        tile(kd - 1 - i, False)
        return carry

    lax.fori_loop(0, kd, body, 0)
    lam = _diff_lambda(lam_ref[...], lam_init)
    o = _flash_result(l_ref, acc_ref, 0) - lam * _flash_result(l_ref, acc_ref, 1)
    y = o * lax.rsqrt(jnp.mean(o * o, axis=-1, keepdims=True) + NORM_EPS)
    o_ref[...] = (y * dn_ref[...] * (1.0 - lam_init)).astype(o_ref.dtype)


def diff_prompt(qd_bf, kvd_bf, lam_vec, dnorm, slopes, lam_init, batch, seq, tq=DIFF_TQ, tk=DIFF_TK):
    assert tk % tq == 0 and seq % tk == 0
    nq = seq // tq
    w = 2 * HEAD_DIM
    qb, kb, vb = 0, 0, DIFF_H
    return pl.pallas_call(
        functools.partial(_diff_prompt_kernel, tq=tq, tk=tk, lam_init=lam_init),
        grid=(batch, DIFF_H, nq),
        in_specs=[pl.BlockSpec(memory_space=pltpu.SMEM),
                  pl.BlockSpec((tq, w), lambda b, h, i: (b * nq + i, qb + h)),
                  pl.BlockSpec((seq, w), lambda b, h, i: (b, kb + h)),
                  pl.BlockSpec((seq, w), lambda b, h, i: (b, vb + h)),
                  pl.BlockSpec((4, HEAD_DIM), lambda b, h, i: (0, 0)),
                  pl.BlockSpec((1, DIFF_VD), lambda b, h, i: (0, 0))],
        out_specs=pl.BlockSpec((tq, w), lambda b, h, i: (b * nq + i, h)),
        out_shape=jax.ShapeDtypeStruct((batch * seq, DIFF_V_W), BF16),
        scratch_shapes=[pltpu.VMEM((2, tq, LANES), F32), pltpu.VMEM((2, tq, LANES), F32),
                        pltpu.VMEM((2, tq, DIFF_VD), F32)],
        compiler_params=_cparams(("parallel", "parallel", "arbitrary")),
        name="diff_prompt",
    )(slopes, qd_bf, kvd_bf, kvd_bf, lam_vec, dnorm.reshape(1, DIFF_VD))


def _ret_prompt_kernel(lg_ref, q_ref, k_ref, v_ref, g_ref, rn_ref, o_ref, st_ref, *, chunk):
    h = pl.program_id(1)
    c = pl.program_id(2)
    lg = lg_ref[h]

    @pl.when(c == 0)
    def _():
        st_ref[...] = jnp.zeros(st_ref.shape, F32)

    ii = lax.broadcasted_iota(jnp.int32, (chunk, chunk), 0)
    jj = lax.broadcasted_iota(jnp.int32, (chunk, chunk), 1)
    d = (ii - jj).astype(F32)
    decay = jnp.where(d >= 0, jnp.exp(lg * jnp.maximum(d, 0.0)), 0.0)
    ik = lax.broadcasted_iota(jnp.int32, (chunk, RET_DK), 0).astype(F32)
    q_dec = jnp.exp(lg * (ik + 1.0))
    k_dec = jnp.exp(lg * (chunk - 1.0 - ik))
    ones = jnp.ones((1, 1), F32)
    g_chunk = jnp.exp(ones * (lg * chunk))

    q = q_ref[...]
    k = k_ref[...] * (RET_DK ** -0.5)
    v = v_ref[...].astype(BF16)
    state = st_ref[0, 0]
    s = lax.dot_general(q.astype(BF16), k.astype(BF16), (((1,), (1,)), ((), ())),
                        preferred_element_type=F32) * decay
    o = jnp.dot(s.astype(BF16), v, preferred_element_type=F32)
    o = o + jnp.dot((q * q_dec).astype(BF16), state.astype(BF16), preferred_element_type=F32)
    kv = lax.dot_general((k * k_dec).astype(BF16), v, (((0,), (0,)), ((), ())), preferred_element_type=F32)
    st_ref[0, 0] = state * g_chunk + kv

    y = o * lax.rsqrt(jnp.mean(o * o, axis=-1, keepdims=True) + NORM_EPS) * rn_ref[0]
    gate = g_ref[...]
    o_ref[...] = (gate * jax.nn.sigmoid(gate) * y).astype(o_ref.dtype)


def ret_prompt(proj, rnorm, log_g, batch, seq, chunk=RET_BLOCK):
    nc = seq // chunk
    kb = RET_H
    vb = 2 * RET_H * RET_DK // RET_DV
    gb = vb + RET_H
    return pl.pallas_call(
        functools.partial(_ret_prompt_kernel, chunk=chunk),
        grid=(batch, RET_H, nc),
        in_specs=[pl.BlockSpec(memory_space=pltpu.SMEM),
                  pl.BlockSpec((chunk, RET_DK), lambda b, h, c: (b * nc + c, h)),
                  pl.BlockSpec((chunk, RET_DK), lambda b, h, c: (b * nc + c, kb + h)),
                  pl.BlockSpec((chunk, RET_DV), lambda b, h, c: (b * nc + c, vb + h)),
                  pl.BlockSpec((chunk, RET_DV), lambda b, h, c: (b * nc + c, gb + h)),
                  pl.BlockSpec((1, 1, RET_DV), lambda b, h, c: (h, 0, 0))],
        out_specs=[pl.BlockSpec((chunk, RET_DV), lambda b, h, c: (b * nc + c, h)),
                   pl.BlockSpec((1, 1, RET_DK, RET_DV), lambda b, h, c: (b, h, 0, 0))],
        out_shape=[jax.ShapeDtypeStruct((batch * seq, RET_H * RET_DV), BF16),
                   jax.ShapeDtypeStruct((batch, RET_H, RET_DK, RET_DV), F32)],
        compiler_params=_cparams(("parallel", "parallel", "arbitrary")),
        name="ret_prompt",
    )(log_g, proj, proj, proj, proj, rnorm.reshape(RET_H, 1, RET_DV))


NT_DIMS = (((1,), (1,)), ((), ()))


def _compress_sample_kernel(pt_ref, pos_ref, *refs, pp):
    del pt_ref
    o_ref = refs[pp]
    per_page = PAGE_SIZE // CMP_BLOCK
    n_rows = PAGE_SIZE * NSA_KVH
    row = lax.broadcasted_iota(jnp.int32, (SUBLANES, n_rows), 0)
    col = lax.broadcasted_iota(jnp.int32, (SUBLANES, n_rows), 1)
    g_shift = NSA_KVH.bit_length() - 1
    member = ((col & (NSA_KVH - 1)) == (row & (NSA_KVH - 1))) & ((col >> (g_shift + CMP_SHIFT)) == (row >> g_shift))
    weights = []
    for kv in range(2):
        logits = jnp.where(member, pos_ref[kv], NEG_INF)
        e = jnp.exp(logits - jnp.max(logits, axis=-1, keepdims=True))
        weights.append((e / jnp.sum(e, axis=-1, keepdims=True)).astype(BF16))
    for i in range(pp):
        for kv in range(2):
            x = refs[i][0, 0, :, kv].reshape(n_rows, HEAD_DIM).astype(BF16)
            res = jnp.dot(weights[kv], x, preferred_element_type=F32)
            for half in range(per_page):
                for g in range(NSA_KVH):
                    c0 = (kv * NSA_KVH + g) * HEAD_DIM
                    o_ref[0, i * per_page + half:i * per_page + half + 1, c0:c0 + HEAD_DIM] = (
                        res[half * NSA_KVH + g:half * NSA_KVH + g + 1])


def compress_sample(cache, layer, page_table, pos_rows, pp=16):
    db, n_pages = page_table.shape
    per_page = PAGE_SIZE // CMP_BLOCK
    width = 2 * NSA_KVH * HEAD_DIM
    assert per_page * NSA_KVH <= SUBLANES

    def page_spec(i):
        return pl.BlockSpec((1, 1) + cache.shape[2:], lambda b, j, pt: (layer, pt[b, j * pp + i], 0, 0, 0, 0))

    return pl.pallas_call(
        functools.partial(_compress_sample_kernel, pp=pp),
        grid_spec=pltpu.PrefetchScalarGridSpec(
            num_scalar_prefetch=1,
            grid=(db, n_pages // pp),
            in_specs=[pl.BlockSpec(pos_rows.shape, lambda b, j, pt: (0, 0, 0))] + [page_spec(i) for i in range(pp)],
            out_specs=pl.BlockSpec((1, pp * per_page, width), lambda b, j, pt: (b, j, 0))),
        out_shape=jax.ShapeDtypeStruct((db, n_pages * per_page, width), F32),
        compiler_params=_cparams(("parallel", "arbitrary")),
        name="nsa_compress_sample",
    )(page_table, pos_rows, *([cache] * pp))


def _pick_group(x):
    row = lax.broadcasted_iota(jnp.int32, (NSA_H, HEAD_DIM), 0)
    return jnp.where(row < NSA_G, x[:, :HEAD_DIM], x[:, HEAD_DIM:])


def _nsa_decode_cmp_kernel(q_ref, slope_ref, kcvc_ref, win_ref, ocmp_ref, owin_ref, idx_ref, *, past, n_win):
    q8 = q_ref[0]
    kcvc = kcvc_ref[0]
    nb = kcvc.shape[0]
    kvw = NSA_KVH * HEAD_DIM
    lane = lax.broadcasted_iota(jnp.int32, (NSA_H, nb), 1)
    row = lax.broadcasted_iota(jnp.int32, (NSA_H, nb), 0)
    slope = slope_ref[...][:, :1]
    cdist = (past - (lane * CMP_BLOCK + CMP_BLOCK - 1)).astype(F32)
    s = lax.dot_general(q8, kcvc[:, :kvw].astype(BF16), NT_DIMS, preferred_element_type=F32) * ATT_SCALE
    s = s - slope * cdist
    p = jnp.exp(s - jnp.max(s, axis=-1, keepdims=True))
    p = p / jnp.sum(p, axis=-1, keepdims=True)
    ocmp_ref[0] = _pick_group(jnp.dot(p.astype(BF16), kcvc[:, kvw:].astype(BF16), preferred_element_type=F32))

    g0 = p[0:1] + p[1:2] + p[2:3] + p[3:4]
    g1 = p[4:5] + p[5:6] + p[6:7] + p[7:8]
    x = jnp.where(row < NSA_G, g0, g1)
    x = jnp.where(lane == 0, FORCE_SCORE, x)
    out_lane = lax.broadcasted_iota(jnp.int32, (NSA_H, LANES), 1)
    lane_f = lane.astype(F32)
    picked = jnp.zeros((NSA_H, LANES), F32)
    for t in range(TOP_N - 1):
        mx = jnp.max(x, axis=-1, keepdims=True)
        idx = jnp.min(jnp.where(x == mx, lane_f, float(nb)), axis=-1, keepdims=True)
        picked = jnp.where(out_lane == t, idx, picked)
        x = jnp.where(lane_f == idx, -2.0, x)
    idx_ref[0] = picked.astype(jnp.int32)

    win = win_ref[0]
    nw = win.shape[0]
    wl = lax.broadcasted_iota(jnp.int32, (NSA_H, nw), 1)
    wdist = (n_win - 1 - wl)
    valid = wdist >= 0
    s = lax.dot_general(q8, win[:, :kvw].astype(BF16), NT_DIMS, preferred_element_type=F32) * ATT_SCALE
    s = jnp.where(valid, s - slope * wdist.astype(F32), NEG_INF)
    p = jnp.exp(s - jnp.max(s, axis=-1, keepdims=True))
    p = p / jnp.sum(p, axis=-1, keepdims=True)
    owin_ref[0] = _pick_group(jnp.dot(p.astype(BF16), win[:, kvw:].astype(BF16), preferred_element_type=F32))


def nsa_decode_cmp(q8, slope8, kcvc, win_all, past, n_win):
    db = q8.shape[0]
    nb = kcvc.shape[1]
    nw = win_all.shape[1]
    w = kcvc.shape[2]
    head_out = jax.ShapeDtypeStruct((db, NSA_H, HEAD_DIM), F32)
    head_spec = pl.BlockSpec((1, NSA_H, HEAD_DIM), lambda b: (b, 0, 0))
    return pl.pallas_call(
        functools.partial(_nsa_decode_cmp_kernel, past=past, n_win=n_win),
        grid=(db,),
        in_specs=[pl.BlockSpec((1, NSA_H, NSA_KVH * HEAD_DIM), lambda b: (b, 0, 0)),
                  pl.BlockSpec((NSA_H, LANES), lambda b: (0, 0)),
                  pl.BlockSpec((1, nb, w), lambda b: (b, 0, 0)),
                  pl.BlockSpec((1, nw, w), lambda b: (b, 0, 0))],
        out_specs=[head_spec, head_spec, pl.BlockSpec((1, NSA_H, LANES), lambda b: (b, 0, 0))],
        out_shape=[head_out, head_out, jax.ShapeDtypeStruct((db, NSA_H, LANES), jnp.int32)],
        compiler_params=_cparams(("parallel",)),
        name="nsa_decode_cmp",
    )(q8, slope8, kcvc, win_all)


SEL_GROUP = 5


def _nsa_decode_sel_kernel(pt_ref, ids_ref, q_ref, slope_ref, new_ref, ocmp_ref, owin_ref, gate_ref, *refs,
                           past, n_sel, grp):
    del pt_ref
    page_refs = refs[:2 * grp]
    o_ref, m_ref, l_ref, acc_ref = refs[2 * grp:]
    b = pl.program_id(0)
    t = pl.program_id(1)
    q8 = q_ref[0]
    row = lax.broadcasted_iota(jnp.int32, (NSA_H, LANES), 0)
    lane = lax.broadcasted_iota(jnp.int32, (NSA_H, LANES), 1)
    slope = slope_ref[...][:, :1]
    kvw = NSA_KVH * HEAD_DIM
    per_page = PAGE_SIZE // CMP_BLOCK

    @pl.when(t == 0)
    def _():
        new = new_ref[0].astype(BF16).astype(F32)
        m_ref[...] = jnp.sum(q8.astype(F32) * new[:, :kvw], axis=-1, keepdims=True) * ATT_SCALE + jnp.zeros(
            (NSA_H, LANES), F32)
        l_ref[...] = jnp.ones((NSA_H, LANES), F32)
        acc_ref[...] = jnp.where(row < NSA_G, new[:, kvw:kvw + HEAD_DIM], new[:, kvw + HEAD_DIM:])

    scores = []
    for u in range(grp):
        pa_ref, pb_ref = page_refs[2 * u], page_refs[2 * u + 1]
        na = ids_ref[b, t * grp + u]
        nbk = ids_ref[b, n_sel + t * grp + u]
        blk = jnp.where(row < NSA_G, na, nbk)
        valid = (lane >> CMP_SHIFT) == (blk & (per_page - 1))
        dist = past - ((blk >> (per_page.bit_length() - 1)) * PAGE_SIZE + lane)
        sa = lax.dot_general(q8[:, :HEAD_DIM], pa_ref[0, 0, :, 0, 0, :].astype(BF16), NT_DIMS,
                             preferred_element_type=F32)
        sb = lax.dot_general(q8[:, HEAD_DIM:], pb_ref[0, 0, :, 0, 1, :].astype(BF16), NT_DIMS,
                             preferred_element_type=F32)
        s = jnp.where(row < NSA_G, sa, sb) * ATT_SCALE - slope * dist.astype(F32)
        scores.append(jnp.where(valid, s, NEG_INF))
    m_prev = m_ref[...]
    m_new = m_prev
    for s in scores:
        m_new = jnp.maximum(m_new, jnp.max(s, axis=-1, keepdims=True))
    alpha = jnp.exp(m_prev - m_new)
    l_new = alpha * l_ref[...]
    acc = alpha * acc_ref[...]
    for u, s in enumerate(scores):
        p = jnp.exp(s - m_new)
        l_new = l_new + jnp.sum(p, axis=-1, keepdims=True)
        pb16 = p.astype(BF16)
        oa = jnp.dot(pb16, page_refs[2 * u][0, 0, :, 1, 0, :].astype(BF16), preferred_element_type=F32)
        ob = jnp.dot(pb16, page_refs[2 * u + 1][0, 0, :, 1, 1, :].astype(BF16), preferred_element_type=F32)
        acc = acc + jnp.where(row < NSA_G, oa, ob)
    m_ref[...] = m_new
    l_ref[...] = l_new
    acc_ref[...] = acc

    @pl.when(t == n_sel // grp - 1)
    def _():
        gates = jax.nn.sigmoid(gate_ref[0])
        o = (gates[:, 0:1] * ocmp_ref[0] + gates[:, 1:2] * (acc_ref[...] / l_ref[...])
             + gates[:, 2:3] * owin_ref[0])
        o_ref[0] = o.astype(o_ref.dtype)


def nsa_decode_sel(cache, layer, page_table, ids, q8, slope8, new_row, o_cmp, o_win, gates, past, grp=SEL_GROUP):
    db = q8.shape[0]
    n_sel = ids.shape[1] // NSA_KVH
    assert n_sel % grp == 0
    width = 2 * NSA_KVH * HEAD_DIM
    per_page = PAGE_SIZE // CMP_BLOCK
    head_spec = pl.BlockSpec((1, NSA_H, HEAD_DIM), lambda b, t, pt, ids: (b, 0, 0))

    def page_spec(g, u):
        return pl.BlockSpec(
            (1, 1) + cache.shape[2:],
            lambda b, t, pt, ids: (layer, pt[b, ids[b, g * n_sel + t * grp + u] // per_page], 0, 0, 0, 0))

    page_specs = [page_spec(g, u) for u in range(grp) for g in range(NSA_KVH)]
    return pl.pallas_call(
        functools.partial(_nsa_decode_sel_kernel, past=past, n_sel=n_sel, grp=grp),
        grid_spec=pltpu.PrefetchScalarGridSpec(
            num_scalar_prefetch=2,
            grid=(db, n_sel // grp),
            in_specs=[pl.BlockSpec((1, NSA_H, NSA_KVH * HEAD_DIM), lambda b, t, pt, ids: (b, 0, 0)),
                      pl.BlockSpec((NSA_H, LANES), lambda b, t, pt, ids: (0, 0)),
                      pl.BlockSpec((1, 1, width), lambda b, t, pt, ids: (b, 0, 0)),
                      head_spec, head_spec, head_spec] + page_specs,
            out_specs=head_spec,
            scratch_shapes=[pltpu.VMEM((NSA_H, LANES), F32), pltpu.VMEM((NSA_H, LANES), F32),
                            pltpu.VMEM((NSA_H, HEAD_DIM), F32)]),
        out_shape=jax.ShapeDtypeStruct((db, NSA_H, HEAD_DIM), BF16),
        compiler_params=_cparams(("parallel", "arbitrary")),
        name="nsa_decode_sel",
    )(page_table, ids, q8, slope8, new_row, o_cmp, o_win, gates, *([cache] * len(page_specs)))


def _diff_decode_kernel(pt_ref, q_ref, slope_ref, new_ref, lam_ref, dn_ref, *refs, pp, past, lam_init):
    del pt_ref
    page_refs = refs[:pp]
    o_ref, m_ref, l_ref, acc_ref = refs[pp:]
    j = pl.program_id(1)
    rows = 2 * DIFF_H
    n_keys = PAGE_SIZE * DIFF_H
    h_shift = DIFF_H.bit_length() - 1
    q8 = q_ref[0]
    slope2 = slope_ref[...][:, :1] * LOG2E
    row = lax.broadcasted_iota(jnp.int32, (rows, n_keys), 0)
    col = lax.broadcasted_iota(jnp.int32, (rows, n_keys), 1)
    own_head = (col & (DIFF_H - 1)) == (row & (DIFF_H - 1))
    key_in_page = (col >> h_shift).astype(F32)

    @pl.when(j == 0)
    def _():
        k_new = jnp.concatenate([new_ref[0, 0]] * 2, axis=0).astype(BF16).astype(F32)
        m_ref[0] = jnp.sum(q8.astype(F32) * k_new, axis=-1, keepdims=True) * (ATT_SCALE * LOG2E) + jnp.zeros(
            (rows, LANES), F32)
        l_ref[0] = jnp.ones((rows, LANES), F32)
        acc_ref[0] = jnp.concatenate([new_ref[0, 1]] * 2, axis=0).astype(BF16).astype(F32)

    scores, values = [], []
    for i in range(pp):
        page = page_refs[i]
        k0 = (j * pp + i) * PAGE_SIZE
        kx = page[0, 0, :, 0].reshape(n_keys, DIFF_VD).astype(BF16)
        values.append(page[0, 0, :, 1].reshape(n_keys, DIFF_VD).astype(BF16))
        s = lax.dot_general(q8, kx, NT_DIMS, preferred_element_type=F32) * (ATT_SCALE * LOG2E)
        s = s - slope2 * ((past - k0).astype(F32) - key_in_page)
        scores.append(jnp.where(own_head, s, NEG_INF))
    _flash_step(scores, values, m_ref, l_ref, acc_ref, 0)

    @pl.when(j == pl.num_programs(1) - 1)
    def _():
        lam = _diff_lambda(lam_ref[...], lam_init)
        o = _flash_result(l_ref, acc_ref, 0)
        oh = o[:DIFF_H] - lam * o[DIFF_H:]
        y = oh * lax.rsqrt(jnp.mean(oh * oh, axis=-1, keepdims=True) + NORM_EPS)
        o_ref[0] = (y * dn_ref[...] * (1.0 - lam_init)).astype(o_ref.dtype)


def diff_decode(cache, layer, page_table, q8, slope8, new_row, lam_vec, dnorm, lam_init, past, pp=8):
    db, n_pages = page_table.shape
    rows = 2 * DIFF_H

    def page_spec(i):
        return pl.BlockSpec((1, 1) + cache.shape[2:], lambda b, j, pt: (layer, pt[b, j * pp + i], 0, 0, 0, 0))

    return pl.pallas_call(
        functools.partial(_diff_decode_kernel, pp=pp, past=past, lam_init=lam_init),
        grid_spec=pltpu.PrefetchScalarGridSpec(
            num_scalar_prefetch=1,
            grid=(db, n_pages // pp),
            in_specs=[pl.BlockSpec((1, rows, DIFF_VD), lambda b, j, pt: (b, 0, 0)),
                      pl.BlockSpec((rows, LANES), lambda b, j, pt: (0, 0)),
                      pl.BlockSpec((1, 2, DIFF_H, DIFF_VD), lambda b, j, pt: (b, 0, 0, 0)),
                      pl.BlockSpec((4, HEAD_DIM), lambda b, j, pt: (0, 0)),
                      pl.BlockSpec((1, DIFF_VD), lambda b, j, pt: (0, 0))] + [page_spec(i) for i in range(pp)],
            out_specs=pl.BlockSpec((1, DIFF_H, DIFF_VD), lambda b, j, pt: (b, 0, 0)),
            scratch_shapes=[pltpu.VMEM((1, rows, LANES), F32), pltpu.VMEM((1, rows, LANES), F32),
                            pltpu.VMEM((1, rows, DIFF_VD), F32)]),
        out_shape=jax.ShapeDtypeStruct((db, DIFF_H, DIFF_VD), BF16),
        compiler_params=_cparams(("parallel", "arbitrary")),
        name="diff_decode",
    )(page_table, q8, slope8, new_row, lam_vec, dnorm.reshape(1, DIFF_VD), *([cache] * pp))


def _ret_decode_kernel(lg_ref, q_ref, k_ref, v_ref, g_ref, rn_ref, st_ref, o_ref, nst_ref):
    ii = lax.broadcasted_iota(jnp.int32, (RET_DK, RET_DK), 0)
    jj = lax.broadcasted_iota(jnp.int32, (RET_DK, RET_DK), 1)
    ones = jnp.ones((1, 1), F32)
    for h in range(RET_H):
        gamma = jnp.exp(ones * lg_ref[h])
        q = q_ref[0, h:h + 1, :]
        k = k_ref[0, h:h + 1, :] * (RET_DK ** -0.5)
        v = v_ref[0, h:h + 1, :]
        state = st_ref[0, h]
        qb = q.astype(BF16).astype(F32)
        kb = k.astype(BF16).astype(F32)
        s = jnp.sum(qb * kb, axis=-1, keepdims=True)
        q_dec = jnp.broadcast_to(q * gamma, (SUBLANES, RET_DK)).astype(BF16)
        o = s * v + jnp.dot(q_dec, state.astype(BF16), preferred_element_type=F32)[0:1]
        k_col = jnp.sum(jnp.where(ii == jj, k, 0.0), axis=-1, keepdims=True)
        nst_ref[0, h] = state * gamma + k_col * v
        y = o * lax.rsqrt(jnp.mean(o * o, axis=-1, keepdims=True) + NORM_EPS) * rn_ref[h]
        gate = g_ref[0, h:h + 1, :]
        o_ref[0, h:h + 1, :] = (gate * jax.nn.sigmoid(gate) * y).astype(o_ref.dtype)


def ret_decode(q, k, v, g, rnorm, state, log_g):
    db = q.shape[0]
    return pl.pallas_call(
        _ret_decode_kernel,
        grid=(db,),
        in_specs=[pl.BlockSpec(memory_space=pltpu.SMEM),
                  pl.BlockSpec((1, RET_H, RET_DK), lambda b: (b, 0, 0)),
                  pl.BlockSpec((1, RET_H, RET_DK), lambda b: (b, 0, 0)),
                  pl.BlockSpec((1, RET_H, RET_DV), lambda b: (b, 0, 0)),
                  pl.BlockSpec((1, RET_H, RET_DV), lambda b: (b, 0, 0)),
                  pl.BlockSpec((RET_H, 1, RET_DV), lambda b: (0, 0, 0)),
                  pl.BlockSpec((1, RET_H, RET_DK, RET_DV), lambda b: (b, 0, 0, 0))],
        out_specs=[pl.BlockSpec((1, RET_H, RET_DV), lambda b: (b, 0, 0)),
                   pl.BlockSpec((1, RET_H, RET_DK, RET_DV), lambda b: (b, 0, 0, 0))],
        out_shape=[jax.ShapeDtypeStruct((db, RET_H, RET_DV), BF16),
                   jax.ShapeDtypeStruct(state.shape, F32)],
        compiler_params=_cparams(("parallel",)),
        name="ret_decode",
    )(log_g, q, k, v, g, rnorm.reshape(RET_H, 1, RET_DV), state)


def _alibi_slopes(n):
    return jnp.asarray([2.0 ** (-8.0 * (i + 1) / n) for i in range(n)], dtype=F32)


def _even_layer(e, layer, hp, hs, gain, batch, seq, caches, page_table, even_w_in, cmp_pos, lam_vec, dnorm, stacks):
    cache_cmp, cache_sel, cache_win, cache_diff = caches
    n_even = even_w_in.shape[0]
    d = even_w_in.shape[1]
    lam_init = 0.8 - 0.6 * math.exp(-0.3 * layer)
    db = page_table.shape[0]
    past = page_table.shape[1] * PAGE_SIZE
    nsa_slopes = _alibi_slopes(NSA_H)
    diff_slopes = _alibi_slopes(DIFF_H)
    kvw = 2 * NSA_KV_W
    per_group = NSA_G * 3
    w_gate = even_w_in[e, :, GATE_OFF:GATE_OFF + NSA_GATE_W].reshape(d, NSA_KVH, per_group)
    w_gate = jnp.pad(w_gate, ((0, 0), (0, 0), (0, LANES - per_group))).reshape(d, NSA_KVH * LANES)
    w_tail = even_w_in[e, :, GATE_OFF + NSA_GATE_W:]
    rn = gain is not None
    if rn:
        w_gate, w_tail = w_gate * gain[:, None], w_tail * gain[:, None]
    w_gate, w_tail = w_gate.astype(BF16), w_tail.astype(BF16)
    nrm = dict(gain=gain, row_norm=rn)
    nrm_folded = dict(row_norm=rn)

    qn_bf, qn_s = dense(hp, even_w_in, NSA_Q_W, lead=e, out_dtypes=(BF16,), xs=hs, **nrm)
    (cmp_st, cmp_bf), (new_cmp, _) = dense(hp, even_w_in, kvw, col0=NSA_Q_W, lead=e, out_dtypes=(F32, BF16),
                                           stack=(n_even, e, stacks[0]), xs=hs, **nrm)
    (sel_st, sel_bf), (new_sel, _) = dense(hp, even_w_in, kvw, col0=NSA_Q_W + kvw, lead=e,
                                           out_dtypes=(F32, BF16), stack=(n_even, e, stacks[1]), xs=hs, **nrm)
    (win_f, win_bf), (new_win, _) = dense(hp, even_w_in, kvw, col0=NSA_Q_W + 2 * kvw, lead=e,
                                          out_dtypes=(F32, BF16), xs=hs, **nrm)
    gate_pre, gate_s = dense(hp, w_gate, NSA_KVH * LANES, xs=hs, **nrm_folded)
    qd_bf, qd_s = dense(hp, w_tail, DIFF_Q_W, out_dtypes=(BF16,), xs=hs, **nrm_folded)
    (kvd_st, kvd_bf), (new_diff, _) = dense(hp, w_tail, DIFF_Q_W + DIFF_V_W, col0=DIFF_Q_W,
                                            out_dtypes=(F32, BF16), stack=(n_even, e, stacks[2]), xs=hs,
                                            **nrm_folded)
    qn_s, new_cmp, new_sel, new_win, gate_s, qd_s, new_diff = (
        t[:db] for t in (qn_s, new_cmp, new_sel, new_win, gate_s, qd_s, new_diff))

    pos_cg = jnp.transpose(cmp_pos, (0, 2, 1)).reshape(2 * NSA_KVH, CMP_BLOCK)
    kcvc = compress_prompt(cmp_bf, jnp.tile(pos_cg, (1, seq // CMP_BLOCK)), batch, seq)
    o_nsa = nsa_prompt(qn_bf, gate_pre, kcvc, sel_bf, win_bf, nsa_slopes, batch, seq)
    o_diff = diff_prompt(qd_bf, kvd_bf, lam_vec, dnorm, diff_slopes, lam_init, batch, seq)
    mix_p = (o_nsa, o_diff)
    keep_p = min(WINDOW, seq)
    win_p = win_f.reshape(batch, seq, kvw)[:, seq - keep_p:].reshape(batch, keep_p, 2, NSA_KVH, HEAD_DIM)

    gates8 = gate_s.reshape(db, NSA_KVH, LANES)[:, :, :per_group].reshape(db, NSA_H, 3)
    gates8 = jnp.pad(gates8, ((0, 0), (0, 0), (0, LANES - 3)))
    group_of_head = jnp.asarray(np.arange(NSA_H) // NSA_G)
    onehot_g = jax.nn.one_hot(group_of_head, NSA_KVH, dtype=BF16)
    qn = qn_s.reshape(db, NSA_H, HEAD_DIM)
    q8 = (qn[:, :, None, :] * onehot_g[None, :, :, None]).reshape(db, NSA_H, NSA_KVH * HEAD_DIM)
    slope8 = jnp.broadcast_to(nsa_slopes[:, None], (NSA_H, LANES))
    pos_rg = jnp.tile(jnp.transpose(cmp_pos, (0, 2, 1)), (1, SUBLANES // NSA_KVH, PAGE_SIZE // CMP_BLOCK))
    pos_rows = jnp.repeat(pos_rg, NSA_KVH, axis=2)
    kcvc_s = compress_sample(cache_cmp, e, page_table, pos_rows)
    w_buf = cache_win.shape[2]
    win_all = jnp.concatenate([cache_win[e].reshape(db, w_buf, -1), new_win[:, None, :]], axis=1)
    n_win = w_buf + 1
    win_pad = jnp.pad(win_all, ((0, 0), (0, -n_win % LANES), (0, 0)))
    o_cmp, o_win, idx = nsa_decode_cmp(q8, slope8, kcvc_s, win_pad, past, n_win)
    ids = jnp.concatenate([idx[:, 0, :TOP_N - 1], idx[:, NSA_G, :TOP_N - 1]], axis=1)
    o_nsa_s = nsa_decode_sel(cache_sel, e, page_table, ids, q8, slope8, new_sel[:, None, :], o_cmp, o_win, gates8,
                             past)

    qd = qd_s.reshape(db, DIFF_H, 2, HEAD_DIM)
    eye_c = jnp.eye(2, dtype=BF16)
    q8d = jnp.einsum('bhcd,cj->bchjd', qd, eye_c).reshape(db, 2 * DIFF_H, DIFF_VD)
    slope8d = jnp.broadcast_to(jnp.tile(diff_slopes, 2)[:, None], (2 * DIFF_H, LANES))
    o_diff_s = diff_decode(cache_diff, e, page_table, q8d, slope8d, new_diff.reshape(db, 2, DIFF_H, DIFF_VD),
                           lam_vec, dnorm, lam_init, past)
    pad_rows = ((0, hs.shape[0] - db), (0, 0))
    mix_s = (jnp.pad(o_nsa_s.reshape(db, NSA_Q_W), pad_rows), jnp.pad(o_diff_s.reshape(db, DIFF_V_W), pad_rows))
    kv_s = (db, 1, 2, NSA_KVH, HEAD_DIM)
    keep = min(WINDOW, n_win)
    small = (win_p, new_cmp.reshape(kv_s), new_sel.reshape(kv_s),
             win_all[:, n_win - keep:].reshape(db, keep, 2, NSA_KVH, HEAD_DIM),
             new_diff.reshape(db, 1, 2, DIFF_H, DIFF_VD))
    return mix_p, mix_s, (cmp_st, sel_st, kvd_st), small


def _ret_layer(o, hp, hs, gain, batch, seq, state, ret_w_in, rnorm):
    db = state.shape[0]
    log_g = jnp.log1p(-jnp.exp2(-5.0 - jnp.arange(RET_H, dtype=F32)))
    n_in = ret_w_in.shape[2]
    proj, ps = dense(hp, ret_w_in, n_in, lead=o, xs=hs, gain=gain, row_norm=gain is not None)
    ps = ps[:db]
    gated_p, st_p = ret_prompt(proj, rnorm, log_g, batch, seq)
    hk = RET_H * RET_DK
    hv = RET_H * RET_DV
    q = ps[:, :hk].reshape(db, RET_H, RET_DK)
    k = ps[:, hk:2 * hk].reshape(db, RET_H, RET_DK)
    v = ps[:, 2 * hk:2 * hk + hv].reshape(db, RET_H, RET_DV)
    g = ps[:, 2 * hk + hv:].reshape(db, RET_H, RET_DV)
    gated_s, st_s = ret_decode(q, k, v, g, rnorm, state, log_g)
    return gated_p, jnp.pad(gated_s.reshape(db, hv), ((0, hs.shape[0] - db), (0, 0))), st_p, st_s


def kernel(x_prompt, x_sample, cache_nsa_cmp, cache_nsa_sel, cache_nsa_win, cache_diff, state_ret, page_table,
           norm_mix, norm_ffn, norm_final, even_w_in, even_w_out, nsa_cmp_pos, diff_lambda, diff_norm,
           ret_w_in, ret_norm, ret_w_out, ffn_w13, ffn_w2):
    batch, seq, d = x_prompt.shape
    db = x_sample.shape[0]
    d_ff = ffn_w2.shape[1]
    xp = x_prompt.reshape(batch * seq, d)
    xs = jnp.pad(x_sample.reshape(db, d), ((0, -db % SAMPLE_ROWS), (0, 0)))
    caches = (cache_nsa_cmp, cache_nsa_sel, cache_nsa_win, cache_diff)
    stacks = (None, None, None)
    small = [[] for _ in range(5)]
    ret_p, ret_s = [], []
    both = (F32, BF16)
    for layer in range(DEPTH):
        if layer == 0:
            hp, hs, gain = rmsnorm(xp, norm_mix[0], BF16), rmsnorm(xs, norm_mix[0], BF16), None
        else:
            hp, hs, gain = xp_bf, xs_bf, norm_mix[layer]
        if layer % 2 == 0:
            e = layer // 2
            mix_p, mix_s, stacks, small_e = _even_layer(
                e, layer, hp, hs, gain, batch, seq, caches, page_table, even_w_in, nsa_cmp_pos[e], diff_lambda[e],
                diff_norm[e], stacks)
            for lst, item in zip(small, small_e):
                lst.append(item)
            w_out, lead = even_w_out, e
        else:
            o = layer // 2
            mix_p, mix_s, st_p, st_s = _ret_layer(o, hp, hs, gain, batch, seq, state_ret[o], ret_w_in, ret_norm[o])
            ret_p.append(st_p)
            ret_s.append(st_s)
            w_out, lead = ret_w_out, o
        (xp, xp_bf), (xs, xs_bf) = dense(mix_p, w_out, d, lead=lead, res=xp, xs=mix_s, res_s=xs, out_dtypes=both)
        up_p, up_s = dense(xp_bf, ffn_w13, d_ff, lead=layer, col3=d_ff, out_dtypes=(BF16,), xs=xs_bf,
                           gain=norm_ffn[layer], row_norm=True)
        if layer + 1 < DEPTH:
            (xp, xp_bf), (xs, xs_bf) = dense(up_p, ffn_w2, d, lead=layer, res=xp, xs=up_s, res_s=xs, out_dtypes=both)
        else:
            xp, xs = dense(up_p, ffn_w2, d, lead=layer, res=xp, xs=up_s, res_s=xs)
    y_prompt = rmsnorm(xp, norm_final, F32).reshape(batch, seq, d)
    y_sample = rmsnorm(xs, norm_final, F32)[:db].reshape(db, 1, d)
    n_even = even_w_in.shape[0]
    cmp_st, sel_st, kvd_st = stacks
    win_p, cmp_s, sel_s, win_s, diff_s = [jnp.stack(t) for t in small]
    return (y_prompt, y_sample,
            cmp_st.reshape(n_even, batch, seq, 2, NSA_KVH, HEAD_DIM),
            sel_st.reshape(n_even, batch, seq, 2, NSA_KVH, HEAD_DIM), win_p,
            kvd_st.reshape(n_even, batch, seq, 2, DIFF_H, DIFF_VD), jnp.stack(ret_p),
            cmp_s, sel_s, win_s, diff_s, jnp.stack(ret_s))
```
